```python
import math
import numpy as np
import jax
import jax.numpy as jnp
from jax import lax

D_MODEL = 1024
BATCH = 4
SEQ = 4096
DEPTH = 1
DEC_BATCH = 128
DEC_SEQ = 1
PAST_LEN = 8192
PAGE_SIZE = 128

GDN_DK = 128
GDN_DV = 128
GDN_HEADS = D_MODEL // GDN_DV
GDN_QK = GDN_HEADS * GDN_DK
GDN_V = GDN_HEADS * GDN_DV
GDN_CONV_DIM = 2 * GDN_QK + GDN_V
GDN_CONV = 4
GDN_CHUNK = 64

NSA_HD = 128
NSA_HEADS = D_MODEL // NSA_HD
NSA_KV_HEADS = NSA_HEADS // 4
NSA_HPG = NSA_HEADS // NSA_KV_HEADS
NSA_Q = NSA_HEADS * NSA_HD
NSA_KV = NSA_KV_HEADS * NSA_HD
CMP_LEN = 32
CMP_STRIDE = 16
SLC_BLK = 64
SLC_TOPN = 16
WINDOW = 512
Q_BLOCK = 128
ATTN_SCALE = NSA_HD ** -0.5
FORCE_SCORE = 1e6

ROPE_DIM = NSA_HD // 4
ROPE_THETA = 500000.0

D_FF = ((8 * D_MODEL // 3 + 127) // 128) * 128
EPS = 1e-6

IN_SIZES = (GDN_CONV_DIM, GDN_HEADS, GDN_HEADS, GDN_V,
            NSA_Q, NSA_KV, NSA_KV, NSA_KV, NSA_KV, NSA_KV, NSA_KV, 3 * NSA_HEADS,
            D_MODEL, D_MODEL)
D_IN = sum(IN_SIZES)

kernel_name = 'gdn_nsa_macaron_hybrid_step'


def rms_norm(x, g):
    xf = x.astype(jnp.float32)
    y = xf * lax.rsqrt(jnp.mean(xf * xf, axis=-1, keepdims=True) + EPS)
    return (y * g.astype(jnp.float32)).astype(x.dtype)


def l2_norm(x):
    return x * lax.rsqrt(jnp.sum(x * x, axis=-1, keepdims=True) + EPS)


def rope_partial(x, pos):
    half = ROPE_DIM // 2
    inv = jnp.float32(ROPE_THETA) ** (-jnp.arange(half, dtype=jnp.float32) / half)
    ang = pos.astype(jnp.float32)[:, None] * inv
    shape = (pos.shape[0],) + (1,) * (x.ndim - 3) + (half,)
    cos = jnp.cos(ang).reshape(shape)
    sin = jnp.sin(ang).reshape(shape)
    xf = x.astype(jnp.float32)
    x1, x2, rest = xf[..., :half], xf[..., half:ROPE_DIM], xf[..., ROPE_DIM:]
    out = jnp.concatenate([x1 * cos - x2 * sin, x2 * cos + x1 * sin, rest], axis=-1)
    return out.astype(x.dtype)


def masked_softmax(s, mask, axis=-1):
    s = jnp.where(mask, s.astype(jnp.float32), -jnp.inf)
    m = jnp.max(s, axis=axis, keepdims=True)
    m = jnp.where(jnp.isfinite(m), m, 0.0)
    p = jnp.exp(s - m)
    return p / jnp.maximum(jnp.sum(p, axis=axis, keepdims=True), 1e-30)


def swiglu_half(x, g, w_in, w_out):
    h = rms_norm(x, g)
    gate, up = jnp.split(h @ w_in, 2, axis=-1)
    return (x + 0.5 * ((jax.nn.silu(gate) * up) @ w_out)).astype(x.dtype)


def split_cols(p, sizes):
    offs = np.cumsum((0,) + tuple(sizes))
    return [p[..., int(offs[i]):int(offs[i + 1])] for i in range(len(sizes))]


def mixer_inputs(x, lp):
    h = rms_norm(x, lp['mix_norm'])
    return split_cols(h @ lp['w_in'], IN_SIZES)


def causal_conv(x, buf, w):
    t = x.shape[1]
    xp = jnp.concatenate([buf.astype(x.dtype), x], axis=1)
    out = xp[:, 0:t] * w[0]
    for j in range(1, GDN_CONV):
        out = out + xp[:, j:j + t] * w[j]
    return jax.nn.silu(out), xp[:, t:]


def gated_delta_rule(q, k, v, g, beta, s0):
    b, t, h, dk = q.shape
    dv = v.shape[-1]
    c = min(GDN_CHUNK, t)
    pad = (-t) % c

    def prep(a):
        a = jnp.pad(a, [(0, 0), (0, pad)] + [(0, 0)] * (a.ndim - 2))
        a = jnp.moveaxis(a, 2, 1)
        return a.reshape((b, h, -1, c) + a.shape[3:])

    q, k, v, g, beta = [prep(a) for a in (q * dk ** -0.5, k, v, g, beta)]
    gc = jnp.cumsum(g, axis=-1)
    ii = jnp.arange(c)
    incl = ii[:, None] >= ii[None, :]
    strict = ii[:, None] > ii[None, :]
    diff = gc[..., :, None] - gc[..., None, :]
    decay = jnp.where(incl, jnp.exp(jnp.where(incl, diff, 0.0)), 0.0)
    kb = k * beta[..., None]
    m = jnp.where(strict, jnp.einsum('bhncd,bhnsd->bhncs', kb, k) * decay, 0.0)
    a_mat = m + jnp.eye(c, dtype=m.dtype)
    rhs = jnp.concatenate([v * beta[..., None], kb * jnp.exp(gc)[..., None]], axis=-1)
    sol = lax.linalg.triangular_solve(a_mat, rhs, left_side=True, lower=True, unit_diagonal=True)
    u, w = sol[..., :dv], sol[..., dv:]
    attn = jnp.where(incl, jnp.einsum('bhncd,bhnsd->bhncs', q, k) * decay, 0.0)

    def step(s, inp):
        q_c, k_c, u_c, w_c, gc_c, attn_c = inp
        v_new = u_c - jnp.einsum('bhcd,bhde->bhce', w_c, s)
        o = (jnp.einsum('bhcd,bhde->bhce', q_c * jnp.exp(gc_c)[..., None], s)
             + jnp.einsum('bhcs,bhse->bhce', attn_c, v_new))
        g_last = gc_c[..., -1]
        k_dec = k_c * jnp.exp(g_last[..., None] - gc_c)[..., None]
        s = s * jnp.exp(g_last)[..., None, None] + jnp.einsum('bhcd,bhce->bhde', k_dec, v_new)
        return s, o

    xs = tuple(jnp.moveaxis(a, 2, 0) for a in (q, k, u, w, gc, attn))
    s_fin, o = lax.scan(step, s0, xs)
    o = jnp.moveaxis(o, 0, 2).reshape(b, h, -1, dv)[:, :, :t]
    return jnp.moveaxis(o, 1, 2), s_fin


def gdn_mixer(qkv, a, b, z, conv_buf, rec, lp):
    bsz, t, _ = qkv.shape
    qkv_c, conv_new = causal_conv(qkv, conv_buf, lp['gdn_conv_w'])
    qkv_c = qkv_c.astype(jnp.float32)
    q = l2_norm(qkv_c[..., :GDN_QK].reshape(bsz, t, GDN_HEADS, GDN_DK))
    k = l2_norm(qkv_c[..., GDN_QK:2 * GDN_QK].reshape(bsz, t, GDN_HEADS, GDN_DK))
    v = qkv_c[..., 2 * GDN_QK:].reshape(bsz, t, GDN_HEADS, GDN_DV)
    g = -jnp.exp(lp['gdn_a_log'].astype(jnp.float32)) * jax.nn.softplus(
        a.astype(jnp.float32) + lp['gdn_dt_bias'].astype(jnp.float32))
    beta = jax.nn.sigmoid(b.astype(jnp.float32))
    o, rec_new = gated_delta_rule(q, k, v, g, beta, rec.astype(jnp.float32))
    zh = z.astype(jnp.float32).reshape(bsz, t, GDN_HEADS, GDN_DV)
    o = rms_norm(o, lp['gdn_out_norm']) * jax.nn.silu(zh)
    return o.reshape(bsz, t, GDN_V).astype(qkv.dtype), conv_new, rec_new.astype(rec.dtype)


def nsa_project(q, kc, vc, ks, vs, kw, vw, pos, lp):
    b, t, _ = q.shape

    def heads(a, n):
        return a.reshape(b, t, n, NSA_HD)

    qn = rms_norm(heads(q, NSA_HEADS), lp['nsa_q_norm']).reshape(b, t, NSA_KV_HEADS, NSA_HPG, NSA_HD)
    qr = rope_partial(qn, pos)
    ks_r = rope_partial(rms_norm(heads(ks, NSA_KV_HEADS), lp['nsa_k_norm_slc']), pos)
    kw_r = rope_partial(rms_norm(heads(kw, NSA_KV_HEADS), lp['nsa_k_norm_win']), pos)
    return (qn, qr, heads(kc, NSA_KV_HEADS), heads(vc, NSA_KV_HEADS), ks_r,
            heads(vs, NSA_KV_HEADS), kw_r, heads(vw, NSA_KV_HEADS))


def subblock_proj(rows, w1):
    b, l, g, hd = rows.shape
    nh = l // CMP_STRIDE
    sub = rows[:, :nh * CMP_STRIDE].reshape(b, nh, CMP_STRIDE, g, hd)
    w = w1.reshape(CMP_LEN // CMP_STRIDE, CMP_STRIDE, hd, hd)
    return jnp.einsum('bnsgd,rsde->rbnge', sub, w)


def compress(parts, w1, w2, pe):
    r_sub = parts.shape[0]
    nc = parts.shape[2] - r_sub + 1
    hid = parts[0, :, 0:nc]
    for r in range(1, r_sub):
        hid = hid + parts[r, :, r:r + nc]
    hid = hid + jnp.einsum('ld,lde->e', pe, w1)
    return jax.nn.silu(hid) @ w2


def compressed_kv(parts_k, parts_v, lp):
    kcc = rms_norm(compress(parts_k, lp['cmp_w1_k'], lp['cmp_w2_k'], lp['cmp_pe_k']), lp['nsa_k_norm_cmp'])
    vcc = compress(parts_v, lp['cmp_w1_v'], lp['cmp_w2_v'], lp['cmp_pe_v'])
    return kcc, vcc


def cmp_attention(q, q_pos, kc, vc):
    nc = kc.shape[1]
    end = jnp.arange(nc) * CMP_STRIDE + CMP_LEN - 1
    vis = end[None, :] <= q_pos[:, None]
    s = jnp.einsum('btghd,bcgd->btghc', q, kc) * ATTN_SCALE
    p = masked_softmax(s, vis[None, :, None, None, :], axis=-1)
    o = jnp.einsum('btghc,bcgd->btghd', p, vc)
    return o, jnp.sum(p, axis=3)


def select_blocks(imp, q_pos, n_keys):
    nc = imp.shape[-1]
    ns = -(-n_keys // SLC_BLK)
    cstart = jnp.arange(nc) * CMP_STRIDE
    sstart = jnp.arange(ns) * SLC_BLK
    overlap = (cstart[:, None] < sstart[None, :] + SLC_BLK) & (cstart[:, None] + CMP_LEN > sstart[None, :])
    score = imp @ overlap.astype(imp.dtype)
    cur = q_pos // SLC_BLK
    j = jnp.arange(ns)[None, :]
    valid = j <= cur[:, None]
    forced = (j == 0) | (j == cur[:, None]) | (j == cur[:, None] - 1)
    score = jnp.where(valid[None, :, None, :],
                      jnp.where(forced[None, :, None, :], FORCE_SCORE, score), -jnp.inf)
    vals, idx = lax.top_k(score, min(SLC_TOPN, ns))
    return idx, jnp.isfinite(vals)


def slc_attention(q, q_pos, idx, ok, kg, vg):
    s = jnp.einsum('btghd,btgnsd->btghns', q, kg) * ATTN_SCALE
    kpos = idx[..., None] * SLC_BLK + jnp.arange(SLC_BLK)
    mask = ok[..., None] & (kpos <= q_pos[None, :, None, None, None])
    p = masked_softmax(s, mask[:, :, :, None], axis=(-2, -1))
    return jnp.einsum('btghns,btgnsd->btghd', p, vg)


def slc_prompt(q, q_pos, idx, ok, ks, vs):
    b, s, g, hd = ks.shape
    ns = s // SLC_BLK
    kb = ks.reshape(b, ns, SLC_BLK, g, hd).transpose(0, 3, 1, 2, 4)
    vb = vs.reshape(b, ns, SLC_BLK, g, hd).transpose(0, 3, 1, 2, 4)
    bi = jnp.arange(b)[:, None, None, None]
    gi = jnp.arange(g)[None, None, :, None]
    nqb = s // Q_BLOCK

    def to_blocks(a):
        return jnp.moveaxis(a.reshape((a.shape[0], nqb, Q_BLOCK) + a.shape[2:]), 1, 0)

    def one_block(args):
        qb, pb, ib, ob = args
        return slc_attention(qb, pb, ib, ob, kb[bi, gi, ib], vb[bi, gi, ib])

    out = lax.map(one_block, (to_blocks(q), q_pos.reshape(nqb, Q_BLOCK), to_blocks(idx), to_blocks(ok)))
    return jnp.moveaxis(out, 0, 1).reshape(q.shape)


def slc_sample(q, q_pos, idx, ok, pool_k, pool_v, layer, page_table, k_new, v_new):
    db, t, g, hd = k_new.shape
    n_past = PAST_LEN // SLC_BLK
    bpp = PAGE_SIZE // SLC_BLK
    bi = jnp.arange(db)[:, None, None, None]
    gi = jnp.arange(g)[None, None, :, None]
    jc = jnp.minimum(idx, n_past - 1)
    page = page_table[bi, jc // bpp]
    row = (jc % bpp)[..., None] * SLC_BLK + jnp.arange(SLC_BLK)
    nnb = -(-t // SLC_BLK)
    jn = jnp.clip(idx - n_past, 0, nnb - 1)
    is_new = (idx >= n_past)[..., None, None]

    def gather(pool, new):
        from_pool = pool[layer, page[..., None], row, gi[..., None]]
        newb = jnp.pad(new, ((0, 0), (0, nnb * SLC_BLK - t), (0, 0), (0, 0)))
        newb = newb.reshape(db, nnb, SLC_BLK, g, hd).transpose(0, 3, 1, 2, 4)
        return jnp.where(is_new, newb[bi, gi, jn], from_pool)

    return slc_attention(q, q_pos, idx, ok, gather(pool_k, k_new), gather(pool_v, v_new))


def win_prompt(q, kw, vw):
    b, s, g, hpg, hd = q.shape
    nqb = s // Q_BLOCK
    nw = WINDOW // Q_BLOCK
    pad = ((0, 0), (WINDOW, 0), (0, 0), (0, 0))
    kb = jnp.pad(kw, pad).reshape(b, nw + nqb, Q_BLOCK, g, hd)
    vb = jnp.pad(vw, pad).reshape(b, nw + nqb, Q_BLOCK, g, hd)
    kband = jnp.concatenate([kb[:, j:j + nqb] for j in range(nw + 1)], axis=2)
    vband = jnp.concatenate([vb[:, j:j + nqb] for j in range(nw + 1)], axis=2)
    qb = q.reshape(b, nqb, Q_BLOCK, g, hpg, hd)
    sc = jnp.einsum('bnqghd,bnkgd->bnqghk', qb, kband) * ATTN_SCALE
    start = jnp.arange(nqb)[:, None] * Q_BLOCK
    qpos = start + jnp.arange(Q_BLOCK)
    kpos = start - WINDOW + jnp.arange((nw + 1) * Q_BLOCK)
    kp, qp = kpos[:, None, :], qpos[:, :, None]
    mask = (kp <= qp) & (kp > qp - WINDOW) & (kp >= 0)
    p = masked_softmax(sc, mask[None, :, :, None, None, :], axis=-1)
    o = jnp.einsum('bnqghk,bnkgd->bnqghd', p, vband)
    return o.reshape(b, s, g, hpg, hd)


def win_sample(q, q_pos, buf_k, buf_v, k_new, v_new):
    wb = buf_k.shape[1]
    t = k_new.shape[1]
    ka = jnp.concatenate([buf_k.astype(k_new.dtype), k_new], axis=1)
    va = jnp.concatenate([buf_v.astype(v_new.dtype), v_new], axis=1)
    kpos = PAST_LEN - wb + jnp.arange(wb + t)
    mask = (kpos[None, :] <= q_pos[:, None]) & (kpos[None, :] > q_pos[:, None] - WINDOW)
    sc = jnp.einsum('btghd,bkgd->btghk', q, ka) * ATTN_SCALE
    p = masked_softmax(sc, mask[None, :, None, None, :], axis=-1)
    o = jnp.einsum('btghk,bkgd->btghd', p, va)
    return o, ka[:, t:], va[:, t:]


def nsa_combine(gates, o_c, o_s, o_w):
    b, t, _ = gates.shape
    gt = jax.nn.sigmoid(gates.astype(jnp.float32)).reshape(b, t, 3, NSA_KV_HEADS, NSA_HPG, 1)
    o = gt[:, :, 0] * o_c + gt[:, :, 1] * o_s + gt[:, :, 2] * o_w
    return o.reshape(b, t, NSA_Q)


def merge_branches(x, o_a, o_b, ga, gb, w_o):
    m = (jax.nn.sigmoid(ga.astype(jnp.float32)) * o_a.astype(jnp.float32)
         + jax.nn.sigmoid(gb.astype(jnp.float32)) * o_b.astype(jnp.float32))
    return (x + m.astype(x.dtype) @ w_o).astype(x.dtype)


def gather_pages(pool, layer, page_table):
    rows = pool[layer, page_table]
    return rows.reshape(rows.shape[0], -1, rows.shape[3], rows.shape[4])


def token_mix_prompt(x, lp):
    b, s, _ = x.shape
    pos = jnp.arange(s)
    (qkv, a, bb, z, q, kc, vc, ks, vs, kw, vw, nsa_g, ga, gb) = mixer_inputs(x, lp)
    conv0 = jnp.zeros((b, GDN_CONV - 1, GDN_CONV_DIM), x.dtype)
    rec0 = jnp.zeros((b, GDN_HEADS, GDN_DK, GDN_DV), jnp.float32)
    o_a, conv_new, rec_new = gdn_mixer(qkv, a, bb, z, conv0, rec0, lp)
    qn, qr, kc_h, vc_h, ks_r, vs_h, kw_r, vw_h = nsa_project(q, kc, vc, ks, vs, kw, vw, pos, lp)
    kcc, vcc = compressed_kv(subblock_proj(kc_h, lp['cmp_w1_k']), subblock_proj(vc_h, lp['cmp_w1_v']), lp)
    o_c, imp = cmp_attention(qn, pos, kcc, vcc)
    idx, ok = select_blocks(imp, pos, s)
    o_s = slc_prompt(qr, pos, idx, ok, ks_r, vs_h)
    o_w = win_prompt(qr, kw_r, vw_h)
    o_b = nsa_combine(nsa_g, o_c, o_s, o_w)
    y = merge_branches(x, o_a, o_b, ga, gb, lp['w_o'])
    wb = min(WINDOW, s)
    return y, (conv_new, rec_new, kc_h, vc_h, ks_r, vs_h, kw_r[:, s - wb:], vw_h[:, s - wb:])


def token_mix_sample(x, layer, conv_buf, rec, cache_k_cmp, cache_v_cmp, cache_k_slc, cache_v_slc,
                     win_k, win_v, page_table, lp):
    db, t, _ = x.shape
    pos = PAST_LEN + jnp.arange(t)
    (qkv, a, bb, z, q, kc, vc, ks, vs, kw, vw, nsa_g, ga, gb) = mixer_inputs(x, lp)
    o_a, conv_new, rec_new = gdn_mixer(qkv, a, bb, z, conv_buf, rec, lp)
    qn, qr, kc_h, vc_h, ks_r, vs_h, kw_r, vw_h = nsa_project(q, kc, vc, ks, vs, kw, vw, pos, lp)
    parts_k = jnp.concatenate([subblock_proj(gather_pages(cache_k_cmp, layer, page_table), lp['cmp_w1_k']),
                               subblock_proj(kc_h, lp['cmp_w1_k'])], axis=2)
    parts_v = jnp.concatenate([subblock_proj(gather_pages(cache_v_cmp, layer, page_table), lp['cmp_w1_v']),
                               subblock_proj(vc_h, lp['cmp_w1_v'])], axis=2)
    kcc, vcc = compressed_kv(parts_k, parts_v, lp)
    o_c, imp = cmp_attention(qn, pos, kcc, vcc)
    idx, ok = select_blocks(imp, pos, PAST_LEN + t)
    o_s = slc_sample(qr, pos, idx, ok, cache_k_slc, cache_v_slc, layer, page_table, ks_r, vs_h)
    o_w, win_k_new, win_v_new = win_sample(qr, pos, win_k, win_v, kw_r, vw_h)
    o_b = nsa_combine(nsa_g, o_c, o_s, o_w)
    y = merge_branches(x, o_a, o_b, ga, gb, lp['w_o'])
    return y, (conv_new, rec_new, kc_h, vc_h, ks_r, vs_h, win_k_new, win_v_new)


def setup_inputs(seed: int = 0) -> dict:
    key = jax.random.key(seed)
    ks = iter(list(jax.random.split(key, 40)))
    f32 = jnp.float32
    n_pages = PAST_LEN // PAGE_SIZE
    n_pool = (5 * DEC_BATCH * n_pages + 3) // 4
    wb = min(WINDOW, PAST_LEN)

    def nrm(shape, scale):
        return jax.random.normal(next(ks), shape, f32) * scale

    def gain(n):
        return 1.0 + nrm((DEPTH, n), 0.02)

    pool = (DEPTH, n_pool, PAGE_SIZE, NSA_KV_HEADS, NSA_HD)
    win = (DEPTH, DEC_BATCH, wb, NSA_KV_HEADS, NSA_HD)
    inp = {}
    inp['x_prompt'] = nrm((BATCH, SEQ, D_MODEL), 1.0)
    inp['x_sample'] = nrm((DEC_BATCH, DEC_SEQ, D_MODEL), 1.0)
    inp['state_gdn_conv'] = nrm((DEPTH, DEC_BATCH, GDN_CONV - 1, GDN_CONV_DIM), 1.0)
    inp['state_gdn_rec'] = nrm((DEPTH, DEC_BATCH, GDN_HEADS, GDN_DK, GDN_DV), 0.5)
    inp['cache_k_cmp'] = nrm(pool, 1.0)
    inp['cache_v_cmp'] = nrm(pool, 1.0)
    inp['cache_k_slc'] = nrm(pool, 1.0)
    inp['cache_v_slc'] = nrm(pool, 1.0)
    inp['cache_k_win'] = nrm(win, 1.0)
    inp['cache_v_win'] = nrm(win, 1.0)
    perm = jax.random.permutation(next(ks), n_pool)
    inp['page_table'] = perm[:DEC_BATCH * n_pages].reshape(DEC_BATCH, n_pages).astype(jnp.int32)
    inp['ffn1_norm'] = gain(D_MODEL)
    inp['ffn1_w_in'] = nrm((DEPTH, D_MODEL, 2 * D_FF), D_MODEL ** -0.5)
    inp['ffn1_w_out'] = nrm((DEPTH, D_FF, D_MODEL), D_FF ** -0.5)
    inp['mix_norm'] = gain(D_MODEL)
    inp['w_in'] = nrm((DEPTH, D_MODEL, D_IN), D_MODEL ** -0.5)
    inp['gdn_conv_w'] = nrm((DEPTH, GDN_CONV, GDN_CONV_DIM), GDN_CONV ** -0.5)
    inp['gdn_a_log'] = jnp.log(jax.random.uniform(next(ks), (DEPTH, GDN_HEADS), f32, 1.0, 16.0))
    dt = jnp.exp(jax.random.uniform(next(ks), (DEPTH, GDN_HEADS), f32, math.log(1e-3), math.log(1e-1)))
    inp['gdn_dt_bias'] = dt + jnp.log(-jnp.expm1(-dt))
    inp['gdn_out_norm'] = gain(GDN_DV)
    inp['nsa_q_norm'] = gain(NSA_HD)
    inp['nsa_k_norm_cmp'] = gain(NSA_HD)
    inp['nsa_k_norm_slc'] = gain(NSA_HD)
    inp['nsa_k_norm_win'] = gain(NSA_HD)
    inp['cmp_w1_k'] = nrm((DEPTH, CMP_LEN, NSA_HD, NSA_HD), (CMP_LEN * NSA_HD) ** -0.5)
    inp['cmp_w2_k'] = nrm((DEPTH, NSA_HD, NSA_HD), NSA_HD ** -0.5)
    inp['cmp_pe_k'] = nrm((DEPTH, CMP_LEN, NSA_HD), 0.1)
    inp['cmp_w1_v'] = nrm((DEPTH, CMP_LEN, NSA_HD, NSA_HD), (CMP_LEN * NSA_HD) ** -0.5)
    inp['cmp_w2_v'] = nrm((DEPTH, NSA_HD, NSA_HD), NSA_HD ** -0.5)
    inp['cmp_pe_v'] = nrm((DEPTH, CMP_LEN, NSA_HD), 0.1)
    inp['w_o'] = nrm((DEPTH, D_MODEL, D_MODEL), D_MODEL ** -0.5)
    inp['ffn2_norm'] = gain(D_MODEL)
    inp['ffn2_w_in'] = nrm((DEPTH, D_MODEL, 2 * D_FF), D_MODEL ** -0.5)
    inp['ffn2_w_out'] = nrm((DEPTH, D_FF, D_MODEL), D_FF ** -0.5)
    return inp


def reference(x_prompt, x_sample, state_gdn_conv, state_gdn_rec, cache_k_cmp, cache_v_cmp,
              cache_k_slc, cache_v_slc, cache_k_win, cache_v_win, page_table,
              ffn1_norm, ffn1_w_in, ffn1_w_out, mix_norm, w_in, gdn_conv_w, gdn_a_log, gdn_dt_bias,
              gdn_out_norm, nsa_q_norm, nsa_k_norm_cmp, nsa_k_norm_slc, nsa_k_norm_win,
              cmp_w1_k, cmp_w2_k, cmp_pe_k, cmp_w1_v, cmp_w2_v, cmp_pe_v, w_o,
              ffn2_norm, ffn2_w_in, ffn2_w_out):
    yp, ys = x_prompt, x_sample
    p_list, s_list = [], []
    for l in range(DEPTH):
        lp = {'mix_norm': mix_norm[l], 'w_in': w_in[l], 'gdn_conv_w': gdn_conv_w[l],
              'gdn_a_log': gdn_a_log[l], 'gdn_dt_bias': gdn_dt_bias[l], 'gdn_out_norm': gdn_out_norm[l],
              'nsa_q_norm': nsa_q_norm[l], 'nsa_k_norm_cmp': nsa_k_norm_cmp[l],
              'nsa_k_norm_slc': nsa_k_norm_slc[l], 'nsa_k_norm_win': nsa_k_norm_win[l],
              'cmp_w1_k': cmp_w1_k[l], 'cmp_w2_k': cmp_w2_k[l], 'cmp_pe_k': cmp_pe_k[l],
              'cmp_w1_v': cmp_w1_v[l], 'cmp_w2_v': cmp_w2_v[l], 'cmp_pe_v': cmp_pe_v[l], 'w_o': w_o[l]}
        yp = swiglu_half(yp, ffn1_norm[l], ffn1_w_in[l], ffn1_w_out[l])
        yp, st_p = token_mix_prompt(yp, lp)
        yp = swiglu_half(yp, ffn2_norm[l], ffn2_w_in[l], ffn2_w_out[l])
        ys = swiglu_half(ys, ffn1_norm[l], ffn1_w_in[l], ffn1_w_out[l])
        ys, st_s = token_mix_sample(ys, l, state_gdn_conv[l], state_gdn_rec[l], cache_k_cmp, cache_v_cmp,
                                    cache_k_slc, cache_v_slc, cache_k_win[l], cache_v_win[l], page_table, lp)
        ys = swiglu_half(ys, ffn2_norm[l], ffn2_w_in[l], ffn2_w_out[l])
        p_list.append(st_p)
        s_list.append(st_s)
    (p_conv, p_rec, p_kc, p_vc, p_ks, p_vs, p_kw, p_vw) = [jnp.stack(a) for a in zip(*p_list)]
    (s_conv, s_rec, s_kc, s_vc, s_ks, s_vs, s_kw, s_vw) = [jnp.stack(a) for a in zip(*s_list)]
    return (yp, ys, p_conv, p_rec, p_kc, p_vc, p_ks, p_vs, p_kw, p_vw,
            s_conv, s_rec, s_kc, s_vc, s_ks, s_vs, s_kw, s_vw)
```

```python
import functools
import math

import numpy as np
import jax
import jax.numpy as jnp
from jax import lax
from jax.experimental import pallas as pl
from jax.experimental.pallas import tpu as pltpu

D_MODEL = 1024
PAST_LEN = 8192
PAGE_SIZE = 128

GDN_DK = 128
GDN_DV = 128
GDN_HEADS = D_MODEL // GDN_DV
GDN_QK = GDN_HEADS * GDN_DK
GDN_V = GDN_HEADS * GDN_DV
GDN_CONV_DIM = 2 * GDN_QK + GDN_V
GDN_CONV = 4
GDN_CHUNK = 64

NSA_HD = 128
NSA_HEADS = D_MODEL // NSA_HD
NSA_KV_HEADS = NSA_HEADS // 4
NSA_HPG = NSA_HEADS // NSA_KV_HEADS
NSA_Q = NSA_HEADS * NSA_HD
NSA_KV = NSA_KV_HEADS * NSA_HD
CMP_LEN = 32
CMP_STRIDE = 16
SLC_BLK = 64
SLC_TOPN = 16
WINDOW = 512
Q_BLOCK = 128
ATTN_SCALE = NSA_HD ** -0.5
FORCE_SCORE = 1e6

ROPE_DIM = NSA_HD // 4
ROPE_THETA = 500000.0

D_FF = ((8 * D_MODEL // 3 + 127) // 128) * 128
EPS = 1e-6

IN_SIZES = (GDN_CONV_DIM, GDN_HEADS, GDN_HEADS, GDN_V,
            NSA_Q, NSA_KV, NSA_KV, NSA_KV, NSA_KV, NSA_KV, NSA_KV, 3 * NSA_HEADS,
            D_MODEL, D_MODEL)

MXU_N = 256
VMEM_LIMIT_BYTES = 56 * 1024 * 1024

BF16 = jnp.bfloat16
F32 = jnp.float32


def _const_spec(shape):
    return pl.BlockSpec(shape, lambda *_: (0,) * len(shape), pipeline_mode=pl.Buffered(1))


def _ffn_body(x_ref, g_ref, wg_ref, wu_ref, wo_ref, o_ref):
    x = x_ref[...]
    ms = jnp.mean(x * x, axis=-1, keepdims=True)
    h = (x * lax.rsqrt(ms + EPS) * g_ref[...]).astype(BF16)
    acc = jnp.zeros(x.shape, F32)
    for c in range(D_FF // MXU_N):
        cols = slice(c * MXU_N, (c + 1) * MXU_N)
        gate = jnp.dot(h, wg_ref[:, cols], preferred_element_type=F32)
        up = jnp.dot(h, wu_ref[:, cols], preferred_element_type=F32)
        act = (gate * jax.nn.sigmoid(gate) * up).astype(BF16)
        acc = acc + jnp.dot(act, wo_ref[cols, :], preferred_element_type=F32)
    o_ref[...] = x + 0.5 * acc


def _ffn_tile(n_rows):
    return min(n_rows, 512)


def swiglu_half(x, g, w_in, w_out):
    shape = x.shape
    x2 = x.reshape(-1, D_MODEL)
    n = x2.shape[0]
    tm = _ffn_tile(n)
    wg = w_in[:, :D_FF].astype(BF16)
    wu = w_in[:, D_FF:].astype(BF16)
    wo = w_out.astype(BF16)
    out = pl.pallas_call(
        _ffn_body,
        grid=(n // tm,),
        in_specs=[pl.BlockSpec((tm, D_MODEL), lambda i: (i, 0)),
                  _const_spec((1, D_MODEL)),
                  _const_spec((D_MODEL, D_FF)),
                  _const_spec((D_MODEL, D_FF)),
                  _const_spec((D_FF, D_MODEL))],
        out_specs=pl.BlockSpec((tm, D_MODEL), lambda i: (i, 0)),
        out_shape=jax.ShapeDtypeStruct((n, D_MODEL), F32),
        compiler_params=pltpu.CompilerParams(dimension_semantics=("arbitrary",),
                                             vmem_limit_bytes=VMEM_LIMIT_BYTES),
        name="swiglu_half",
    )(x2, g.reshape(1, D_MODEL), wg, wu, wo)
    return out.reshape(shape)


def rms_norm(x, g):
    xf = x.astype(jnp.float32)
    y = xf * lax.rsqrt(jnp.mean(xf * xf, axis=-1, keepdims=True) + EPS)
    return (y * g.astype(jnp.float32)).astype(x.dtype)


def l2_norm(x):
    return x * lax.rsqrt(jnp.sum(x * x, axis=-1, keepdims=True) + EPS)


def rope_partial(x, pos):
    half = ROPE_DIM // 2
    inv = jnp.float32(ROPE_THETA) ** (-jnp.arange(half, dtype=jnp.float32) / half)
    ang = pos.astype(jnp.float32)[:, None] * inv
    shape = (pos.shape[0],) + (1,) * (x.ndim - 3) + (half,)
    cos = jnp.cos(ang).reshape(shape)
    sin = jnp.sin(ang).reshape(shape)
    xf = x.astype(jnp.float32)
    x1, x2, rest = xf[..., :half], xf[..., half:ROPE_DIM], xf[..., ROPE_DIM:]
    out = jnp.concatenate([x1 * cos - x2 * sin, x2 * cos + x1 * sin, rest], axis=-1)
    return out.astype(x.dtype)


def masked_softmax(s, mask, axis=-1):
    s = jnp.where(mask, s.astype(jnp.float32), -jnp.inf)
    m = jnp.max(s, axis=axis, keepdims=True)
    m = jnp.where(jnp.isfinite(m), m, 0.0)
    p = jnp.exp(s - m)
    return p / jnp.maximum(jnp.sum(p, axis=axis, keepdims=True), 1e-30)


def split_cols(p, sizes):
    offs = np.cumsum((0,) + tuple(sizes))
    return [p[..., int(offs[i]):int(offs[i + 1])] for i in range(len(sizes))]


def mixer_inputs(x, lp):
    h = rms_norm(x, lp['mix_norm'])
    return split_cols(h @ lp['w_in'], IN_SIZES)


def causal_conv(x, buf, w):
    t = x.shape[1]
    xp = jnp.concatenate([buf.astype(x.dtype), x], axis=1)
    out = xp[:, 0:t] * w[0]
    for j in range(1, GDN_CONV):
        out = out + xp[:, j:j + t] * w[j]
    return jax.nn.silu(out), xp[:, t:]


def gated_delta_rule(q, k, v, g, beta, s0):
    b, t, h, dk = q.shape
    dv = v.shape[-1]
    c = min(GDN_CHUNK, t)
    pad = (-t) % c

    def prep(a):
        a = jnp.pad(a, [(0, 0), (0, pad)] + [(0, 0)] * (a.ndim - 2))
        a = jnp.moveaxis(a, 2, 1)
        return a.reshape((b, h, -1, c) + a.shape[3:])

    q, k, v, g, beta = [prep(a) for a in (q * dk ** -0.5, k, v, g, beta)]
    gc = jnp.cumsum(g, axis=-1)
    ii = jnp.arange(c)
    incl = ii[:, None] >= ii[None, :]
    strict = ii[:, None] > ii[None, :]
    diff = gc[..., :, None] - gc[..., None, :]
    decay = jnp.where(incl, jnp.exp(jnp.where(incl, diff, 0.0)), 0.0)
    kb = k * beta[..., None]
    m = jnp.where(strict, jnp.einsum('bhncd,bhnsd->bhncs', kb, k) * decay, 0.0)
    a_mat = m + jnp.eye(c, dtype=m.dtype)
    rhs = jnp.concatenate([v * beta[..., None], kb * jnp.exp(gc)[..., None]], axis=-1)
    sol = lax.linalg.triangular_solve(a_mat, rhs, left_side=True, lower=True, unit_diagonal=True)
    u, w = sol[..., :dv], sol[..., dv:]
    attn = jnp.where(incl, jnp.einsum('bhncd,bhnsd->bhncs', q, k) * decay, 0.0)

    def step(s, inp):
        q_c, k_c, u_c, w_c, gc_c, attn_c = inp
        v_new = u_c - jnp.einsum('bhcd,bhde->bhce', w_c, s)
        o = (jnp.einsum('bhcd,bhde->bhce', q_c * jnp.exp(gc_c)[..., None], s)
             + jnp.einsum('bhcs,bhse->bhce', attn_c, v_new))
        g_last = gc_c[..., -1]
        k_dec = k_c * jnp.exp(g_last[..., None] - gc_c)[..., None]
        s = s * jnp.exp(g_last)[..., None, None] + jnp.einsum('bhcd,bhce->bhde', k_dec, v_new)
        return s, o

    xs = tuple(jnp.moveaxis(a, 2, 0) for a in (q, k, u, w, gc, attn))
    s_fin, o = lax.scan(step, s0, xs)
    o = jnp.moveaxis(o, 0, 2).reshape(b, h, -1, dv)[:, :, :t]
    return jnp.moveaxis(o, 1, 2), s_fin


def gdn_mixer(qkv, a, b, z, conv_buf, rec, lp):
    bsz, t, _ = qkv.shape
    qkv_c, conv_new = causal_conv(qkv, conv_buf, lp['gdn_conv_w'])
    qkv_c = qkv_c.astype(jnp.float32)
    q = l2_norm(qkv_c[..., :GDN_QK].reshape(bsz, t, GDN_HEADS, GDN_DK))
    k = l2_norm(qkv_c[..., GDN_QK:2 * GDN_QK].reshape(bsz, t, GDN_HEADS, GDN_DK))
    v = qkv_c[..., 2 * GDN_QK:].reshape(bsz, t, GDN_HEADS, GDN_DV)
    g = -jnp.exp(lp['gdn_a_log'].astype(jnp.float32)) * jax.nn.softplus(
        a.astype(jnp.float32) + lp['gdn_dt_bias'].astype(jnp.float32))
    beta = jax.nn.sigmoid(b.astype(jnp.float32))
    o, rec_new = gated_delta_rule(q, k, v, g, beta, rec.astype(jnp.float32))
    zh = z.astype(jnp.float32).reshape(bsz, t, GDN_HEADS, GDN_DV)
    o = rms_norm(o, lp['gdn_out_norm']) * jax.nn.silu(zh)
    return o.reshape(bsz, t, GDN_V).astype(qkv.dtype), conv_new, rec_new.astype(rec.dtype)


def nsa_project(q, kc, vc, ks, vs, kw, vw, pos, lp):
    b, t, _ = q.shape

    def heads(a, n):
        return a.reshape(b, t, n, NSA_HD)

    qn = rms_norm(heads(q, NSA_HEADS), lp['nsa_q_norm']).reshape(b, t, NSA_KV_HEADS, NSA_HPG, NSA_HD)
    qr = rope_partial(qn, pos)
    ks_r = rope_partial(rms_norm(heads(ks, NSA_KV_HEADS), lp['nsa_k_norm_slc']), pos)
    kw_r = rope_partial(rms_norm(heads(kw, NSA_KV_HEADS), lp['nsa_k_norm_win']), pos)
    return (qn, qr, heads(kc, NSA_KV_HEADS), heads(vc, NSA_KV_HEADS), ks_r,
            heads(vs, NSA_KV_HEADS), kw_r, heads(vw, NSA_KV_HEADS))


def subblock_proj(rows, w1):
    b, l, g, hd = rows.shape
    nh = l // CMP_STRIDE
    sub = rows[:, :nh * CMP_STRIDE].reshape(b, nh, CMP_STRIDE, g, hd)
    w = w1.reshape(CMP_LEN // CMP_STRIDE, CMP_STRIDE, hd, hd)
    return jnp.einsum('bnsgd,rsde->rbnge', sub, w)


def compress(parts, w1, w2, pe):
    r_sub = parts.shape[0]
    nc = parts.shape[2] - r_sub + 1
    hid = parts[0, :, 0:nc]
    for r in range(1, r_sub):
        hid = hid + parts[r, :, r:r + nc]
    hid = hid + jnp.einsum('ld,lde->e', pe, w1)
    return jax.nn.silu(hid) @ w2


def compressed_kv(parts_k, parts_v, lp):
    kcc = rms_norm(compress(parts_k, lp['cmp_w1_k'], lp['cmp_w2_k'], lp['cmp_pe_k']), lp['nsa_k_norm_cmp'])
    vcc = compress(parts_v, lp['cmp_w1_v'], lp['cmp_w2_v'], lp['cmp_pe_v'])
    return kcc, vcc


def cmp_attention(q, q_pos, kc, vc):
    nc = kc.shape[1]
    end = jnp.arange(nc) * CMP_STRIDE + CMP_LEN - 1
    vis = end[None, :] <= q_pos[:, None]
    s = jnp.einsum('btghd,bcgd->btghc', q, kc) * ATTN_SCALE
    p = masked_softmax(s, vis[None, :, None, None, :], axis=-1)
    o = jnp.einsum('btghc,bcgd->btghd', p, vc)
    return o, jnp.sum(p, axis=3)


def select_blocks(imp, q_pos, n_keys):
    nc = imp.shape[-1]
    ns = -(-n_keys // SLC_BLK)
    cstart = jnp.arange(nc) * CMP_STRIDE
    sstart = jnp.arange(ns) * SLC_BLK
    overlap = (cstart[:, None] < sstart[None, :] + SLC_BLK) & (cstart[:, None] + CMP_LEN > sstart[None, :])
    score = imp @ overlap.astype(imp.dtype)
    cur = q_pos // SLC_BLK
    j = jnp.arange(ns)[None, :]
    valid = j <= cur[:, None]
    forced = (j == 0) | (j == cur[:, None]) | (j == cur[:, None] - 1)
    score = jnp.where(valid[None, :, None, :],
                      jnp.where(forced[None, :, None, :], FORCE_SCORE, score), -jnp.inf)
    vals, idx = lax.top_k(score, min(SLC_TOPN, ns))
    return idx, jnp.isfinite(vals)


def slc_attention(q, q_pos, idx, ok, kg, vg):
    s = jnp.einsum('btghd,btgnsd->btghns', q, kg) * ATTN_SCALE
    kpos = idx[..., None] * SLC_BLK + jnp.arange(SLC_BLK)
    mask = ok[..., None] & (kpos <= q_pos[None, :, None, None, None])
    p = masked_softmax(s, mask[:, :, :, None], axis=(-2, -1))
    return jnp.einsum('btghns,btgnsd->btghd', p, vg)


def slc_prompt(q, q_pos, idx, ok, ks, vs):
    b, s, g, hd = ks.shape
    ns = s // SLC_BLK
    kb = ks.reshape(b, ns, SLC_BLK, g, hd).transpose(0, 3, 1, 2, 4)
    vb = vs.reshape(b, ns, SLC_BLK, g, hd).transpose(0, 3, 1, 2, 4)
    bi = jnp.arange(b)[:, None, None, None]
    gi = jnp.arange(g)[None, None, :, None]
    nqb = s // Q_BLOCK

    def to_blocks(a):
        return jnp.moveaxis(a.reshape((a.shape[0], nqb, Q_BLOCK) + a.shape[2:]), 1, 0)

    def one_block(args):
        qb, pb, ib, ob = args
        return slc_attention(qb, pb, ib, ob, kb[bi, gi, ib], vb[bi, gi, ib])

    out = lax.map(one_block, (to_blocks(q), q_pos.reshape(nqb, Q_BLOCK), to_blocks(idx), to_blocks(ok)))
    return jnp.moveaxis(out, 0, 1).reshape(q.shape)


def slc_sample(q, q_pos, idx, ok, pool_k, pool_v, layer, page_table, k_new, v_new):
    db, t, g, hd = k_new.shape
    n_past = PAST_LEN // SLC_BLK
    bpp = PAGE_SIZE // SLC_BLK
    bi = jnp.arange(db)[:, None, None, None]
    gi = jnp.arange(g)[None, None, :, None]
    jc = jnp.minimum(idx, n_past - 1)
    page = page_table[bi, jc // bpp]
    row = (jc % bpp)[..., None] * SLC_BLK + jnp.arange(SLC_BLK)
    nnb = -(-t // SLC_BLK)
    jn = jnp.clip(idx - n_past, 0, nnb - 1)
    is_new = (idx >= n_past)[..., None, None]

    def gather(pool, new):
        from_pool = pool[layer, page[..., None], row, gi[..., None]]
        newb = jnp.pad(new, ((0, 0), (0, nnb * SLC_BLK - t), (0, 0), (0, 0)))
        newb = newb.reshape(db, nnb, SLC_BLK, g, hd).transpose(0, 3, 1, 2, 4)
        return jnp.where(is_new, newb[bi, gi, jn], from_pool)

    return slc_attention(q, q_pos, idx, ok, gather(pool_k, k_new), gather(pool_v, v_new))


def win_prompt(q, kw, vw):
    b, s, g, hpg, hd = q.shape
    nqb = s // Q_BLOCK
    nw = WINDOW // Q_BLOCK
    pad = ((0, 0), (WINDOW, 0), (0, 0), (0, 0))
    kb = jnp.pad(kw, pad).reshape(b, nw + nqb, Q_BLOCK, g, hd)
    vb = jnp.pad(vw, pad).reshape(b, nw + nqb, Q_BLOCK, g, hd)
    kband = jnp.concatenate([kb[:, j:j + nqb] for j in range(nw + 1)], axis=2)
    vband = jnp.concatenate([vb[:, j:j + nqb] for j in range(nw + 1)], axis=2)
    qb = q.reshape(b, nqb, Q_BLOCK, g, hpg, hd)
    sc = jnp.einsum('bnqghd,bnkgd->bnqghk', qb, kband) * ATTN_SCALE
    start = jnp.arange(nqb)[:, None] * Q_BLOCK
    qpos = start + jnp.arange(Q_BLOCK)
    kpos = start - WINDOW + jnp.arange((nw + 1) * Q_BLOCK)
    kp, qp = kpos[:, None, :], qpos[:, :, None]
    mask = (kp <= qp) & (kp > qp - WINDOW) & (kp >= 0)
    p = masked_softmax(sc, mask[None, :, :, None, None, :], axis=-1)
    o = jnp.einsum('bnqghk,bnkgd->bnqghd', p, vband)
    return o.reshape(b, s, g, hpg, hd)


def win_sample(q, q_pos, buf_k, buf_v, k_new, v_new):
    wb = buf_k.shape[1]
    t = k_new.shape[1]
    ka = jnp.concatenate([buf_k.astype(k_new.dtype), k_new], axis=1)
    va = jnp.concatenate([buf_v.astype(v_new.dtype), v_new], axis=1)
    kpos = PAST_LEN - wb + jnp.arange(wb + t)
    mask = (kpos[None, :] <= q_pos[:, None]) & (kpos[None, :] > q_pos[:, None] - WINDOW)
    sc = jnp.einsum('btghd,bkgd->btghk', q, ka) * ATTN_SCALE
    p = masked_softmax(sc, mask[None, :, None, None, :], axis=-1)
    o = jnp.einsum('btghk,bkgd->btghd', p, va)
    return o, ka[:, t:], va[:, t:]


def nsa_combine(gates, o_c, o_s, o_w):
    b, t, _ = gates.shape
    gt = jax.nn.sigmoid(gates.astype(jnp.float32)).reshape(b, t, 3, NSA_KV_HEADS, NSA_HPG, 1)
    o = gt[:, :, 0] * o_c + gt[:, :, 1] * o_s + gt[:, :, 2] * o_w
    return o.reshape(b, t, NSA_Q)


def merge_branches(x, o_a, o_b, ga, gb, w_o):
    m = (jax.nn.sigmoid(ga.astype(jnp.float32)) * o_a.astype(jnp.float32)
         + jax.nn.sigmoid(gb.astype(jnp.float32)) * o_b.astype(jnp.float32))
    return (x + m.astype(x.dtype) @ w_o).astype(x.dtype)


def gather_pages(pool, layer, page_table):
    rows = pool[layer, page_table]
    return rows.reshape(rows.shape[0], -1, rows.shape[3], rows.shape[4])


def token_mix_prompt(x, lp):
    b, s, _ = x.shape
    pos = jnp.arange(s)
    (qkv, a, bb, z, q, kc, vc, ks, vs, kw, vw, nsa_g, ga, gb) = mixer_inputs(x, lp)
    conv0 = jnp.zeros((b, GDN_CONV - 1, GDN_CONV_DIM), x.dtype)
    rec0 = jnp.zeros((b, GDN_HEADS, GDN_DK, GDN_DV), jnp.float32)
    o_a, conv_new, rec_new = gdn_mixer(qkv, a, bb, z, conv0, rec0, lp)
    qn, qr, kc_h, vc_h, ks_r, vs_h, kw_r, vw_h = nsa_project(q, kc, vc, ks, vs, kw, vw, pos, lp)
    kcc, vcc = compressed_kv(subblock_proj(kc_h, lp['cmp_w1_k']), subblock_proj(vc_h, lp['cmp_w1_v']), lp)
    o_c, imp = cmp_attention(qn, pos, kcc, vcc)
    idx, ok = select_blocks(imp, pos, s)
    o_s = slc_prompt(qr, pos, idx, ok, ks_r, vs_h)
    o_w = win_prompt(qr, kw_r, vw_h)
    o_b = nsa_combine(nsa_g, o_c, o_s, o_w)
    y = merge_branches(x, o_a, o_b, ga, gb, lp['w_o'])
    wb = min(WINDOW, s)
    return y, (conv_new, rec_new, kc_h, vc_h, ks_r, vs_h, kw_r[:, s - wb:], vw_h[:, s - wb:])


def token_mix_sample(x, layer, conv_buf, rec, cache_k_cmp, cache_v_cmp, cache_k_slc, cache_v_slc,
                     win_k, win_v, page_table, lp):
    db, t, _ = x.shape
    pos = PAST_LEN + jnp.arange(t)
    (qkv, a, bb, z, q, kc, vc, ks, vs, kw, vw, nsa_g, ga, gb) = mixer_inputs(x, lp)
    o_a, conv_new, rec_new = gdn_mixer(qkv, a, bb, z, conv_buf, rec, lp)
    qn, qr, kc_h, vc_h, ks_r, vs_h, kw_r, vw_h = nsa_project(q, kc, vc, ks, vs, kw, vw, pos, lp)
    parts_k = jnp.concatenate([subblock_proj(gather_pages(cache_k_cmp, layer, page_table), lp['cmp_w1_k']),
                               subblock_proj(kc_h, lp['cmp_w1_k'])], axis=2)
    parts_v = jnp.concatenate([subblock_proj(gather_pages(cache_v_cmp, layer, page_table), lp['cmp_w1_v']),
                               subblock_proj(vc_h, lp['cmp_w1_v'])], axis=2)
    kcc, vcc = compressed_kv(parts_k, parts_v, lp)
    o_c, imp = cmp_attention(qn, pos, kcc, vcc)
    idx, ok = select_blocks(imp, pos, PAST_LEN + t)
    o_s = slc_sample(qr, pos, idx, ok, cache_k_slc, cache_v_slc, layer, page_table, ks_r, vs_h)
    o_w, win_k_new, win_v_new = win_sample(qr, pos, win_k, win_v, kw_r, vw_h)
    o_b = nsa_combine(nsa_g, o_c, o_s, o_w)
    y = merge_branches(x, o_a, o_b, ga, gb, lp['w_o'])
    return y, (conv_new, rec_new, kc_h, vc_h, ks_r, vs_h, win_k_new, win_v_new)


def kernel(x_prompt, x_sample, state_gdn_conv, state_gdn_rec, cache_k_cmp, cache_v_cmp, cache_k_slc, cache_v_slc, cache_k_win, cache_v_win, page_table, ffn1_norm, ffn1_w_in, ffn1_w_out, mix_norm, w_in, gdn_conv_w, gdn_a_log, gdn_dt_bias, gdn_out_norm, nsa_q_norm, nsa_k_norm_cmp, nsa_k_norm_slc, nsa_k_norm_win, cmp_w1_k, cmp_w2_k, cmp_pe_k, cmp_w1_v, cmp_w2_v, cmp_pe_v, w_o, ffn2_norm, ffn2_w_in, ffn2_w_out):
    depth = w_in.shape[0]
    yp, ys = x_prompt, x_sample
    p_list, s_list = [], []
    for l in range(depth):
        lp = {'mix_norm': mix_norm[l], 'w_in': w_in[l], 'gdn_conv_w': gdn_conv_w[l],
              'gdn_a_log': gdn_a_log[l], 'gdn_dt_bias': gdn_dt_bias[l], 'gdn_out_norm': gdn_out_norm[l],
              'nsa_q_norm': nsa_q_norm[l], 'nsa_k_norm_cmp': nsa_k_norm_cmp[l],
              'nsa_k_norm_slc': nsa_k_norm_slc[l], 'nsa_k_norm_win': nsa_k_norm_win[l],
              'cmp_w1_k': cmp_w1_k[l], 'cmp_w2_k': cmp_w2_k[l], 'cmp_pe_k': cmp_pe_k[l],
              'cmp_w1_v': cmp_w1_v[l], 'cmp_w2_v': cmp_w2_v[l], 'cmp_pe_v': cmp_pe_v[l], 'w_o': w_o[l]}
        yp = swiglu_half(yp, ffn1_norm[l], ffn1_w_in[l], ffn1_w_out[l])
        yp, st_p = token_mix_prompt(yp, lp)
        yp = swiglu_half(yp, ffn2_norm[l], ffn2_w_in[l], ffn2_w_out[l])
        ys = swiglu_half(ys, ffn1_norm[l], ffn1_w_in[l], ffn1_w_out[l])
        ys, st_s = token_mix_sample(ys, l, state_gdn_conv[l], state_gdn_rec[l], cache_k_cmp, cache_v_cmp,
                                    cache_k_slc, cache_v_slc, cache_k_win[l], cache_v_win[l], page_table, lp)
        ys = swiglu_half(ys, ffn2_norm[l], ffn2_w_in[l], ffn2_w_out[l])
        p_list.append(st_p)
        s_list.append(st_s)
    (p_conv, p_rec, p_kc, p_vc, p_ks, p_vs, p_kw, p_vw) = [jnp.stack(a) for a in zip(*p_list)]
    (s_conv, s_rec, s_kc, s_vc, s_ks, s_vs, s_kw, s_vw) = [jnp.stack(a) for a in zip(*s_list)]
    return (yp, ys, p_conv, p_rec, p_kc, p_vc, p_ks, p_vs, p_kw, p_vw,
            s_conv, s_rec, s_kc, s_vc, s_ks, s_vs, s_kw, s_vw)
```

```python
import functools
import math

import numpy as np
import jax
import jax.numpy as jnp
from jax import lax
from jax.experimental import pallas as pl
from jax.experimental.pallas import tpu as pltpu

D_MODEL = 1024
PAST_LEN = 8192
PAGE_SIZE = 128

GDN_DK = 128
GDN_DV = 128
GDN_HEADS = D_MODEL // GDN_DV
GDN_QK = GDN_HEADS * GDN_DK
GDN_V = GDN_HEADS * GDN_DV
GDN_CONV_DIM = 2 * GDN_QK + GDN_V
GDN_CONV = 4
GDN_CHUNK = 64

NSA_HD = 128
NSA_HEADS = D_MODEL // NSA_HD
NSA_KV_HEADS = NSA_HEADS // 4
NSA_HPG = NSA_HEADS // NSA_KV_HEADS
NSA_Q = NSA_HEADS * NSA_HD
NSA_KV = NSA_KV_HEADS * NSA_HD
CMP_LEN = 32
CMP_STRIDE = 16
SLC_BLK = 64
SLC_TOPN = 16
WINDOW = 512
Q_BLOCK = 128
ATTN_SCALE = NSA_HD ** -0.5
FORCE_SCORE = 1e6

ROPE_DIM = NSA_HD // 4
ROPE_THETA = 500000.0

D_FF = ((8 * D_MODEL // 3 + 127) // 128) * 128
EPS = 1e-6

IN_SIZES = (GDN_CONV_DIM, GDN_HEADS, GDN_HEADS, GDN_V,
            NSA_Q, NSA_KV, NSA_KV, NSA_KV, NSA_KV, NSA_KV, NSA_KV, 3 * NSA_HEADS,
            D_MODEL, D_MODEL)

MXU_N = 256
VMEM_LIMIT_BYTES = 56 * 1024 * 1024

BF16 = jnp.bfloat16
F32 = jnp.float32


def _const_spec(shape):
    return pl.BlockSpec(shape, lambda *_: (0,) * len(shape), pipeline_mode=pl.Buffered(1))


def _ffn_body(x_ref, g_ref, wg_ref, wu_ref, wo_ref, o_ref):
    x = x_ref[...]
    ms = jnp.mean(x * x, axis=-1, keepdims=True)
    h = (x * lax.rsqrt(ms + EPS) * g_ref[...]).astype(BF16)
    acc = jnp.zeros(x.shape, F32)
    for c in range(D_FF // MXU_N):
        cols = slice(c * MXU_N, (c + 1) * MXU_N)
        gate = jnp.dot(h, wg_ref[:, cols], preferred_element_type=F32)
        up = jnp.dot(h, wu_ref[:, cols], preferred_element_type=F32)
        act = (gate * jax.nn.sigmoid(gate) * up).astype(BF16)
        acc = acc + jnp.dot(act, wo_ref[cols, :], preferred_element_type=F32)
    o_ref[...] = x + 0.5 * acc


def _ffn_tile(n_rows):
    return min(n_rows, 512)


def swiglu_half(x, g, w_in, w_out):
    shape = x.shape
    x2 = x.reshape(-1, D_MODEL)
    n = x2.shape[0]
    tm = _ffn_tile(n)
    wg = w_in[:, :D_FF].astype(BF16)
    wu = w_in[:, D_FF:].astype(BF16)
    wo = w_out.astype(BF16)
    out = pl.pallas_call(
        _ffn_body,
        grid=(n // tm,),
        in_specs=[pl.BlockSpec((tm, D_MODEL), lambda i: (i, 0)),
                  _const_spec((1, D_MODEL)),
                  _const_spec((D_MODEL, D_FF)),
                  _const_spec((D_MODEL, D_FF)),
                  _const_spec((D_FF, D_MODEL))],
        out_specs=pl.BlockSpec((tm, D_MODEL), lambda i: (i, 0)),
        out_shape=jax.ShapeDtypeStruct((n, D_MODEL), F32),
        compiler_params=pltpu.CompilerParams(dimension_semantics=("arbitrary",),
                                             vmem_limit_bytes=VMEM_LIMIT_BYTES),
        name="swiglu_half",
    )(x2, g.reshape(1, D_MODEL), wg, wu, wo)
    return out.reshape(shape)


NEG_INF = float("-inf")
NEG_BIG = -1e30
NT_DIMS = (((1,), (1,)), ((), ()))


def _rope_tables(pos):
    half = ROPE_DIM // 2
    inv = jnp.float32(ROPE_THETA) ** (-jnp.arange(half, dtype=jnp.float32) / half)
    ang = pos.astype(jnp.float32)[:, None] * inv
    cos, sin = jnp.cos(ang), jnp.sin(ang)
    rest = NSA_HD - ROPE_DIM
    cos_t = jnp.concatenate([cos, cos, jnp.ones((pos.shape[0], rest), F32)], axis=-1)
    sin_t = jnp.concatenate([-sin, sin, jnp.zeros((pos.shape[0], rest), F32)], axis=-1)
    return cos_t, sin_t


def _rope(y, cos_t, sin_t):
    half = ROPE_DIM // 2
    lane = lax.broadcasted_iota(jnp.int32, y.shape, 1)
    partner = jnp.where(lane < half, pltpu.roll(y, NSA_HD - half, 1), pltpu.roll(y, half, 1))
    return y * cos_t + partner * sin_t


def _head_norm(x, g):
    ms = jnp.mean(x * x, axis=-1, keepdims=True)
    return x * lax.rsqrt(ms + EPS) * g


def _nsa_prep_body(q_ref, ks_ref, kw_ref, cos_ref, sin_ref, gq_ref, gks_ref, gkw_ref,
                   qn_ref, qr_ref, ksr_ref, kwr_ref):
    cos_t, sin_t = cos_ref[...], sin_ref[...]
    for h in range(NSA_HEADS):
        cols = slice(h * NSA_HD, (h + 1) * NSA_HD)
        y = _head_norm(q_ref[0, :, cols], gq_ref[...])
        qn_ref[0, :, cols] = (y * ATTN_SCALE).astype(BF16)
        qr_ref[0, :, cols] = (_rope(y, cos_t, sin_t) * ATTN_SCALE).astype(BF16)
    for h in range(NSA_KV_HEADS):
        cols = slice(h * NSA_HD, (h + 1) * NSA_HD)
        ksr_ref[0, :, cols] = _rope(_head_norm(ks_ref[0, :, cols], gks_ref[...]), cos_t, sin_t)
        kwr_ref[0, :, cols] = _rope(_head_norm(kw_ref[0, :, cols], gkw_ref[...]), cos_t, sin_t)


def nsa_prep(q, ks, kw, pos, lp):
    b, t, _ = q.shape
    tp = min(t, 512)
    cos_t, sin_t = _rope_tables(pos)
    row = lambda w: pl.BlockSpec((1, tp, w), lambda bi, i: (bi, i, 0))
    tab = pl.BlockSpec((tp, NSA_HD), lambda bi, i: (i, 0))
    gain = pl.BlockSpec((1, NSA_HD), lambda bi, i: (0, 0))
    return pl.pallas_call(
        _nsa_prep_body,
        grid=(b, t // tp),
        in_specs=[row(NSA_Q), row(NSA_KV), row(NSA_KV), tab, tab, gain, gain, gain],
        out_specs=[row(NSA_Q), row(NSA_Q), row(NSA_KV), row(NSA_KV)],
        out_shape=[jax.ShapeDtypeStruct((b, t, NSA_Q), BF16), jax.ShapeDtypeStruct((b, t, NSA_Q), BF16),
                   jax.ShapeDtypeStruct((b, t, NSA_KV), F32), jax.ShapeDtypeStruct((b, t, NSA_KV), F32)],
        compiler_params=pltpu.CompilerParams(dimension_semantics=("arbitrary", "arbitrary"),
                                             vmem_limit_bytes=VMEM_LIMIT_BYTES),
        name="nsa_prep",
    )(q, ks, kw, cos_t, sin_t, lp['nsa_q_norm'].reshape(1, NSA_HD),
      lp['nsa_k_norm_slc'].reshape(1, NSA_HD), lp['nsa_k_norm_win'].reshape(1, NSA_HD))


def _group_queries(q, g):
    return jnp.concatenate(
        [q[:, (g * NSA_HPG + h) * NSA_HD:(g * NSA_HPG + h + 1) * NSA_HD] for h in range(NSA_HPG)], axis=0)


def _store_group(o_ref, o, g, tq):
    for h in range(NSA_HPG):
        c0 = (g * NSA_HPG + h) * NSA_HD
        o_ref[0, :, c0:c0 + NSA_HD] = o[h * tq:(h + 1) * tq]


def _cmp_select_body(q_ref, kcc_ref, vcc_ref, ovl_ref, oc_ref, sel_ref, *, tq, nc, ns, selw):
    ncp = kcc_ref.shape[2]
    q0 = pl.program_id(1) * tq
    q = q_ref[0]
    tpos = q0 + lax.broadcasted_iota(jnp.int32, (tq, ncp), 0)
    cidx = lax.broadcasted_iota(jnp.int32, (tq, ncp), 1)
    vis = (cidx * CMP_STRIDE + (CMP_LEN - 1) <= tpos) & (cidx < nc)
    jrow = lax.broadcasted_iota(jnp.int32, (ns, tq), 0)
    cur = (q0 + lax.broadcasted_iota(jnp.int32, (ns, tq), 1)) // SLC_BLK
    valid = jrow <= cur
    forced = (jrow == 0) | (jrow == cur) | (jrow == cur - 1)
    sel_parts = []
    for g in range(NSA_KV_HEADS):
        s = lax.dot_general(_group_queries(q, g), kcc_ref[0, g], NT_DIMS, preferred_element_type=F32)
        sm = jnp.where(vis[None], s.reshape(NSA_HPG, tq, ncp), NEG_INF)
        m = jnp.max(sm, axis=-1, keepdims=True)
        m = jnp.where(m > NEG_INF, m, 0.0)
        p = jnp.exp(sm - m)
        p = p / jnp.maximum(jnp.sum(p, axis=-1, keepdims=True), 1e-30)
        o = jnp.dot(p.reshape(NSA_HPG * tq, ncp).astype(BF16), vcc_ref[0, g], preferred_element_type=F32)
        _store_group(oc_ref, o, g, tq)
        imp = p[0]
        for h in range(1, NSA_HPG):
            imp = imp + p[h]
        hi = imp.astype(BF16)
        lo = (imp - hi.astype(F32)).astype(BF16)
        score = (lax.dot_general(ovl_ref[...], hi, NT_DIMS, preferred_element_type=F32)
                 + lax.dot_general(ovl_ref[...], lo, NT_DIMS, preferred_element_type=F32))
        work = jnp.where(valid, jnp.where(forced, FORCE_SCORE, score), NEG_INF)
        chosen = jnp.zeros((ns, tq), F32)
        for _ in range(min(SLC_TOPN, ns)):
            mx = jnp.max(work, axis=0, keepdims=True)
            first = jnp.min(jnp.where(work == mx, jrow, ns), axis=0, keepdims=True)
            hit = jrow == first
            chosen = jnp.where(hit & (mx > NEG_INF), 1.0, chosen)
            work = jnp.where(hit, NEG_INF, work)
        sel_parts.append(chosen)
    if selw > NSA_KV_HEADS * ns:
        sel_parts.append(jnp.zeros((selw - NSA_KV_HEADS * ns, tq), F32))
    sel_ref[0] = jnp.concatenate(sel_parts, axis=0).T.astype(BF16)


def _sel_width(ns):
    return -(-NSA_KV_HEADS * ns // 128) * 128


def cmp_select_prompt(qn, kcc, vcc):
    b, s, _ = qn.shape
    nc = kcc.shape[1]
    ns = -(-s // SLC_BLK)
    ncp = -(-nc // 128) * 128
    selw = _sel_width(ns)
    tq = min(s, 256)
    pad = lambda a: jnp.pad(jnp.moveaxis(a, 2, 1), ((0, 0), (0, 0), (0, ncp - nc), (0, 0))).astype(BF16)
    cstart = np.arange(ncp) * CMP_STRIDE
    sstart = np.arange(ns) * SLC_BLK
    ovl = ((cstart[None, :] < sstart[:, None] + SLC_BLK) & (cstart[None, :] + CMP_LEN > sstart[:, None])
           & (np.arange(ncp)[None, :] < nc))
    ovl_t = jnp.asarray(ovl, dtype=BF16)
    kv_spec = pl.BlockSpec((1, NSA_KV_HEADS, ncp, NSA_HD), lambda bi, i: (bi, 0, 0, 0))
    return pl.pallas_call(
        functools.partial(_cmp_select_body, tq=tq, nc=nc, ns=ns, selw=selw),
        grid=(b, s // tq),
        in_specs=[pl.BlockSpec((1, tq, NSA_Q), lambda bi, i: (bi, i, 0)), kv_spec, kv_spec,
                  pl.BlockSpec((ns, ncp), lambda bi, i: (0, 0))],
        out_specs=[pl.BlockSpec((1, tq, NSA_Q), lambda bi, i: (bi, i, 0)),
                   pl.BlockSpec((1, tq, selw), lambda bi, i: (bi, i, 0))],
        out_shape=[jax.ShapeDtypeStruct((b, s, NSA_Q), F32), jax.ShapeDtypeStruct((b, s, selw), BF16)],
        compiler_params=pltpu.CompilerParams(dimension_semantics=("arbitrary", "arbitrary"),
                                             vmem_limit_bytes=VMEM_LIMIT_BYTES),
        name="cmp_select_prompt",
    )(qn, pad(kcc), pad(vcc), ovl_t)


def _slc_body(q_ref, sel_ref, k_ref, v_ref, o_ref, *, tq, tk, ns):
    selw = sel_ref.shape[2]
    q0 = pl.program_id(1) * tq
    n_kv = (q0 + tq + tk - 1) // tk
    q = q_ref[0]
    selm = sel_ref[0]
    qpos = q0 + lax.broadcasted_iota(jnp.int32, (tq, tk), 0)
    kio = lax.broadcasted_iota(jnp.int32, (tq, tk), 1)
    erow = lax.broadcasted_iota(jnp.int32, (selw, tk), 0)
    ecol = lax.broadcasted_iota(jnp.int32, (selw, tk), 1)
    for g in range(NSA_KV_HEADS):
        q4 = _group_queries(q, g)
        gcols = slice(g * NSA_HD, (g + 1) * NSA_HD)

        def body(j, carry, q4=q4, gcols=gcols, g=g):
            m, l, acc = carry
            k0 = pl.multiple_of(j * tk, tk)
            kt = k_ref[0, pl.ds(k0, tk), gcols]
            vt = v_ref[0, pl.ds(k0, tk), gcols]
            s = lax.dot_general(q4, kt, NT_DIMS, preferred_element_type=F32).reshape(NSA_HPG, tq, tk)
            expand = jnp.where(erow - g * ns == (ecol + k0) // SLC_BLK, 1.0, 0.0).astype(BF16)
            chosen = jnp.dot(selm, expand, preferred_element_type=F32)
            allowed = ((chosen > 0.5) & (kio + k0 <= qpos))[None]
            m_new = jnp.maximum(m, jnp.max(jnp.where(allowed, s, NEG_BIG), axis=-1, keepdims=True))
            p = jnp.where(allowed, jnp.exp(s - m_new), 0.0)
            alpha = jnp.exp(m - m_new)
            l = alpha * l + jnp.sum(p, axis=-1, keepdims=True)
            pv = jnp.dot(p.reshape(NSA_HPG * tq, tk).astype(BF16), vt, preferred_element_type=F32)
            acc = alpha.reshape(NSA_HPG * tq, 1) * acc + pv
            return m_new, l, acc

        init = (jnp.full((NSA_HPG, tq, 1), NEG_BIG, F32), jnp.zeros((NSA_HPG, tq, 1), F32),
                jnp.zeros((NSA_HPG * tq, NSA_HD), F32))
        _, l, acc = lax.fori_loop(0, n_kv, body, init)
        _store_group(o_ref, acc / jnp.maximum(l.reshape(NSA_HPG * tq, 1), 1e-30), g, tq)


def slc_attention_prompt(qr, sel, k, v):
    b, s, _ = qr.shape
    selw = sel.shape[2]
    ns = -(-s // SLC_BLK)
    tq = min(s, 256)
    tk = min(s, 512)
    kv_spec = pl.BlockSpec((1, s, NSA_KV), lambda bi, i: (bi, 0, 0))
    return pl.pallas_call(
        functools.partial(_slc_body, tq=tq, tk=tk, ns=ns),
        grid=(b, s // tq),
        in_specs=[pl.BlockSpec((1, tq, NSA_Q), lambda bi, i: (bi, i, 0)),
                  pl.BlockSpec((1, tq, selw), lambda bi, i: (bi, i, 0)), kv_spec, kv_spec],
        out_specs=pl.BlockSpec((1, tq, NSA_Q), lambda bi, i: (bi, i, 0)),
        out_shape=jax.ShapeDtypeStruct((b, s, NSA_Q), F32),
        compiler_params=pltpu.CompilerParams(dimension_semantics=("arbitrary", "arbitrary"),
                                             vmem_limit_bytes=VMEM_LIMIT_BYTES),
        name="slc_attention_prompt",
    )(qr, sel, k, v)


def _win_body(q_ref, k_ref, v_ref, o_ref, *, tq, span):
    q0 = pl.program_id(1) * tq
    kstart = pl.multiple_of(jnp.maximum(q0 - WINDOW, 0), tq)
    q = q_ref[0]
    qpos = q0 + lax.broadcasted_iota(jnp.int32, (tq, span), 0)
    kpos = kstart + lax.broadcasted_iota(jnp.int32, (tq, span), 1)
    allowed = ((kpos <= qpos) & (kpos > qpos - WINDOW))[None]
    for g in range(NSA_KV_HEADS):
        gcols = slice(g * NSA_HD, (g + 1) * NSA_HD)
        kt = k_ref[0, pl.ds(kstart, span), gcols]
        vt = v_ref[0, pl.ds(kstart, span), gcols]
        s = lax.dot_general(_group_queries(q, g), kt, NT_DIMS, preferred_element_type=F32)
        sm = jnp.where(allowed, s.reshape(NSA_HPG, tq, span), NEG_INF)
        p = jnp.exp(sm - jnp.max(sm, axis=-1, keepdims=True))
        l = jnp.sum(p, axis=-1, keepdims=True)
        o = jnp.dot(p.reshape(NSA_HPG * tq, span).astype(BF16), vt, preferred_element_type=F32)
        _store_group(o_ref, o / jnp.maximum(l.reshape(NSA_HPG * tq, 1), 1e-30), g, tq)


def win_attention_prompt(qr, k, v):
    b, s, _ = qr.shape
    tq = min(s, 256)
    span = min(s, WINDOW + tq)
    assert WINDOW % tq == 0 or s == tq
    kv_spec = pl.BlockSpec((1, s, NSA_KV), lambda bi, i: (bi, 0, 0))
    return pl.pallas_call(
        functools.partial(_win_body, tq=tq, span=span),
        grid=(b, s // tq),
        in_specs=[pl.BlockSpec((1, tq, NSA_Q), lambda bi, i: (bi, i, 0)), kv_spec, kv_spec],
        out_specs=pl.BlockSpec((1, tq, NSA_Q), lambda bi, i: (bi, i, 0)),
        out_shape=jax.ShapeDtypeStruct((b, s, NSA_Q), F32),
        compiler_params=pltpu.CompilerParams(dimension_semantics=("arbitrary", "arbitrary"),
                                             vmem_limit_bytes=VMEM_LIMIT_BYTES),
        name="win_attention_prompt",
    )(qr, k, v)


def nsa_combine_flat(gates, o_c, o_s, o_w):
    gt = jax.nn.sigmoid(gates.astype(jnp.float32))
    out = 0.0
    for br, o in enumerate((o_c, o_s, o_w)):
        out = out + jnp.repeat(gt[..., br * NSA_HEADS:(br + 1) * NSA_HEADS], NSA_HD, axis=-1) * o
    return out


CMP_PAGES_PER_STEP = 16
SUB_PER_PAGE = PAGE_SIZE // CMP_STRIDE
PAGE_ROWS = PAGE_SIZE * NSA_KV_HEADS
CMP_R = CMP_LEN // CMP_STRIDE
Q_ROWS = 8


def _pad_rows(x, rows):
    return jnp.concatenate([x, jnp.zeros((rows - x.shape[0], x.shape[1]), x.dtype)], axis=0)


def _token_group_queries(q, g):
    rows = [q[:, (g * NSA_HPG + h) * NSA_HD:(g * NSA_HPG + h + 1) * NSA_HD] for h in range(NSA_HPG)]
    return _pad_rows(jnp.concatenate(rows, axis=0), Q_ROWS)


def _subblock_weight(w1):
    w = w1.reshape(CMP_R, CMP_STRIDE, NSA_HD, NSA_HD)
    return w.transpose(1, 2, 0, 3).reshape(CMP_STRIDE * NSA_HD, CMP_R * NSA_HD).astype(BF16)


def _pe_rows(pe):
    return _pad_rows(pe.reshape(CMP_R, CMP_STRIDE * NSA_HD), Q_ROWS).astype(BF16)


def _compress_from_parts(parts, pe_rows, w_sub, w2):
    n = parts.shape[0]
    hid = parts[:, :NSA_HD]
    for r in range(1, CMP_R):
        hid = hid + pltpu.roll(parts[:, r * NSA_HD:(r + 1) * NSA_HD], n - r, 0)
    pe_proj = jnp.dot(pe_rows, w_sub, preferred_element_type=F32)
    for r in range(CMP_R):
        hid = hid + pe_proj[r:r + 1, r * NSA_HD:(r + 1) * NSA_HD]
    act = hid * jax.nn.sigmoid(hid)
    return jnp.dot(act.astype(BF16), w2, preferred_element_type=F32)


def _cmp_sample_body(pt_ref, *refs, q_pos, nc):
    npg = CMP_PAGES_PER_STEP
    k_pages, v_pages = refs[:npg], refs[npg:2 * npg]
    (wk_ref, wv_ref, pek_ref, pev_ref, w2k_ref, w2v_ref, gk_ref, q_ref,
     oc_ref, imp_ref, pk_ref, pv_ref) = refs[2 * npg:]
    c = pl.program_id(1)
    rows = npg * SUB_PER_PAGE

    def project(pages, w_ref, parts_ref):
        xg = jnp.concatenate(
            [jnp.concatenate(
                [jnp.concatenate([p[0, pl.ds(s * NSA_KV_HEADS + g, SUB_PER_PAGE, stride=CMP_STRIDE * NSA_KV_HEADS), :]
                                  for s in range(CMP_STRIDE)], axis=1) for p in pages], axis=0)
             for g in range(NSA_KV_HEADS)], axis=0).astype(BF16)
        parts = jnp.dot(xg, w_ref[...], preferred_element_type=F32)
        for g in range(NSA_KV_HEADS):
            parts_ref[g, pl.ds(pl.multiple_of(c * rows, rows), rows), :] = parts[g * rows:(g + 1) * rows]

    project(k_pages, wk_ref, pk_ref)
    project(v_pages, wv_ref, pv_ref)

    @pl.when(c == pl.num_programs(1) - 1)
    def _():
        ncp = pk_ref.shape[1]
        q = q_ref[0]
        cidx = lax.broadcasted_iota(jnp.int32, (Q_ROWS, ncp), 1)
        vis = (cidx < nc) & (cidx * CMP_STRIDE + (CMP_LEN - 1) <= q_pos)
        for g in range(NSA_KV_HEADS):
            kcc = _compress_from_parts(pk_ref[g], pek_ref[...], wk_ref[...], w2k_ref[...])
            kcc = _head_norm(kcc, gk_ref[...])
            vcc = _compress_from_parts(pv_ref[g], pev_ref[...], wv_ref[...], w2v_ref[...])
            s = lax.dot_general(_token_group_queries(q, g), kcc.astype(BF16), NT_DIMS, preferred_element_type=F32)
            sm = jnp.where(vis, s, NEG_INF)
            m = jnp.max(sm, axis=-1, keepdims=True)
            m = jnp.where(m > NEG_INF, m, 0.0)
            p = jnp.exp(sm - m)
            p = p / jnp.maximum(jnp.sum(p, axis=-1, keepdims=True), 1e-30)
            o = jnp.dot(p.astype(BF16), vcc.astype(BF16), preferred_element_type=F32)
            for h in range(NSA_HPG):
                c0 = (g * NSA_HPG + h) * NSA_HD
                oc_ref[0, :, c0:c0 + NSA_HD] = o[h:h + 1]
            imp_ref[0, g:g + 1, :] = jnp.sum(p[:NSA_HPG], axis=0, keepdims=True)


def cmp_attention_sample(qn, pool_k, pool_v, page_table, lp, q_pos):
    db, n_pages = page_table.shape
    n_pool = pool_k.shape[0]
    npg = CMP_PAGES_PER_STEP
    assert n_pages % npg == 0
    n_sub = n_pages * SUB_PER_PAGE
    nc = n_sub - CMP_R + 1
    view = lambda p: p.reshape(n_pool, PAGE_ROWS, NSA_HD)
    page_spec = lambda i: pl.BlockSpec((1, PAGE_ROWS, NSA_HD), lambda b, c, pt: (pt[b, c * npg + i], 0, 0))
    const = lambda shape: pl.BlockSpec(shape, lambda b, c, pt: (0,) * len(shape))
    kdim = CMP_STRIDE * NSA_HD
    grid_spec = pltpu.PrefetchScalarGridSpec(
        num_scalar_prefetch=1,
        grid=(db, n_pages // npg),
        in_specs=([page_spec(i) for i in range(npg)] + [page_spec(i) for i in range(npg)]
                  + [const((kdim, CMP_R * NSA_HD)), const((kdim, CMP_R * NSA_HD)),
                     const((Q_ROWS, kdim)), const((Q_ROWS, kdim)),
                     const((NSA_HD, NSA_HD)), const((NSA_HD, NSA_HD)), const((1, NSA_HD)),
                     pl.BlockSpec((1, 1, NSA_Q), lambda b, c, pt: (b, 0, 0))]),
        out_specs=[pl.BlockSpec((1, 1, NSA_Q), lambda b, c, pt: (b, 0, 0)),
                   pl.BlockSpec((1, NSA_KV_HEADS, n_sub), lambda b, c, pt: (b, 0, 0))],
        scratch_shapes=[pltpu.VMEM((NSA_KV_HEADS, n_sub, CMP_R * NSA_HD), F32),
                        pltpu.VMEM((NSA_KV_HEADS, n_sub, CMP_R * NSA_HD), F32)],
    )
    return pl.pallas_call(
        functools.partial(_cmp_sample_body, q_pos=q_pos, nc=nc),
        grid_spec=grid_spec,
        out_shape=[jax.ShapeDtypeStruct((db, 1, NSA_Q), F32), jax.ShapeDtypeStruct((db, NSA_KV_HEADS, n_sub), F32)],
        compiler_params=pltpu.CompilerParams(dimension_semantics=("arbitrary", "arbitrary"),
                                             vmem_limit_bytes=VMEM_LIMIT_BYTES),
        name="cmp_attention_sample",
    )(page_table, *([view(pool_k)] * npg), *([view(pool_v)] * npg),
      _subblock_weight(lp['cmp_w1_k']), _subblock_weight(lp['cmp_w1_v']),
      _pe_rows(lp['cmp_pe_k']), _pe_rows(lp['cmp_pe_v']),
      lp['cmp_w2_k'].astype(BF16), lp['cmp_w2_v'].astype(BF16),
      lp['nsa_k_norm_cmp'].reshape(1, NSA_HD), qn), nc


IDX_LANES = 128


def _select_sample_body(imp_ref, ovl_ref, idx_ref, *, q_pos, ns):
    imp = imp_ref[...]
    hi = imp.astype(BF16)
    lo = (imp - hi.astype(F32)).astype(BF16)
    score = (jnp.dot(hi, ovl_ref[...], preferred_element_type=F32)
             + jnp.dot(lo, ovl_ref[...], preferred_element_type=F32))
    rows, nsp = score.shape
    j = lax.broadcasted_iota(jnp.int32, (rows, nsp), 1)
    cur = q_pos // SLC_BLK
    valid = (j <= cur) & (j < ns)
    forced = (j == 0) | (j == cur) | (j == cur - 1)
    work = jnp.where(valid, jnp.where(forced, FORCE_SCORE, score), NEG_INF)
    slot = lax.broadcasted_iota(jnp.int32, (rows, IDX_LANES), 1)
    idx = jnp.full((rows, IDX_LANES), -1, jnp.int32)
    for it in range(min(SLC_TOPN, ns)):
        mx = jnp.max(work, axis=-1, keepdims=True)
        first = jnp.min(jnp.where(work == mx, j, nsp), axis=-1, keepdims=True)
        idx = jnp.where((slot == it) & (mx > NEG_INF), first, idx)
        work = jnp.where(j == first, NEG_INF, work)
    idx_ref[...] = idx


def select_blocks_sample(imp, nc, q_pos, n_keys):
    db, kv, ncp = imp.shape
    ns = -(-n_keys // SLC_BLK)
    nsp = -(-ns // 128) * 128
    cstart = np.arange(ncp) * CMP_STRIDE
    sstart = np.arange(nsp) * SLC_BLK
    ovl = ((cstart[:, None] < sstart[None, :] + SLC_BLK) & (cstart[:, None] + CMP_LEN > sstart[None, :])
           & (np.arange(ncp)[:, None] < nc) & (np.arange(nsp)[None, :] < ns))
    rows = db * kv
    idx = pl.pallas_call(
        functools.partial(_select_sample_body, q_pos=q_pos, ns=ns),
        grid=(1,),
        in_specs=[pl.BlockSpec((rows, ncp), lambda i: (0, 0)), pl.BlockSpec((ncp, nsp), lambda i: (0, 0))],
        out_specs=pl.BlockSpec((rows, IDX_LANES), lambda i: (0, 0)),
        out_shape=jax.ShapeDtypeStruct((rows, IDX_LANES), jnp.int32),
        compiler_params=pltpu.CompilerParams(vmem_limit_bytes=VMEM_LIMIT_BYTES),
        name="select_blocks_sample",
    )(imp.reshape(rows, ncp), jnp.asarray(ovl, dtype=BF16))
    return idx.reshape(db, kv, IDX_LANES)


def _token_softmax_attend(q8, s_past, allowed, v_past, k_new, v_new, new_ok):
    s_new = jnp.sum(q8.astype(F32) * k_new, axis=-1, keepdims=True)
    sm = jnp.where(allowed, s_past, NEG_INF)
    m = jnp.maximum(jnp.max(sm, axis=-1, keepdims=True), jnp.where(new_ok, s_new, NEG_INF))
    m = jnp.where(m > NEG_INF, m, 0.0)
    p = jnp.exp(sm - m)
    p_new = jnp.where(new_ok, jnp.exp(s_new - m), 0.0)
    l = jnp.sum(p, axis=-1, keepdims=True) + p_new
    o = jnp.dot(p.astype(BF16), v_past, preferred_element_type=F32) + p_new * v_new
    return o / jnp.maximum(l, 1e-30)


def _slc_sample_body(idx_ref, pt_ref, *refs, n_past):
    n = SLC_TOPN
    k_blocks, v_blocks = refs[:n], refs[n:2 * n]
    q_ref, kn_ref, vn_ref, o_ref = refs[2 * n:]
    b, g = pl.program_id(0), pl.program_id(1)
    q8 = _token_group_queries_half(q_ref[0])

    def group_rows(blocks):
        sel_g = jnp.full((SLC_TOPN * SLC_BLK, NSA_HD), g, jnp.int32)
        per_group = [jnp.concatenate([r[0, pl.ds(gg, SLC_BLK, stride=NSA_KV_HEADS), :] for r in blocks], axis=0)
                     for gg in range(NSA_KV_HEADS)]
        out = per_group[0]
        for gg in range(1, NSA_KV_HEADS):
            out = jnp.where(sel_g == gg, per_group[gg], out)
        return out.astype(BF16)

    k = group_rows(k_blocks)
    v = group_rows(v_blocks)
    s = lax.dot_general(q8, k, NT_DIMS, preferred_element_type=F32)
    slot = lax.broadcasted_iota(jnp.int32, s.shape, 1) // SLC_BLK
    slot_ok = jnp.zeros(s.shape, jnp.int32)
    n_new = jnp.int32(0)
    for i in range(n):
        ji = idx_ref[(b * NSA_KV_HEADS + g) * n + i]
        slot_ok = jnp.where(slot == i, jnp.where((ji >= 0) & (ji < n_past), 1, 0), slot_ok)
        n_new = n_new + jnp.where(ji >= n_past, 1, 0)
    new_ok = jnp.full((Q_ROWS, 1), n_new, jnp.int32) > 0
    o = _token_softmax_attend(q8, s, slot_ok > 0, v, kn_ref[0], vn_ref[0], new_ok)
    for h in range(NSA_HPG):
        o_ref[0, :, h * NSA_HD:(h + 1) * NSA_HD] = o[h:h + 1]


def _token_group_queries_half(q):
    rows = [q[:, h * NSA_HD:(h + 1) * NSA_HD] for h in range(NSA_HPG)]
    return _pad_rows(jnp.concatenate(rows, axis=0), Q_ROWS)


def slc_attention_sample(qr, idx, pool_k, pool_v, page_table, k_new, v_new):
    db, n_pages = page_table.shape
    n_pool = pool_k.shape[0]
    bpp = PAGE_SIZE // SLC_BLK
    n_past = n_pages * bpp
    n = SLC_TOPN
    gw = NSA_HPG * NSA_HD
    idx_flat = idx[:, :, :n].reshape(-1)
    pt_flat = page_table.reshape(-1)
    view = lambda p: p.reshape(n_pool * bpp, SLC_BLK * NSA_KV_HEADS, NSA_HD)

    def blk_spec(i):
        def index_map(b, g, idx_s, pt_s):
            jc = jnp.clip(idx_s[(b * NSA_KV_HEADS + g) * n + i], 0, n_past - 1)
            return (pt_s[b * n_pages + jc // bpp] * bpp + jc % bpp, 0, 0)
        return pl.BlockSpec((1, SLC_BLK * NSA_KV_HEADS, NSA_HD), index_map)

    grid_spec = pltpu.PrefetchScalarGridSpec(
        num_scalar_prefetch=2,
        grid=(db, NSA_KV_HEADS),
        in_specs=([blk_spec(i) for i in range(n)] + [blk_spec(i) for i in range(n)]
                  + [pl.BlockSpec((1, 1, gw), lambda b, g, *_: (b, 0, g)),
                     pl.BlockSpec((1, 1, NSA_HD), lambda b, g, *_: (b, 0, g)),
                     pl.BlockSpec((1, 1, NSA_HD), lambda b, g, *_: (b, 0, g))]),
        out_specs=pl.BlockSpec((1, 1, gw), lambda b, g, *_: (b, 0, g)),
    )
    return pl.pallas_call(
        functools.partial(_slc_sample_body, n_past=n_past),
        grid_spec=grid_spec,
        out_shape=jax.ShapeDtypeStruct((db, 1, NSA_Q), F32),
        compiler_params=pltpu.CompilerParams(dimension_semantics=("arbitrary", "arbitrary"),
                                             vmem_limit_bytes=VMEM_LIMIT_BYTES),
        name="slc_attention_sample",
    )(idx_flat, pt_flat, *([view(pool_k)] * n), *([view(pool_v)] * n), qr, k_new, v_new)


def _win_sample_body(q_ref, bk_ref, bv_ref, kn_ref, vn_ref, o_ref, nk_ref, nv_ref, *, q_pos, buf_start):
    rows = bk_ref.shape[1]
    wb = rows // NSA_KV_HEADS
    kn, vn = kn_ref[0], vn_ref[0]
    q = q_ref[0]
    kpos = buf_start + lax.broadcasted_iota(jnp.int32, (Q_ROWS, wb), 1)
    allowed = (kpos <= q_pos) & (kpos > q_pos - WINDOW)
    new_ok = jnp.ones((Q_ROWS, 1), jnp.bool_)
    for g in range(NSA_KV_HEADS):
        gcols = slice(g * NSA_HD, (g + 1) * NSA_HD)
        q8 = _token_group_queries(q, g)
        kg = bk_ref[0, pl.ds(g, wb, stride=NSA_KV_HEADS), :].astype(BF16)
        vg = bv_ref[0, pl.ds(g, wb, stride=NSA_KV_HEADS), :].astype(BF16)
        s = lax.dot_general(q8, kg, NT_DIMS, preferred_element_type=F32)
        o = _token_softmax_attend(q8, s, allowed, vg, kn[:, gcols], vn[:, gcols], new_ok)
        for h in range(NSA_HPG):
            c0 = (g * NSA_HPG + h) * NSA_HD
            o_ref[0, :, c0:c0 + NSA_HD] = o[h:h + 1]
    row = lax.broadcasted_iota(jnp.int32, (rows, NSA_HD), 0)

    def shifted(buf_ref, new):
        out = pltpu.roll(buf_ref[0], rows - NSA_KV_HEADS, 0)
        for g in range(NSA_KV_HEADS):
            out = jnp.where(row == rows - NSA_KV_HEADS + g, new[:, g * NSA_HD:(g + 1) * NSA_HD], out)
        return out

    nk_ref[0] = shifted(bk_ref, kn)
    nv_ref[0] = shifted(bv_ref, vn)


def win_attention_sample(qr, buf_k, buf_v, k_new, v_new, q_pos):
    db, rows, _ = buf_k.shape
    wb = rows // NSA_KV_HEADS
    tok = lambda w: pl.BlockSpec((1, 1, w), lambda b: (b, 0, 0))
    buf = pl.BlockSpec((1, rows, NSA_HD), lambda b: (b, 0, 0))
    return pl.pallas_call(
        functools.partial(_win_sample_body, q_pos=q_pos, buf_start=q_pos - wb),
        grid=(db,),
        in_specs=[tok(NSA_Q), buf, buf, tok(NSA_KV), tok(NSA_KV)],
        out_specs=[tok(NSA_Q), buf, buf],
        out_shape=[jax.ShapeDtypeStruct((db, 1, NSA_Q), F32), jax.ShapeDtypeStruct((db, rows, NSA_HD), F32),
                   jax.ShapeDtypeStruct((db, rows, NSA_HD), F32)],
        compiler_params=pltpu.CompilerParams(dimension_semantics=("arbitrary",),
                                             vmem_limit_bytes=VMEM_LIMIT_BYTES),
        name="win_attention_sample",
    )(qr, buf_k, buf_v, k_new, v_new)


def rms_norm(x, g):
    xf = x.astype(jnp.float32)
    y = xf * lax.rsqrt(jnp.mean(xf * xf, axis=-1, keepdims=True) + EPS)
    return (y * g.astype(jnp.float32)).astype(x.dtype)


def l2_norm(x):
    return x * lax.rsqrt(jnp.sum(x * x, axis=-1, keepdims=True) + EPS)


def rope_partial(x, pos):
    half = ROPE_DIM // 2
    inv = jnp.float32(ROPE_THETA) ** (-jnp.arange(half, dtype=jnp.float32) / half)
    ang = pos.astype(jnp.float32)[:, None] * inv
    shape = (pos.shape[0],) + (1,) * (x.ndim - 3) + (half,)
    cos = jnp.cos(ang).reshape(shape)
    sin = jnp.sin(ang).reshape(shape)
    xf = x.astype(jnp.float32)
    x1, x2, rest = xf[..., :half], xf[..., half:ROPE_DIM], xf[..., ROPE_DIM:]
    out = jnp.concatenate([x1 * cos - x2 * sin, x2 * cos + x1 * sin, rest], axis=-1)
    return out.astype(x.dtype)


def masked_softmax(s, mask, axis=-1):
    s = jnp.where(mask, s.astype(jnp.float32), -jnp.inf)
    m = jnp.max(s, axis=axis, keepdims=True)
    m = jnp.where(jnp.isfinite(m), m, 0.0)
    p = jnp.exp(s - m)
    return p / jnp.maximum(jnp.sum(p, axis=axis, keepdims=True), 1e-30)


def split_cols(p, sizes):
    offs = np.cumsum((0,) + tuple(sizes))
    return [p[..., int(offs[i]):int(offs[i + 1])] for i in range(len(sizes))]


def mixer_inputs(x, lp):
    h = rms_norm(x, lp['mix_norm'])
    return split_cols(h @ lp['w_in'], IN_SIZES)


def causal_conv(x, buf, w):
    t = x.shape[1]
    xp = jnp.concatenate([buf.astype(x.dtype), x], axis=1)
    out = xp[:, 0:t] * w[0]
    for j in range(1, GDN_CONV):
        out = out + xp[:, j:j + t] * w[j]
    return jax.nn.silu(out), xp[:, t:]


def gated_delta_rule(q, k, v, g, beta, s0):
    b, t, h, dk = q.shape
    dv = v.shape[-1]
    c = min(GDN_CHUNK, t)
    pad = (-t) % c

    def prep(a):
        a = jnp.pad(a, [(0, 0), (0, pad)] + [(0, 0)] * (a.ndim - 2))
        a = jnp.moveaxis(a, 2, 1)
        return a.reshape((b, h, -1, c) + a.shape[3:])

    q, k, v, g, beta = [prep(a) for a in (q * dk ** -0.5, k, v, g, beta)]
    gc = jnp.cumsum(g, axis=-1)
    ii = jnp.arange(c)
    incl = ii[:, None] >= ii[None, :]
    strict = ii[:, None] > ii[None, :]
    diff = gc[..., :, None] - gc[..., None, :]
    decay = jnp.where(incl, jnp.exp(jnp.where(incl, diff, 0.0)), 0.0)
    kb = k * beta[..., None]
    m = jnp.where(strict, jnp.einsum('bhncd,bhnsd->bhncs', kb, k) * decay, 0.0)
    a_mat = m + jnp.eye(c, dtype=m.dtype)
    rhs = jnp.concatenate([v * beta[..., None], kb * jnp.exp(gc)[..., None]], axis=-1)
    sol = lax.linalg.triangular_solve(a_mat, rhs, left_side=True, lower=True, unit_diagonal=True)
    u, w = sol[..., :dv], sol[..., dv:]
    attn = jnp.where(incl, jnp.einsum('bhncd,bhnsd->bhncs', q, k) * decay, 0.0)

    def step(s, inp):
        q_c, k_c, u_c, w_c, gc_c, attn_c = inp
        v_new = u_c - jnp.einsum('bhcd,bhde->bhce', w_c, s)
        o = (jnp.einsum('bhcd,bhde->bhce', q_c * jnp.exp(gc_c)[..., None], s)
             + jnp.einsum('bhcs,bhse->bhce', attn_c, v_new))
        g_last = gc_c[..., -1]
        k_dec = k_c * jnp.exp(g_last[..., None] - gc_c)[..., None]
        s = s * jnp.exp(g_last)[..., None, None] + jnp.einsum('bhcd,bhce->bhde', k_dec, v_new)
        return s, o

    xs = tuple(jnp.moveaxis(a, 2, 0) for a in (q, k, u, w, gc, attn))
    s_fin, o = lax.scan(step, s0, xs)
    o = jnp.moveaxis(o, 0, 2).reshape(b, h, -1, dv)[:, :, :t]
    return jnp.moveaxis(o, 1, 2), s_fin


def gdn_mixer(qkv, a, b, z, conv_buf, rec, lp):
    bsz, t, _ = qkv.shape
    qkv_c, conv_new = causal_conv(qkv, conv_buf, lp['gdn_conv_w'])
    qkv_c = qkv_c.astype(jnp.float32)
    q = l2_norm(qkv_c[..., :GDN_QK].reshape(bsz, t, GDN_HEADS, GDN_DK))
    k = l2_norm(qkv_c[..., GDN_QK:2 * GDN_QK].reshape(bsz, t, GDN_HEADS, GDN_DK))
    v = qkv_c[..., 2 * GDN_QK:].reshape(bsz, t, GDN_HEADS, GDN_DV)
    g = -jnp.exp(lp['gdn_a_log'].astype(jnp.float32)) * jax.nn.softplus(
        a.astype(jnp.float32) + lp['gdn_dt_bias'].astype(jnp.float32))
    beta = jax.nn.sigmoid(b.astype(jnp.float32))
    o, rec_new = gated_delta_rule(q, k, v, g, beta, rec.astype(jnp.float32))
    zh = z.astype(jnp.float32).reshape(bsz, t, GDN_HEADS, GDN_DV)
    o = rms_norm(o, lp['gdn_out_norm']) * jax.nn.silu(zh)
    return o.reshape(bsz, t, GDN_V).astype(qkv.dtype), conv_new, rec_new.astype(rec.dtype)


def nsa_project(q, kc, vc, ks, vs, kw, vw, pos, lp):
    b, t, _ = q.shape

    def heads(a, n):
        return a.reshape(b, t, n, NSA_HD)

    qn = rms_norm(heads(q, NSA_HEADS), lp['nsa_q_norm']).reshape(b, t, NSA_KV_HEADS, NSA_HPG, NSA_HD)
    qr = rope_partial(qn, pos)
    ks_r = rope_partial(rms_norm(heads(ks, NSA_KV_HEADS), lp['nsa_k_norm_slc']), pos)
    kw_r = rope_partial(rms_norm(heads(kw, NSA_KV_HEADS), lp['nsa_k_norm_win']), pos)
    return (qn, qr, heads(kc, NSA_KV_HEADS), heads(vc, NSA_KV_HEADS), ks_r,
            heads(vs, NSA_KV_HEADS), kw_r, heads(vw, NSA_KV_HEADS))


def subblock_proj(rows, w1):
    b, l, g, hd = rows.shape
    nh = l // CMP_STRIDE
    sub = rows[:, :nh * CMP_STRIDE].reshape(b, nh, CMP_STRIDE, g, hd)
    w = w1.reshape(CMP_LEN // CMP_STRIDE, CMP_STRIDE, hd, hd)
    return jnp.einsum('bnsgd,rsde->rbnge', sub, w)


def compress(parts, w1, w2, pe):
    r_sub = parts.shape[0]
    nc = parts.shape[2] - r_sub + 1
    hid = parts[0, :, 0:nc]
    for r in range(1, r_sub):
        hid = hid + parts[r, :, r:r + nc]
    hid = hid + jnp.einsum('ld,lde->e', pe, w1)
    return jax.nn.silu(hid) @ w2


def compressed_kv(parts_k, parts_v, lp):
    kcc = rms_norm(compress(parts_k, lp['cmp_w1_k'], lp['cmp_w2_k'], lp['cmp_pe_k']), lp['nsa_k_norm_cmp'])
    vcc = compress(parts_v, lp['cmp_w1_v'], lp['cmp_w2_v'], lp['cmp_pe_v'])
    return kcc, vcc


def cmp_attention(q, q_pos, kc, vc):
    nc = kc.shape[1]
    end = jnp.arange(nc) * CMP_STRIDE + CMP_LEN - 1
    vis = end[None, :] <= q_pos[:, None]
    s = jnp.einsum('btghd,bcgd->btghc', q, kc) * ATTN_SCALE
    p = masked_softmax(s, vis[None, :, None, None, :], axis=-1)
    o = jnp.einsum('btghc,bcgd->btghd', p, vc)
    return o, jnp.sum(p, axis=3)


def select_blocks(imp, q_pos, n_keys):
    nc = imp.shape[-1]
    ns = -(-n_keys // SLC_BLK)
    cstart = jnp.arange(nc) * CMP_STRIDE
    sstart = jnp.arange(ns) * SLC_BLK
    overlap = (cstart[:, None] < sstart[None, :] + SLC_BLK) & (cstart[:, None] + CMP_LEN > sstart[None, :])
    score = imp @ overlap.astype(imp.dtype)
    cur = q_pos // SLC_BLK
    j = jnp.arange(ns)[None, :]
    valid = j <= cur[:, None]
    forced = (j == 0) | (j == cur[:, None]) | (j == cur[:, None] - 1)
    score = jnp.where(valid[None, :, None, :],
                      jnp.where(forced[None, :, None, :], FORCE_SCORE, score), -jnp.inf)
    vals, idx = lax.top_k(score, min(SLC_TOPN, ns))
    return idx, jnp.isfinite(vals)


def slc_attention(q, q_pos, idx, ok, kg, vg):
    s = jnp.einsum('btghd,btgnsd->btghns', q, kg) * ATTN_SCALE
    kpos = idx[..., None] * SLC_BLK + jnp.arange(SLC_BLK)
    mask = ok[..., None] & (kpos <= q_pos[None, :, None, None, None])
    p = masked_softmax(s, mask[:, :, :, None], axis=(-2, -1))
    return jnp.einsum('btghns,btgnsd->btghd', p, vg)


def slc_prompt(q, q_pos, idx, ok, ks, vs):
    b, s, g, hd = ks.shape
    ns = s // SLC_BLK
    kb = ks.reshape(b, ns, SLC_BLK, g, hd).transpose(0, 3, 1, 2, 4)
    vb = vs.reshape(b, ns, SLC_BLK, g, hd).transpose(0, 3, 1, 2, 4)
    bi = jnp.arange(b)[:, None, None, None]
    gi = jnp.arange(g)[None, None, :, None]
    nqb = s // Q_BLOCK

    def to_blocks(a):
        return jnp.moveaxis(a.reshape((a.shape[0], nqb, Q_BLOCK) + a.shape[2:]), 1, 0)

    def one_block(args):
        qb, pb, ib, ob = args
        return slc_attention(qb, pb, ib, ob, kb[bi, gi, ib], vb[bi, gi, ib])

    out = lax.map(one_block, (to_blocks(q), q_pos.reshape(nqb, Q_BLOCK), to_blocks(idx), to_blocks(ok)))
    return jnp.moveaxis(out, 0, 1).reshape(q.shape)


def slc_sample(q, q_pos, idx, ok, pool_k, pool_v, layer, page_table, k_new, v_new):
    db, t, g, hd = k_new.shape
    n_past = PAST_LEN // SLC_BLK
    bpp = PAGE_SIZE // SLC_BLK
    bi = jnp.arange(db)[:, None, None, None]
    gi = jnp.arange(g)[None, None, :, None]
    jc = jnp.minimum(idx, n_past - 1)
    page = page_table[bi, jc // bpp]
    row = (jc % bpp)[..., None] * SLC_BLK + jnp.arange(SLC_BLK)
    nnb = -(-t // SLC_BLK)
    jn = jnp.clip(idx - n_past, 0, nnb - 1)
    is_new = (idx >= n_past)[..., None, None]

    def gather(pool, new):
        from_pool = pool[layer, page[..., None], row, gi[..., None]]
        newb = jnp.pad(new, ((0, 0), (0, nnb * SLC_BLK - t), (0, 0), (0, 0)))
        newb = newb.reshape(db, nnb, SLC_BLK, g, hd).transpose(0, 3, 1, 2, 4)
        return jnp.where(is_new, newb[bi, gi, jn], from_pool)

    return slc_attention(q, q_pos, idx, ok, gather(pool_k, k_new), gather(pool_v, v_new))


def win_prompt(q, kw, vw):
    b, s, g, hpg, hd = q.shape
    nqb = s // Q_BLOCK
    nw = WINDOW // Q_BLOCK
    pad = ((0, 0), (WINDOW, 0), (0, 0), (0, 0))
    kb = jnp.pad(kw, pad).reshape(b, nw + nqb, Q_BLOCK, g, hd)
    vb = jnp.pad(vw, pad).reshape(b, nw + nqb, Q_BLOCK, g, hd)
    kband = jnp.concatenate([kb[:, j:j + nqb] for j in range(nw + 1)], axis=2)
    vband = jnp.concatenate([vb[:, j:j + nqb] for j in range(nw + 1)], axis=2)
    qb = q.reshape(b, nqb, Q_BLOCK, g, hpg, hd)
    sc = jnp.einsum('bnqghd,bnkgd->bnqghk', qb, kband) * ATTN_SCALE
    start = jnp.arange(nqb)[:, None] * Q_BLOCK
    qpos = start + jnp.arange(Q_BLOCK)
    kpos = start - WINDOW + jnp.arange((nw + 1) * Q_BLOCK)
    kp, qp = kpos[:, None, :], qpos[:, :, None]
    mask = (kp <= qp) & (kp > qp - WINDOW) & (kp >= 0)
    p = masked_softmax(sc, mask[None, :, :, None, None, :], axis=-1)
    o = jnp.einsum('bnqghk,bnkgd->bnqghd', p, vband)
    return o.reshape(b, s, g, hpg, hd)


def win_sample(q, q_pos, buf_k, buf_v, k_new, v_new):
    wb = buf_k.shape[1]
    t = k_new.shape[1]
    ka = jnp.concatenate([buf_k.astype(k_new.dtype), k_new], axis=1)
    va = jnp.concatenate([buf_v.astype(v_new.dtype), v_new], axis=1)
    kpos = PAST_LEN - wb + jnp.arange(wb + t)
    mask = (kpos[None, :] <= q_pos[:, None]) & (kpos[None, :] > q_pos[:, None] - WINDOW)
    sc = jnp.einsum('btghd,bkgd->btghk', q, ka) * ATTN_SCALE
    p = masked_softmax(sc, mask[None, :, None, None, :], axis=-1)
    o = jnp.einsum('btghk,bkgd->btghd', p, va)
    return o, ka[:, t:], va[:, t:]


def nsa_combine(gates, o_c, o_s, o_w):
    b, t, _ = gates.shape
    gt = jax.nn.sigmoid(gates.astype(jnp.float32)).reshape(b, t, 3, NSA_KV_HEADS, NSA_HPG, 1)
    o = gt[:, :, 0] * o_c + gt[:, :, 1] * o_s + gt[:, :, 2] * o_w
    return o.reshape(b, t, NSA_Q)


def merge_branches(x, o_a, o_b, ga, gb, w_o):
    m = (jax.nn.sigmoid(ga.astype(jnp.float32)) * o_a.astype(jnp.float32)
         + jax.nn.sigmoid(gb.astype(jnp.float32)) * o_b.astype(jnp.float32))
    return (x + m.astype(x.dtype) @ w_o).astype(x.dtype)


def gather_pages(pool, layer, page_table):
    rows = pool[layer, page_table]
    return rows.reshape(rows.shape[0], -1, rows.shape[3], rows.shape[4])


def token_mix_prompt(x, lp):
    b, s, _ = x.shape
    pos = jnp.arange(s)
    (qkv, a, bb, z, q, kc, vc, ks, vs, kw, vw, nsa_g, ga, gb) = mixer_inputs(x, lp)
    conv0 = jnp.zeros((b, GDN_CONV - 1, GDN_CONV_DIM), x.dtype)
    rec0 = jnp.zeros((b, GDN_HEADS, GDN_DK, GDN_DV), jnp.float32)
    o_a, conv_new, rec_new = gdn_mixer(qkv, a, bb, z, conv0, rec0, lp)
    heads = lambda a_: a_.reshape(b, s, NSA_KV_HEADS, NSA_HD)
    qn, qr, ks_r, kw_r = nsa_prep(q, ks, kw, pos, lp)
    kc_h, vc_h = heads(kc), heads(vc)
    kcc, vcc = compressed_kv(subblock_proj(kc_h, lp['cmp_w1_k']), subblock_proj(vc_h, lp['cmp_w1_v']), lp)
    o_c, sel = cmp_select_prompt(qn, kcc, vcc)
    o_s = slc_attention_prompt(qr, sel, ks_r.astype(BF16), vs.astype(BF16))
    o_w = win_attention_prompt(qr, kw_r.astype(BF16), vw.astype(BF16))
    o_b = nsa_combine_flat(nsa_g, o_c, o_s, o_w)
    y = merge_branches(x, o_a, o_b, ga, gb, lp['w_o'])
    wb = min(WINDOW, s)
    return y, (conv_new, rec_new, kc_h, vc_h, heads(ks_r), heads(vs),
               heads(kw_r)[:, s - wb:], heads(vw)[:, s - wb:])


def token_mix_sample(x, layer, conv_buf, rec, cache_k_cmp, cache_v_cmp, cache_k_slc, cache_v_slc,
                     win_k, win_v, page_table, lp):
    db, t, _ = x.shape
    pos = PAST_LEN + jnp.arange(t)
    (qkv, a, bb, z, q, kc, vc, ks, vs, kw, vw, nsa_g, ga, gb) = mixer_inputs(x, lp)
    o_a, conv_new, rec_new = gdn_mixer(qkv, a, bb, z, conv_buf, rec, lp)
    assert t == 1 and t < CMP_STRIDE
    q_pos = PAST_LEN
    heads = lambda a_: a_.reshape(db, -1, NSA_KV_HEADS, NSA_HD)
    as_rows = lambda a_: a_.reshape(1, db, -1)
    qn, qr, ks_r, kw_r = [a_.reshape(db, 1, -1) for a_ in
                          nsa_prep(as_rows(q), as_rows(ks), as_rows(kw), jnp.full((db,), q_pos), lp)]
    (o_c, imp), nc = cmp_attention_sample(qn, cache_k_cmp[layer], cache_v_cmp[layer], page_table, lp, q_pos)
    idx = select_blocks_sample(imp, nc, q_pos, PAST_LEN + t)
    o_s = slc_attention_sample(qr, idx, cache_k_slc[layer], cache_v_slc[layer], page_table, ks_r, vs)
    wb = win_k.shape[1]
    o_w, win_k_new, win_v_new = win_attention_sample(
        qr, win_k.reshape(db, wb * NSA_KV_HEADS, NSA_HD), win_v.reshape(db, wb * NSA_KV_HEADS, NSA_HD),
        kw_r, vw, q_pos)
    o_b = nsa_combine_flat(nsa_g, o_c, o_s, o_w)
    y = merge_branches(x, o_a, o_b, ga, gb, lp['w_o'])
    return y, (conv_new, rec_new, heads(kc), heads(vc), heads(ks_r), heads(vs),
               heads(win_k_new), heads(win_v_new))


def kernel(x_prompt, x_sample, state_gdn_conv, state_gdn_rec, cache_k_cmp, cache_v_cmp, cache_k_slc, cache_v_slc, cache_k_win, cache_v_win, page_table, ffn1_norm, ffn1_w_in, ffn1_w_out, mix_norm, w_in, gdn_conv_w, gdn_a_log, gdn_dt_bias, gdn_out_norm, nsa_q_norm, nsa_k_norm_cmp, nsa_k_norm_slc, nsa_k_norm_win, cmp_w1_k, cmp_w2_k, cmp_pe_k, cmp_w1_v, cmp_w2_v, cmp_pe_v, w_o, ffn2_norm, ffn2_w_in, ffn2_w_out):
    depth = w_in.shape[0]
    yp, ys = x_prompt, x_sample
    p_list, s_list = [], []
    for l in range(depth):
        lp = {'mix_norm': mix_norm[l], 'w_in': w_in[l], 'gdn_conv_w': gdn_conv_w[l],
              'gdn_a_log': gdn_a_log[l], 'gdn_dt_bias': gdn_dt_bias[l], 'gdn_out_norm': gdn_out_norm[l],
              'nsa_q_norm': nsa_q_norm[l], 'nsa_k_norm_cmp': nsa_k_norm_cmp[l],
              'nsa_k_norm_slc': nsa_k_norm_slc[l], 'nsa_k_norm_win': nsa_k_norm_win[l],
              'cmp_w1_k': cmp_w1_k[l], 'cmp_w2_k': cmp_w2_k[l], 'cmp_pe_k': cmp_pe_k[l],
              'cmp_w1_v': cmp_w1_v[l], 'cmp_w2_v': cmp_w2_v[l], 'cmp_pe_v': cmp_pe_v[l], 'w_o': w_o[l]}
        yp = swiglu_half(yp, ffn1_norm[l], ffn1_w_in[l], ffn1_w_out[l])
        yp, st_p = token_mix_prompt(yp, lp)
        yp = swiglu_half(yp, ffn2_norm[l], ffn2_w_in[l], ffn2_w_out[l])
        ys = swiglu_half(ys, ffn1_norm[l], ffn1_w_in[l], ffn1_w_out[l])
        ys, st_s = token_mix_sample(ys, l, state_gdn_conv[l], state_gdn_rec[l], cache_k_cmp, cache_v_cmp,
                                    cache_k_slc, cache_v_slc, cache_k_win[l], cache_v_win[l], page_table, lp)
        ys = swiglu_half(ys, ffn2_norm[l], ffn2_w_in[l], ffn2_w_out[l])
        p_list.append(st_p)
        s_list.append(st_s)
    (p_conv, p_rec, p_kc, p_vc, p_ks, p_vs, p_kw, p_vw) = [jnp.stack(a) for a in zip(*p_list)]
    (s_conv, s_rec, s_kc, s_vc, s_ks, s_vs, s_kw, s_vw) = [jnp.stack(a) for a in zip(*s_list)]
    return (yp, ys, p_conv, p_rec, p_kc, p_vc, p_ks, p_vs, p_kw, p_vw,
            s_conv, s_rec, s_kc, s_vc, s_ks, s_vs, s_kw, s_vw)
```

```python
import functools
import math

import numpy as np
import jax
import jax.numpy as jnp
from jax import lax
from jax.experimental import pallas as pl
from jax.experimental.pallas import tpu as pltpu

D_MODEL = 1024
PAST_LEN = 8192
PAGE_SIZE = 128

GDN_DK = 128
GDN_DV = 128
GDN_HEADS = D_MODEL // GDN_DV
GDN_QK = GDN_HEADS * GDN_DK
GDN_V = GDN_HEADS * GDN_DV
GDN_CONV_DIM = 2 * GDN_QK + GDN_V
GDN_CONV = 4
GDN_CHUNK = 64

NSA_HD = 128
NSA_HEADS = D_MODEL // NSA_HD
NSA_KV_HEADS = NSA_HEADS // 4
NSA_HPG = NSA_HEADS // NSA_KV_HEADS
NSA_Q = NSA_HEADS * NSA_HD
NSA_KV = NSA_KV_HEADS * NSA_HD
CMP_LEN = 32
CMP_STRIDE = 16
SLC_BLK = 64
SLC_TOPN = 16
WINDOW = 512
Q_BLOCK = 128
ATTN_SCALE = NSA_HD ** -0.5
FORCE_SCORE = 1e6

ROPE_DIM = NSA_HD // 4
ROPE_THETA = 500000.0

D_FF = ((8 * D_MODEL // 3 + 127) // 128) * 128
EPS = 1e-6

IN_SIZES = (GDN_CONV_DIM, GDN_HEADS, GDN_HEADS, GDN_V,
            NSA_Q, NSA_KV, NSA_KV, NSA_KV, NSA_KV, NSA_KV, NSA_KV, 3 * NSA_HEADS,
            D_MODEL, D_MODEL)

MXU_N = 256
VMEM_LIMIT_BYTES = 56 * 1024 * 1024

BF16 = jnp.bfloat16
F32 = jnp.float32


def _const_spec(shape):
    return pl.BlockSpec(shape, lambda *_: (0,) * len(shape), pipeline_mode=pl.Buffered(1))


def _ffn_body(x_ref, g_ref, wg_ref, wu_ref, wo_ref, o_ref):
    x = x_ref[...]
    ms = jnp.mean(x * x, axis=-1, keepdims=True)
    h = (x * lax.rsqrt(ms + EPS) * g_ref[...]).astype(BF16)
    acc = jnp.zeros(x.shape, F32)
    for c in range(D_FF // MXU_N):
        cols = slice(c * MXU_N, (c + 1) * MXU_N)
        gate = jnp.dot(h, wg_ref[:, cols], preferred_element_type=F32)
        up = jnp.dot(h, wu_ref[:, cols], preferred_element_type=F32)
        act = (gate * jax.nn.sigmoid(gate) * up).astype(BF16)
        acc = acc + jnp.dot(act, wo_ref[cols, :], preferred_element_type=F32)
    o_ref[...] = x + 0.5 * acc


def _ffn_tile(n_rows):
    return min(n_rows, 512)


def swiglu_half(x, g, w_in, w_out):
    shape = x.shape
    x2 = x.reshape(-1, D_MODEL)
    n = x2.shape[0]
    tm = _ffn_tile(n)
    wg = w_in[:, :D_FF].astype(BF16)
    wu = w_in[:, D_FF:].astype(BF16)
    wo = w_out.astype(BF16)
    out = pl.pallas_call(
        _ffn_body,
        grid=(n // tm,),
        in_specs=[pl.BlockSpec((tm, D_MODEL), lambda i: (i, 0)),
                  _const_spec((1, D_MODEL)),
                  _const_spec((D_MODEL, D_FF)),
                  _const_spec((D_MODEL, D_FF)),
                  _const_spec((D_FF, D_MODEL))],
        out_specs=pl.BlockSpec((tm, D_MODEL), lambda i: (i, 0)),
        out_shape=jax.ShapeDtypeStruct((n, D_MODEL), F32),
        compiler_params=pltpu.CompilerParams(dimension_semantics=("arbitrary",),
                                             vmem_limit_bytes=VMEM_LIMIT_BYTES),
        name="swiglu_half",
    )(x2, g.reshape(1, D_MODEL), wg, wu, wo)
    return out.reshape(shape)


LANE = 128
CB_QKV = 0
CB_Z = CB_QKV + GDN_CONV_DIM // LANE
CB_Q = CB_Z + GDN_V // LANE
CB_GA = CB_Q + NSA_Q // LANE
CB_GB = CB_GA + D_MODEL // LANE
CB_KV = CB_GB + D_MODEL // LANE
CB_SMALL = CB_KV + 6 * NSA_KV // LANE
P_COLS = (CB_SMALL + 1) * LANE
SMALL_A, SMALL_B, SMALL_G = 0, GDN_HEADS, 2 * GDN_HEADS
PROJ_CHUNKS = 3
assert (P_COLS // LANE) % PROJ_CHUNKS == 0 and 2 * GDN_HEADS + 3 * NSA_HEADS <= LANE


def _permuted_w_in(w_in):
    offs = np.cumsum((0,) + IN_SIZES)
    seg = lambda i: np.arange(offs[i], offs[i + 1])
    order = np.concatenate([seg(0), seg(3), seg(4), seg(12), seg(13)] + [seg(i) for i in range(5, 11)]
                           + [seg(1), seg(2), seg(11)])
    w = w_in[:, order]
    return jnp.pad(w, ((0, 0), (0, P_COLS - w.shape[1]))).astype(BF16)


def _proj_body(x_ref, g_ref, w_ref, o_ref):
    x = x_ref[...]
    ms = jnp.mean(x * x, axis=-1, keepdims=True)
    h = (x * lax.rsqrt(ms + EPS) * g_ref[...]).astype(BF16)
    o_ref[...] = jnp.dot(h, w_ref[...], preferred_element_type=F32)


def mixer_projection(x2, g, w_in):
    n = x2.shape[0]
    tm = min(n, 512)
    wc = P_COLS // PROJ_CHUNKS
    return pl.pallas_call(
        _proj_body,
        grid=(PROJ_CHUNKS, n // tm),
        in_specs=[pl.BlockSpec((tm, D_MODEL), lambda c, i: (i, 0)),
                  pl.BlockSpec((1, D_MODEL), lambda c, i: (0, 0)),
                  pl.BlockSpec((D_MODEL, wc), lambda c, i: (0, c))],
        out_specs=pl.BlockSpec((tm, wc), lambda c, i: (i, c)),
        out_shape=jax.ShapeDtypeStruct((n, P_COLS), F32),
        compiler_params=pltpu.CompilerParams(dimension_semantics=("arbitrary", "arbitrary"),
                                             vmem_limit_bytes=VMEM_LIMIT_BYTES),
        name="mixer_projection",
    )(x2, g.reshape(1, D_MODEL), _permuted_w_in(w_in))


GDN_TILE = 128
GDN_HB = 4
TN_DIMS = (((0,), (0,)), ((), ()))


def _split3(x):
    p1 = x.astype(BF16)
    r1 = x - p1.astype(F32)
    p2 = r1.astype(BF16)
    p3 = (r1 - p2.astype(F32)).astype(BF16)
    return p1, p2, p3


def _gdn_gates(small, alog_row, dt_row):
    xa = small + dt_row
    softplus = jnp.maximum(xa, 0.0) + jnp.log(1.0 + jnp.exp(-jnp.abs(xa)))
    return -jnp.exp(alog_row) * softplus, jax.nn.sigmoid(small)


def _lane_col(x, lane_idx):
    lane = lax.broadcasted_iota(jnp.int32, x.shape, 1)
    return jnp.sum(jnp.where(lane == lane_idx, x, 0.0), axis=1, keepdims=True)


def _gdn_prompt_body(q_ref, k_ref, v_ref, z_ref, sm_ref, wq_ref, wk_ref, wv_ref, alog_ref, dt_ref, gn_ref,
                     o_ref, rec_ref, s_ref, pq_ref, pk_ref, pv_ref, *, tile, hb):
    c = GDN_CHUNK
    hblk, step = pl.program_id(1), pl.program_id(2)
    width = hb * GDN_DK

    @pl.when(step == 0)
    def _():
        s_ref[...] = jnp.zeros(s_ref.shape, F32)
        for r in (pq_ref, pk_ref, pv_ref):
            r[...] = jnp.zeros(r.shape, F32)

    row8 = lax.broadcasted_iota(jnp.int32, (8, width), 0)

    def conv(x_ref, prev_ref, w_ref):
        x, prev, w = x_ref[0], prev_ref[...], w_ref[...]
        acc = x * w[GDN_CONV - 1:GDN_CONV]
        for j in range(1, GDN_CONV):
            main = pltpu.roll(x, j, 0)
            first = jnp.where(row8 < j, pltpu.roll(prev, j, 0), main[:8])
            acc = acc + jnp.concatenate([first, main[8:]], axis=0) * w[GDN_CONV - 1 - j:GDN_CONV - j]
        prev_ref[...] = x[tile - 8:]
        return acc * jax.nn.sigmoid(acc)

    qc, kc, vc = conv(q_ref, pq_ref, wq_ref), conv(k_ref, pk_ref, wk_ref), conv(v_ref, pv_ref, wv_ref)
    z = z_ref[0]
    g_all, beta_all = _gdn_gates(sm_ref[0], alog_ref[...], dt_ref[...])

    ii = lax.broadcasted_iota(jnp.int32, (c, c), 0)
    jj = lax.broadcasted_iota(jnp.int32, (c, c), 1)
    incl, strict = ii >= jj, ii > jj
    tri = jnp.where(incl, 1.0, 0.0).astype(BF16)
    er = lax.broadcasted_iota(jnp.int32, (8, LANE), 0)
    el = lax.broadcasted_iota(jnp.int32, (8, LANE), 1)
    head_rows = jnp.where((el == SMALL_A + hblk * hb + er) & (er < hb), 1.0, 0.0).astype(BF16)

    for ci in range(tile // c):
        rows = slice(ci * c, (ci + 1) * c)
        gc_all = sum(jnp.dot(tri, piece, preferred_element_type=F32) for piece in _split3(g_all[rows]))
        gc_rows = sum(lax.dot_general(head_rows, piece, NT_DIMS, preferred_element_type=F32)
                      for piece in _split3(gc_all))
        for hh in range(hb):
            cols = slice(hh * GDN_DK, (hh + 1) * GDN_DK)
            head = hblk * hb + hh
            gcol = _lane_col(gc_all, SMALL_A + head)
            bcol = _lane_col(beta_all[rows], SMALL_B + head)
            q, k, v = qc[rows, cols], kc[rows, cols], vc[rows, cols]
            q = q * lax.rsqrt(jnp.sum(q * q, axis=-1, keepdims=True) + EPS) * (GDN_DK ** -0.5)
            k = k * lax.rsqrt(jnp.sum(k * k, axis=-1, keepdims=True) + EPS)
            diff = gcol - gc_rows[hh:hh + 1]
            decay = jnp.where(incl, jnp.exp(jnp.where(incl, diff, 0.0)), 0.0)
            kb = k * bcol
            k16 = k.astype(BF16)
            m = jnp.where(strict, lax.dot_general(kb.astype(BF16), k16, NT_DIMS, preferred_element_type=F32) * decay,
                          0.0)
            egc = jnp.exp(gcol)
            rhs = jnp.concatenate([v * bcol, kb * egc], axis=1)
            m1, m2, _ = _split3(m)
            r1, r2, _ = _split3(rhs)
            sol = rhs - (jnp.dot(m1, r1, preferred_element_type=F32) + jnp.dot(m1, r2, preferred_element_type=F32)
                         + jnp.dot(m2, r1, preferred_element_type=F32))
            mp = m1
            for _ in range(int(math.log2(c)) - 1):
                mp = jnp.dot(mp, mp, preferred_element_type=F32).astype(BF16)
                sol = sol + jnp.dot(mp, sol.astype(BF16), preferred_element_type=F32)
            u, w = sol[:, :GDN_DV], sol[:, GDN_DV:]
            attn = jnp.where(incl, lax.dot_general(q.astype(BF16), k16, NT_DIMS, preferred_element_type=F32) * decay,
                             0.0)
            s_old = s_ref[hh]
            s16 = s_old.astype(BF16)
            v_new = u - jnp.dot(w.astype(BF16), s16, preferred_element_type=F32)
            o = (jnp.dot((q * egc).astype(BF16), s16, preferred_element_type=F32)
                 + jnp.dot(attn.astype(BF16), v_new.astype(BF16), preferred_element_type=F32))
            g_last = gcol[c - 1:c]
            k_dec = k * jnp.exp(g_last - gcol)
            s_ref[hh] = s_old * jnp.exp(g_last) + lax.dot_general(
                k_dec.astype(BF16), v_new.astype(BF16), TN_DIMS, preferred_element_type=F32)
            zc = z[rows, cols]
            o_ref[0, rows, cols] = _head_norm(o, gn_ref[...]) * (zc * jax.nn.sigmoid(zc))

    @pl.when(step == pl.num_programs(2) - 1)
    def _():
        rec_ref[0] = s_ref[...]


def _lane_row(vec, offset):
    return jnp.pad(vec.astype(F32), (offset, LANE - offset - vec.shape[0])).reshape(1, LANE)


def gdn_prompt(p, lp):
    b, s, _ = p.shape
    tile, hb = GDN_TILE, GDN_HB
    assert s % tile == 0 and tile % GDN_CHUNK == 0 and GDN_HEADS % hb == 0 and GDN_CHUNK & (GDN_CHUNK - 1) == 0
    width = hb * GDN_DK
    wb = width // LANE
    col = lambda cb0: pl.BlockSpec((1, tile, width), lambda bi, h, i: (bi, i, cb0 // wb + h))
    wcol = lambda cb0: pl.BlockSpec((GDN_CONV, width), lambda bi, h, i: (0, cb0 // wb + h))
    vec = pl.BlockSpec((1, LANE), lambda bi, h, i: (0, 0))
    qk_blocks = GDN_QK // LANE
    return pl.pallas_call(
        functools.partial(_gdn_prompt_body, tile=tile, hb=hb),
        grid=(b, GDN_HEADS // hb, s // tile),
        in_specs=[col(CB_QKV), col(CB_QKV + qk_blocks), col(CB_QKV + 2 * qk_blocks), col(CB_Z),
                  pl.BlockSpec((1, tile, LANE), lambda bi, h, i: (bi, i, CB_SMALL)),
                  wcol(0), wcol(qk_blocks), wcol(2 * qk_blocks), vec, vec, vec],
        out_specs=[pl.BlockSpec((1, tile, width), lambda bi, h, i: (bi, i, h)),
                   pl.BlockSpec((1, hb, GDN_DK, GDN_DV), lambda bi, h, i: (bi, h, 0, 0))],
        out_shape=[jax.ShapeDtypeStruct((b, s, GDN_V), F32),
                   jax.ShapeDtypeStruct((b, GDN_HEADS, GDN_DK, GDN_DV), F32)],
        scratch_shapes=[pltpu.VMEM((hb, GDN_DK, GDN_DV), F32)] + [pltpu.VMEM((8, width), F32)] * 3,
        compiler_params=pltpu.CompilerParams(dimension_semantics=("arbitrary", "arbitrary", "arbitrary"),
                                             vmem_limit_bytes=VMEM_LIMIT_BYTES),
        name="gdn_prompt",
    )(p, p, p, p, p, lp['gdn_conv_w'], lp['gdn_conv_w'], lp['gdn_conv_w'],
      _lane_row(lp['gdn_a_log'], SMALL_A), _lane_row(lp['gdn_dt_bias'], SMALL_A),
      lp['gdn_out_norm'].reshape(1, GDN_DV))


NEG_INF = float("-inf")
NEG_BIG = -1e30
NT_DIMS = (((1,), (1,)), ((), ()))


def _rope_tables(pos):
    half = ROPE_DIM // 2
    inv = jnp.float32(ROPE_THETA) ** (-jnp.arange(half, dtype=jnp.float32) / half)
    ang = pos.astype(jnp.float32)[:, None] * inv
    cos, sin = jnp.cos(ang), jnp.sin(ang)
    rest = NSA_HD - ROPE_DIM
    cos_t = jnp.concatenate([cos, cos, jnp.ones((pos.shape[0], rest), F32)], axis=-1)
    sin_t = jnp.concatenate([-sin, sin, jnp.zeros((pos.shape[0], rest), F32)], axis=-1)
    return cos_t, sin_t


def _rope(y, cos_t, sin_t):
    half = ROPE_DIM // 2
    lane = lax.broadcasted_iota(jnp.int32, y.shape, 1)
    partner = jnp.where(lane < half, pltpu.roll(y, NSA_HD - half, 1), pltpu.roll(y, half, 1))
    return y * cos_t + partner * sin_t


def _head_norm(x, g):
    ms = jnp.mean(x * x, axis=-1, keepdims=True)
    return x * lax.rsqrt(ms + EPS) * g


def _nsa_prep_body(q_ref, ks_ref, kw_ref, cos_ref, sin_ref, gq_ref, gks_ref, gkw_ref,
                   qn_ref, qr_ref, ksr_ref, kwr_ref):
    cos_t, sin_t = cos_ref[...], sin_ref[...]
    for h in range(NSA_HEADS):
        cols = slice(h * NSA_HD, (h + 1) * NSA_HD)
        y = _head_norm(q_ref[0, :, cols], gq_ref[...])
        qn_ref[0, :, cols] = (y * ATTN_SCALE).astype(BF16)
        qr_ref[0, :, cols] = (_rope(y, cos_t, sin_t) * ATTN_SCALE).astype(BF16)
    for h in range(NSA_KV_HEADS):
        cols = slice(h * NSA_HD, (h + 1) * NSA_HD)
        ksr_ref[0, :, cols] = _rope(_head_norm(ks_ref[0, :, cols], gks_ref[...]), cos_t, sin_t)
        kwr_ref[0, :, cols] = _rope(_head_norm(kw_ref[0, :, cols], gkw_ref[...]), cos_t, sin_t)


def _col_spec(rows, width, col_block):
    assert (col_block * LANE) % width == 0
    cb = col_block * LANE // width
    return pl.BlockSpec((1, rows, width), lambda bi, i: (bi, i, cb))


def nsa_prep(p, pos, lp):
    b, t, _ = p.shape
    tp = min(t, 512)
    cos_t, sin_t = _rope_tables(pos)
    row = lambda w: pl.BlockSpec((1, tp, w), lambda bi, i: (bi, i, 0))
    kv_w = NSA_KV // LANE
    tab = pl.BlockSpec((tp, NSA_HD), lambda bi, i: (i, 0))
    gain = pl.BlockSpec((1, NSA_HD), lambda bi, i: (0, 0))
    return pl.pallas_call(
        _nsa_prep_body,
        grid=(b, t // tp),
        in_specs=[_col_spec(tp, NSA_Q, CB_Q), _col_spec(tp, NSA_KV, CB_KV + 2 * kv_w),
                  _col_spec(tp, NSA_KV, CB_KV + 4 * kv_w), tab, tab, gain, gain, gain],
        out_specs=[row(NSA_Q), row(NSA_Q), row(NSA_KV), row(NSA_KV)],
        out_shape=[jax.ShapeDtypeStruct((b, t, NSA_Q), BF16), jax.ShapeDtypeStruct((b, t, NSA_Q), BF16),
                   jax.ShapeDtypeStruct((b, t, NSA_KV), F32), jax.ShapeDtypeStruct((b, t, NSA_KV), F32)],
        compiler_params=pltpu.CompilerParams(dimension_semantics=("arbitrary", "arbitrary"),
                                             vmem_limit_bytes=VMEM_LIMIT_BYTES),
        name="nsa_prep",
    )(p, p, p, cos_t, sin_t, lp['nsa_q_norm'].reshape(1, NSA_HD),
      lp['nsa_k_norm_slc'].reshape(1, NSA_HD), lp['nsa_k_norm_win'].reshape(1, NSA_HD))


def _group_queries(q, g):
    return jnp.concatenate(
        [q[:, (g * NSA_HPG + h) * NSA_HD:(g * NSA_HPG + h + 1) * NSA_HD] for h in range(NSA_HPG)], axis=0)


def _store_group(o_ref, o, g, tq):
    for h in range(NSA_HPG):
        c0 = (g * NSA_HPG + h) * NSA_HD
        o_ref[0, :, c0:c0 + NSA_HD] = o[h * tq:(h + 1) * tq]


def _cmp_select_body(q_ref, kcc_ref, vcc_ref, ovl_ref, oc_ref, sel_ref, *, tq, nc, ns, selw):
    ncp = kcc_ref.shape[2]
    q0 = pl.program_id(1) * tq
    q = q_ref[0]
    tpos = q0 + lax.broadcasted_iota(jnp.int32, (tq, ncp), 0)
    cidx = lax.broadcasted_iota(jnp.int32, (tq, ncp), 1)
    vis = (cidx * CMP_STRIDE + (CMP_LEN - 1) <= tpos) & (cidx < nc)
    jrow = lax.broadcasted_iota(jnp.int32, (ns, tq), 0)
    cur = (q0 + lax.broadcasted_iota(jnp.int32, (ns, tq), 1)) // SLC_BLK
    valid = jrow <= cur
    forced = (jrow == 0) | (jrow == cur) | (jrow == cur - 1)
    sel_parts = []
    for g in range(NSA_KV_HEADS):
        s = lax.dot_general(_group_queries(q, g), kcc_ref[0, g], NT_DIMS, preferred_element_type=F32)
        sm = jnp.where(vis[None], s.reshape(NSA_HPG, tq, ncp), NEG_INF)
        m = jnp.max(sm, axis=-1, keepdims=True)
        m = jnp.where(m > NEG_INF, m, 0.0)
        p = jnp.exp(sm - m)
        p = p / jnp.maximum(jnp.sum(p, axis=-1, keepdims=True), 1e-30)
        o = jnp.dot(p.reshape(NSA_HPG * tq, ncp).astype(BF16), vcc_ref[0, g], preferred_element_type=F32)
        _store_group(oc_ref, o, g, tq)
        imp = p[0]
        for h in range(1, NSA_HPG):
            imp = imp + p[h]
        hi = imp.astype(BF16)
        lo = (imp - hi.astype(F32)).astype(BF16)
        score = (lax.dot_general(ovl_ref[...], hi, NT_DIMS, preferred_element_type=F32)
                 + lax.dot_general(ovl_ref[...], lo, NT_DIMS, preferred_element_type=F32))
        work = jnp.where(valid, jnp.where(forced, FORCE_SCORE, score), NEG_INF)
        chosen = jnp.zeros((ns, tq), F32)
        for _ in range(min(SLC_TOPN, ns)):
            mx = jnp.max(work, axis=0, keepdims=True)
            first = jnp.min(jnp.where(work == mx, jrow, ns), axis=0, keepdims=True)
            hit = jrow == first
            chosen = jnp.where(hit & (mx > NEG_INF), 1.0, chosen)
            work = jnp.where(hit, NEG_INF, work)
        sel_parts.append(chosen)
    if selw > NSA_KV_HEADS * ns:
        sel_parts.append(jnp.zeros((selw - NSA_KV_HEADS * ns, tq), F32))
    sel_ref[0] = jnp.concatenate(sel_parts, axis=0).T.astype(BF16)


def _sel_width(ns):
    return -(-NSA_KV_HEADS * ns // 128) * 128


def cmp_select_prompt(qn, kcc, vcc, nc):
    b, s, _ = qn.shape
    ncp = kcc.shape[2]
    assert ncp % LANE == 0
    ns = -(-s // SLC_BLK)
    selw = _sel_width(ns)
    tq = min(s, 256)
    cstart = np.arange(ncp) * CMP_STRIDE
    sstart = np.arange(ns) * SLC_BLK
    ovl = ((cstart[None, :] < sstart[:, None] + SLC_BLK) & (cstart[None, :] + CMP_LEN > sstart[:, None])
           & (np.arange(ncp)[None, :] < nc))
    ovl_t = jnp.asarray(ovl, dtype=BF16)
    kv_spec = pl.BlockSpec((1, NSA_KV_HEADS, ncp, NSA_HD), lambda bi, i: (bi, 0, 0, 0))
    return pl.pallas_call(
        functools.partial(_cmp_select_body, tq=tq, nc=nc, ns=ns, selw=selw),
        grid=(b, s // tq),
        in_specs=[pl.BlockSpec((1, tq, NSA_Q), lambda bi, i: (bi, i, 0)), kv_spec, kv_spec,
                  pl.BlockSpec((ns, ncp), lambda bi, i: (0, 0))],
        out_specs=[pl.BlockSpec((1, tq, NSA_Q), lambda bi, i: (bi, i, 0)),
                   pl.BlockSpec((1, tq, selw), lambda bi, i: (bi, i, 0))],
        out_shape=[jax.ShapeDtypeStruct((b, s, NSA_Q), F32), jax.ShapeDtypeStruct((b, s, selw), BF16)],
        compiler_params=pltpu.CompilerParams(dimension_semantics=("arbitrary", "arbitrary"),
                                             vmem_limit_bytes=VMEM_LIMIT_BYTES),
        name="cmp_select_prompt",
    )(qn, kcc, vcc, ovl_t)


def _slc_body(q_ref, sel_ref, k_ref, v_ref, o_ref, *, tq, tk, ns):
    selw = sel_ref.shape[2]
    q0 = pl.program_id(1) * tq
    n_kv = (q0 + tq + tk - 1) // tk
    q = q_ref[0]
    selm = sel_ref[0]
    qpos = q0 + lax.broadcasted_iota(jnp.int32, (tq, tk), 0)
    kio = lax.broadcasted_iota(jnp.int32, (tq, tk), 1)
    erow = lax.broadcasted_iota(jnp.int32, (selw, tk), 0)
    ecol = lax.broadcasted_iota(jnp.int32, (selw, tk), 1)
    for g in range(NSA_KV_HEADS):
        q4 = _group_queries(q, g)
        gcols = slice(g * NSA_HD, (g + 1) * NSA_HD)

        def body(j, carry, q4=q4, gcols=gcols, g=g):
            m, l, acc = carry
            k0 = pl.multiple_of(j * tk, tk)
            kt = k_ref[0, pl.ds(k0, tk), gcols]
            vt = v_ref[0, pl.ds(k0, tk), gcols]
            s = lax.dot_general(q4, kt, NT_DIMS, preferred_element_type=F32).reshape(NSA_HPG, tq, tk)
            expand = jnp.where(erow - g * ns == (ecol + k0) // SLC_BLK, 1.0, 0.0).astype(BF16)
            chosen = jnp.dot(selm, expand, preferred_element_type=F32)
            allowed = ((chosen > 0.5) & (kio + k0 <= qpos))[None]
            m_new = jnp.maximum(m, jnp.max(jnp.where(allowed, s, NEG_BIG), axis=-1, keepdims=True))
            p = jnp.where(allowed, jnp.exp(s - m_new), 0.0)
            alpha = jnp.exp(m - m_new)
            l = alpha * l + jnp.sum(p, axis=-1, keepdims=True)
            pv = jnp.dot(p.reshape(NSA_HPG * tq, tk).astype(BF16), vt, preferred_element_type=F32)
            acc = alpha.reshape(NSA_HPG * tq, 1) * acc + pv
            return m_new, l, acc

        init = (jnp.full((NSA_HPG, tq, 1), NEG_BIG, F32), jnp.zeros((NSA_HPG, tq, 1), F32),
                jnp.zeros((NSA_HPG * tq, NSA_HD), F32))
        _, l, acc = lax.fori_loop(0, n_kv, body, init)
        _store_group(o_ref, acc / jnp.maximum(l.reshape(NSA_HPG * tq, 1), 1e-30), g, tq)


def slc_attention_prompt(qr, sel, k, v):
    b, s, _ = qr.shape
    selw = sel.shape[2]
    ns = -(-s // SLC_BLK)
    tq = min(s, 256)
    tk = min(s, 512)
    kv_spec = pl.BlockSpec((1, s, NSA_KV), lambda bi, i: (bi, 0, 0))
    return pl.pallas_call(
        functools.partial(_slc_body, tq=tq, tk=tk, ns=ns),
        grid=(b, s // tq),
        in_specs=[pl.BlockSpec((1, tq, NSA_Q), lambda bi, i: (bi, i, 0)),
                  pl.BlockSpec((1, tq, selw), lambda bi, i: (bi, i, 0)), kv_spec, kv_spec],
        out_specs=pl.BlockSpec((1, tq, NSA_Q), lambda bi, i: (bi, i, 0)),
        out_shape=jax.ShapeDtypeStruct((b, s, NSA_Q), F32),
        compiler_params=pltpu.CompilerParams(dimension_semantics=("arbitrary", "arbitrary"),
                                             vmem_limit_bytes=VMEM_LIMIT_BYTES),
        name="slc_attention_prompt",
    )(qr, sel, k, v)


def _win_body(q_ref, k_ref, v_ref, o_ref, *, tq, span):
    q0 = pl.program_id(1) * tq
    kstart = pl.multiple_of(jnp.maximum(q0 - WINDOW, 0), tq)
    q = q_ref[0]
    qpos = q0 + lax.broadcasted_iota(jnp.int32, (tq, span), 0)
    kpos = kstart + lax.broadcasted_iota(jnp.int32, (tq, span), 1)
    allowed = ((kpos <= qpos) & (kpos > qpos - WINDOW))[None]
    for g in range(NSA_KV_HEADS):
        gcols = slice(g * NSA_HD, (g + 1) * NSA_HD)
        kt = k_ref[0, pl.ds(kstart, span), gcols]
        vt = v_ref[0, pl.ds(kstart, span), gcols]
        s = lax.dot_general(_group_queries(q, g), kt, NT_DIMS, preferred_element_type=F32)
        sm = jnp.where(allowed, s.reshape(NSA_HPG, tq, span), NEG_INF)
        p = jnp.exp(sm - jnp.max(sm, axis=-1, keepdims=True))
        l = jnp.sum(p, axis=-1, keepdims=True)
        o = jnp.dot(p.reshape(NSA_HPG * tq, span).astype(BF16), vt, preferred_element_type=F32)
        _store_group(o_ref, o / jnp.maximum(l.reshape(NSA_HPG * tq, 1), 1e-30), g, tq)


def win_attention_prompt(qr, k, v):
    b, s, _ = qr.shape
    tq = min(s, 256)
    span = min(s, WINDOW + tq)
    assert WINDOW % tq == 0 or s == tq
    kv_spec = pl.BlockSpec((1, s, NSA_KV), lambda bi, i: (bi, 0, 0))
    return pl.pallas_call(
        functools.partial(_win_body, tq=tq, span=span),
        grid=(b, s // tq),
        in_specs=[pl.BlockSpec((1, tq, NSA_Q), lambda bi, i: (bi, i, 0)), kv_spec, kv_spec],
        out_specs=pl.BlockSpec((1, tq, NSA_Q), lambda bi, i: (bi, i, 0)),
        out_shape=jax.ShapeDtypeStruct((b, s, NSA_Q), F32),
        compiler_params=pltpu.CompilerParams(dimension_semantics=("arbitrary", "arbitrary"),
                                             vmem_limit_bytes=VMEM_LIMIT_BYTES),
        name="win_attention_prompt",
    )(qr, k, v)


def nsa_combine_flat(gates, o_c, o_s, o_w):
    gt = jax.nn.sigmoid(gates.astype(jnp.float32))
    out = 0.0
    for br, o in enumerate((o_c, o_s, o_w)):
        out = out + jnp.repeat(gt[..., br * NSA_HEADS:(br + 1) * NSA_HEADS], NSA_HD, axis=-1) * o
    return out


CMP_PAGES_PER_STEP = 16
SUB_PER_PAGE = PAGE_SIZE // CMP_STRIDE
PAGE_ROWS = PAGE_SIZE * NSA_KV_HEADS
CMP_R = CMP_LEN // CMP_STRIDE
Q_ROWS = 8


def _pad_rows(x, rows):
    return jnp.concatenate([x, jnp.zeros((rows - x.shape[0], x.shape[1]), x.dtype)], axis=0)


def _token_group_queries(q, g):
    rows = [q[:, (g * NSA_HPG + h) * NSA_HD:(g * NSA_HPG + h + 1) * NSA_HD] for h in range(NSA_HPG)]
    return _pad_rows(jnp.concatenate(rows, axis=0), Q_ROWS)


def _subblock_weight(w1):
    w = w1.reshape(CMP_R, CMP_STRIDE, NSA_HD, NSA_HD)
    return w.transpose(1, 2, 0, 3).reshape(CMP_STRIDE * NSA_HD, CMP_R * NSA_HD).astype(BF16)


def _pe_rows(pe):
    return _pad_rows(pe.reshape(CMP_R, CMP_STRIDE * NSA_HD), Q_ROWS).astype(BF16)


def _compress_from_parts(parts, pe_rows, w_sub, w2):
    n = parts.shape[0]
    hid = parts[:, :NSA_HD]
    for r in range(1, CMP_R):
        hid = hid + pltpu.roll(parts[:, r * NSA_HD:(r + 1) * NSA_HD], n - r, 0)
    pe_proj = jnp.dot(pe_rows, w_sub, preferred_element_type=F32)
    for r in range(CMP_R):
        hid = hid + pe_proj[r:r + 1, r * NSA_HD:(r + 1) * NSA_HD]
    act = hid * jax.nn.sigmoid(hid)
    return jnp.dot(act.astype(BF16), w2, preferred_element_type=F32)


def _cmp_sample_body(pt_ref, *refs, q_pos, nc):
    npg = CMP_PAGES_PER_STEP
    k_pages, v_pages = refs[:npg], refs[npg:2 * npg]
    (wk_ref, wv_ref, pek_ref, pev_ref, w2k_ref, w2v_ref, gk_ref, q_ref,
     oc_ref, imp_ref, pk_ref, pv_ref) = refs[2 * npg:]
    c = pl.program_id(1)
    rows = npg * SUB_PER_PAGE

    def project(pages, w_ref, parts_ref):
        xg = jnp.concatenate(
            [jnp.concatenate(
                [jnp.concatenate([p[0, pl.ds(s * NSA_KV_HEADS + g, SUB_PER_PAGE, stride=CMP_STRIDE * NSA_KV_HEADS), :]
                                  for s in range(CMP_STRIDE)], axis=1) for p in pages], axis=0)
             for g in range(NSA_KV_HEADS)], axis=0).astype(BF16)
        parts = jnp.dot(xg, w_ref[...], preferred_element_type=F32)
        for g in range(NSA_KV_HEADS):
            parts_ref[g, pl.ds(pl.multiple_of(c * rows, rows), rows), :] = parts[g * rows:(g + 1) * rows]

    project(k_pages, wk_ref, pk_ref)
    project(v_pages, wv_ref, pv_ref)

    @pl.when(c == pl.num_programs(1) - 1)
    def _():
        ncp = pk_ref.shape[1]
        q = q_ref[0]
        cidx = lax.broadcasted_iota(jnp.int32, (Q_ROWS, ncp), 1)
        vis = (cidx < nc) & (cidx * CMP_STRIDE + (CMP_LEN - 1) <= q_pos)
        for g in range(NSA_KV_HEADS):
            kcc = _compress_from_parts(pk_ref[g], pek_ref[...], wk_ref[...], w2k_ref[...])
            kcc = _head_norm(kcc, gk_ref[...])
            vcc = _compress_from_parts(pv_ref[g], pev_ref[...], wv_ref[...], w2v_ref[...])
            s = lax.dot_general(_token_group_queries(q, g), kcc.astype(BF16), NT_DIMS, preferred_element_type=F32)
            sm = jnp.where(vis, s, NEG_INF)
            m = jnp.max(sm, axis=-1, keepdims=True)
            m = jnp.where(m > NEG_INF, m, 0.0)
            p = jnp.exp(sm - m)
            p = p / jnp.maximum(jnp.sum(p, axis=-1, keepdims=True), 1e-30)
            o = jnp.dot(p.astype(BF16), vcc.astype(BF16), preferred_element_type=F32)
            for h in range(NSA_HPG):
                c0 = (g * NSA_HPG + h) * NSA_HD
                oc_ref[0, :, c0:c0 + NSA_HD] = o[h:h + 1]
            imp_ref[0, g:g + 1, :] = jnp.sum(p[:NSA_HPG], axis=0, keepdims=True)


def cmp_attention_sample(qn, pool_k, pool_v, page_table, lp, q_pos):
    db, n_pages = page_table.shape
    n_pool = pool_k.shape[0]
    npg = CMP_PAGES_PER_STEP
    assert n_pages % npg == 0
    n_sub = n_pages * SUB_PER_PAGE
    nc = n_sub - CMP_R + 1
    view = lambda p: p.reshape(n_pool, PAGE_ROWS, NSA_HD)
    page_spec = lambda i: pl.BlockSpec((1, PAGE_ROWS, NSA_HD), lambda b, c, pt: (pt[b, c * npg + i], 0, 0))
    const = lambda shape: pl.BlockSpec(shape, lambda b, c, pt: (0,) * len(shape))
    kdim = CMP_STRIDE * NSA_HD
    grid_spec = pltpu.PrefetchScalarGridSpec(
        num_scalar_prefetch=1,
        grid=(db, n_pages // npg),
        in_specs=([page_spec(i) for i in range(npg)] + [page_spec(i) for i in range(npg)]
                  + [const((kdim, CMP_R * NSA_HD)), const((kdim, CMP_R * NSA_HD)),
                     const((Q_ROWS, kdim)), const((Q_ROWS, kdim)),
                     const((NSA_HD, NSA_HD)), const((NSA_HD, NSA_HD)), const((1, NSA_HD)),
                     pl.BlockSpec((1, 1, NSA_Q), lambda b, c, pt: (b, 0, 0))]),
        out_specs=[pl.BlockSpec((1, 1, NSA_Q), lambda b, c, pt: (b, 0, 0)),
                   pl.BlockSpec((1, NSA_KV_HEADS, n_sub), lambda b, c, pt: (b, 0, 0))],
        scratch_shapes=[pltpu.VMEM((NSA_KV_HEADS, n_sub, CMP_R * NSA_HD), F32),
                        pltpu.VMEM((NSA_KV_HEADS, n_sub, CMP_R * NSA_HD), F32)],
    )
    return pl.pallas_call(
        functools.partial(_cmp_sample_body, q_pos=q_pos, nc=nc),
        grid_spec=grid_spec,
        out_shape=[jax.ShapeDtypeStruct((db, 1, NSA_Q), F32), jax.ShapeDtypeStruct((db, NSA_KV_HEADS, n_sub), F32)],
        compiler_params=pltpu.CompilerParams(dimension_semantics=("arbitrary", "arbitrary"),
                                             vmem_limit_bytes=VMEM_LIMIT_BYTES),
        name="cmp_attention_sample",
    )(page_table, *([view(pool_k)] * npg), *([view(pool_v)] * npg),
      _subblock_weight(lp['cmp_w1_k']), _subblock_weight(lp['cmp_w1_v']),
      _pe_rows(lp['cmp_pe_k']), _pe_rows(lp['cmp_pe_v']),
      lp['cmp_w2_k'].astype(BF16), lp['cmp_w2_v'].astype(BF16),
      lp['nsa_k_norm_cmp'].reshape(1, NSA_HD), qn), nc


def _cmp_kv_prompt_body(*refs, n_sub):
    kc_refs, vc_refs = refs[:NSA_KV_HEADS], refs[NSA_KV_HEADS:2 * NSA_KV_HEADS]
    wk_ref, wv_ref, pek_ref, pev_ref, w2k_ref, w2v_ref, gk_ref, kcc_ref, vcc_ref = refs[2 * NSA_KV_HEADS:]

    def compress(x_ref, w_ref, pe_ref, w2_ref):
        x = jnp.concatenate([x_ref[0, pl.ds(s, n_sub, stride=CMP_STRIDE), :] for s in range(CMP_STRIDE)],
                            axis=1).astype(BF16)
        parts = jnp.dot(x, w_ref[...], preferred_element_type=F32)
        return _compress_from_parts(parts, pe_ref[...], w_ref[...], w2_ref[...])

    for g in range(NSA_KV_HEADS):
        kcc = _head_norm(compress(kc_refs[g], wk_ref, pek_ref, w2k_ref), gk_ref[...])
        kcc_ref[0, g] = kcc.astype(kcc_ref.dtype)
        vcc_ref[0, g] = compress(vc_refs[g], wv_ref, pev_ref, w2v_ref).astype(vcc_ref.dtype)


def compressed_kv_prompt(p, lp):
    b, s, _ = p.shape
    assert s % CMP_STRIDE == 0
    n_sub = s // CMP_STRIDE
    nc = n_sub - CMP_R + 1
    kdim = CMP_STRIDE * NSA_HD
    kv_w = NSA_KV // LANE
    const = lambda shape: pl.BlockSpec(shape, lambda bi: (0,) * len(shape))
    assert NSA_HD == LANE
    src = lambda cb: pl.BlockSpec((1, s, NSA_HD), lambda bi: (bi, 0, cb))
    out = pl.BlockSpec((1, NSA_KV_HEADS, n_sub, NSA_HD), lambda bi: (bi, 0, 0, 0))
    kcc, vcc = pl.pallas_call(
        functools.partial(_cmp_kv_prompt_body, n_sub=n_sub),
        grid=(b,),
        in_specs=[src(CB_KV + g) for g in range(NSA_KV_HEADS)] + [src(CB_KV + kv_w + g) for g in range(NSA_KV_HEADS)] + [
                  const((kdim, CMP_R * NSA_HD)), const((kdim, CMP_R * NSA_HD)),
                  const((Q_ROWS, kdim)), const((Q_ROWS, kdim)),
                  const((NSA_HD, NSA_HD)), const((NSA_HD, NSA_HD)), const((1, NSA_HD))],
        out_specs=[out, out],
        out_shape=[jax.ShapeDtypeStruct((b, NSA_KV_HEADS, n_sub, NSA_HD), BF16)] * 2,
        compiler_params=pltpu.CompilerParams(dimension_semantics=("arbitrary",),
                                             vmem_limit_bytes=VMEM_LIMIT_BYTES),
        name="compressed_kv_prompt",
    )(*([p] * (2 * NSA_KV_HEADS)), _subblock_weight(lp['cmp_w1_k']), _subblock_weight(lp['cmp_w1_v']),
      _pe_rows(lp['cmp_pe_k']), _pe_rows(lp['cmp_pe_v']),
      lp['cmp_w2_k'].astype(BF16), lp['cmp_w2_v'].astype(BF16), lp['nsa_k_norm_cmp'].reshape(1, NSA_HD))
    return kcc, vcc, nc


def _gdn_sample_body(qkv_ref, buf_ref, w_ref, sm_ref, z_ref, rec_ref, alog_ref, dt_ref, gn_ref,
                     o_ref, conv_ref, rec_out_ref):
    x = qkv_ref[0]
    buf = buf_ref[0]
    w = w_ref[...]
    acc = x * w[GDN_CONV - 1:GDN_CONV]
    for j in range(GDN_CONV - 1):
        acc = acc + buf[j:j + 1] * w[j:j + 1]
    for j in range(GDN_CONV - 2):
        conv_ref[0, j:j + 1, :] = buf[j + 1:j + 2]
    conv_ref[0, GDN_CONV - 2:GDN_CONV - 1, :] = x
    cv = acc * jax.nn.sigmoid(acc)
    g_all, beta_all = _gdn_gates(sm_ref[0], alog_ref[...], dt_ref[...])
    z = z_ref[0]
    rows = []
    heads = []
    for h in range(GDN_HEADS):
        q = cv[:, h * GDN_DK:(h + 1) * GDN_DK]
        k = cv[:, GDN_QK + h * GDN_DK:GDN_QK + (h + 1) * GDN_DK]
        v = cv[:, 2 * GDN_QK + h * GDN_DV:2 * GDN_QK + (h + 1) * GDN_DV]
        q = q * lax.rsqrt(jnp.sum(q * q, axis=-1, keepdims=True) + EPS) * (GDN_DK ** -0.5)
        k = k * lax.rsqrt(jnp.sum(k * k, axis=-1, keepdims=True) + EPS)
        g = g_all[:, SMALL_A + h:SMALL_A + h + 1]
        beta = beta_all[:, SMALL_B + h:SMALL_B + h + 1]
        eg = jnp.exp(g)
        heads.append((q, k, v * beta, eg, jnp.sum(q * k, axis=-1, keepdims=True)))
        rows += [q * eg, k * beta * eg, k]
    stacked = jnp.concatenate(rows, axis=0)
    eye = jnp.where(lax.broadcasted_iota(jnp.int32, (GDN_DK, GDN_DK), 0)
                    == lax.broadcasted_iota(jnp.int32, (GDN_DK, GDN_DK), 1), 1.0, 0.0).astype(BF16)
    cols = sum(lax.dot_general(eye, piece, NT_DIMS, preferred_element_type=F32) for piece in _split3(stacked))
    for h, (q, k, u, eg, attn) in enumerate(heads):
        s_old = rec_ref[0, h]
        qe_col, w_col, k_col = (cols[:, 3 * h + i:3 * h + i + 1] for i in range(3))
        v_new = u - jnp.sum(w_col * s_old, axis=0, keepdims=True)
        o = jnp.sum(qe_col * s_old, axis=0, keepdims=True) + attn * v_new
        rec_out_ref[0, h] = s_old * eg + k_col * v_new
        zc = z[:, h * GDN_DV:(h + 1) * GDN_DV]
        o_ref[0, :, h * GDN_DV:(h + 1) * GDN_DV] = _head_norm(o, gn_ref[...]) * (zc * jax.nn.sigmoid(zc))


def gdn_sample(p, conv_buf, rec, lp):
    db = p.shape[0]
    tok = lambda w, cb: pl.BlockSpec((1, 1, w), lambda b: (b, 0, cb * LANE // w))
    vec = pl.BlockSpec((1, LANE), lambda b: (0, 0))
    buf = pl.BlockSpec((1, GDN_CONV - 1, GDN_CONV_DIM), lambda b: (b, 0, 0))
    st = pl.BlockSpec((1, GDN_HEADS, GDN_DK, GDN_DV), lambda b: (b, 0, 0, 0))
    assert CB_QKV == 0 and (CB_Z * LANE) % GDN_V == 0
    return pl.pallas_call(
        _gdn_sample_body,
        grid=(db,),
        in_specs=[tok(GDN_CONV_DIM, CB_QKV), buf, pl.BlockSpec((GDN_CONV, GDN_CONV_DIM), lambda b: (0, 0)),
                  tok(LANE, CB_SMALL), tok(GDN_V, CB_Z), st, vec, vec, vec],
        out_specs=[pl.BlockSpec((1, 1, GDN_V), lambda b: (b, 0, 0)), buf, st],
        out_shape=[jax.ShapeDtypeStruct((db, 1, GDN_V), F32),
                   jax.ShapeDtypeStruct((db, GDN_CONV - 1, GDN_CONV_DIM), F32),
                   jax.ShapeDtypeStruct((db, GDN_HEADS, GDN_DK, GDN_DV), F32)],
        compiler_params=pltpu.CompilerParams(dimension_semantics=("arbitrary",),
                                             vmem_limit_bytes=VMEM_LIMIT_BYTES),
        name="gdn_sample",
    )(p, conv_buf, lp['gdn_conv_w'], p, p, rec,
      _lane_row(lp['gdn_a_log'], SMALL_A), _lane_row(lp['gdn_dt_bias'], SMALL_A),
      lp['gdn_out_norm'].reshape(1, GDN_DV))


def _merge_body(x_ref, oa_ref, oc_ref, os_ref, ow_ref, ga_ref, gb_ref, sm_ref, wo_ref, y_ref):
    gates = jax.nn.sigmoid(sm_ref[...])
    o_b = jnp.zeros(x_ref.shape, F32)
    for br, ref in enumerate((oc_ref, os_ref, ow_ref)):
        o = ref[...]
        o_b = o_b + jnp.concatenate(
            [gates[:, SMALL_G + br * NSA_HEADS + h:SMALL_G + br * NSA_HEADS + h + 1] * o[:, h * NSA_HD:(h + 1) * NSA_HD]
             for h in range(NSA_HEADS)], axis=1)
    m = jax.nn.sigmoid(ga_ref[...]) * oa_ref[...] + jax.nn.sigmoid(gb_ref[...]) * o_b
    y_ref[...] = x_ref[...] + jnp.dot(m.astype(BF16), wo_ref[...], preferred_element_type=F32)


def merge_branches_fused(x2, p2, o_a, o_c, o_s, o_w, w_o):
    n = x2.shape[0]
    tm = min(n, 512)
    row = pl.BlockSpec((tm, D_MODEL), lambda i: (i, 0))
    pcol = lambda w, cb: pl.BlockSpec((tm, w), lambda i: (i, cb * LANE // w))
    return pl.pallas_call(
        _merge_body,
        grid=(n // tm,),
        in_specs=[row, row, row, row, row, pcol(D_MODEL, CB_GA), pcol(D_MODEL, CB_GB), pcol(LANE, CB_SMALL),
                  _const_spec((D_MODEL, D_MODEL))],
        out_specs=row,
        out_shape=jax.ShapeDtypeStruct((n, D_MODEL), F32),
        compiler_params=pltpu.CompilerParams(dimension_semantics=("arbitrary",),
                                             vmem_limit_bytes=VMEM_LIMIT_BYTES),
        name="merge_branches",
    )(x2, o_a, o_c, o_s, o_w, p2, p2, p2, w_o.astype(BF16))


IDX_LANES = 128


def _select_sample_body(imp_ref, ovl_ref, idx_ref, *, q_pos, ns):
    imp = imp_ref[...]
    hi = imp.astype(BF16)
    lo = (imp - hi.astype(F32)).astype(BF16)
    score = (jnp.dot(hi, ovl_ref[...], preferred_element_type=F32)
             + jnp.dot(lo, ovl_ref[...], preferred_element_type=F32))
    rows, nsp = score.shape
    j = lax.broadcasted_iota(jnp.int32, (rows, nsp), 1)
    cur = q_pos // SLC_BLK
    valid = (j <= cur) & (j < ns)
    forced = (j == 0) | (j == cur) | (j == cur - 1)
    work = jnp.where(valid, jnp.where(forced, FORCE_SCORE, score), NEG_INF)
    slot = lax.broadcasted_iota(jnp.int32, (rows, IDX_LANES), 1)
    idx = jnp.full((rows, IDX_LANES), -1, jnp.int32)
    for it in range(min(SLC_TOPN, ns)):
        mx = jnp.max(work, axis=-1, keepdims=True)
        first = jnp.min(jnp.where(work == mx, j, nsp), axis=-1, keepdims=True)
        idx = jnp.where((slot == it) & (mx > NEG_INF), first, idx)
        work = jnp.where(j == first, NEG_INF, work)
    idx_ref[...] = idx


def select_blocks_sample(imp, nc, q_pos, n_keys):
    db, kv, ncp = imp.shape
    ns = -(-n_keys // SLC_BLK)
    nsp = -(-ns // 128) * 128
    cstart = np.arange(ncp) * CMP_STRIDE
    sstart = np.arange(nsp) * SLC_BLK
    ovl = ((cstart[:, None] < sstart[None, :] + SLC_BLK) & (cstart[:, None] + CMP_LEN > sstart[None, :])
           & (np.arange(ncp)[:, None] < nc) & (np.arange(nsp)[None, :] < ns))
    rows = db * kv
    idx = pl.pallas_call(
        functools.partial(_select_sample_body, q_pos=q_pos, ns=ns),
        grid=(1,),
        in_specs=[pl.BlockSpec((rows, ncp), lambda i: (0, 0)), pl.BlockSpec((ncp, nsp), lambda i: (0, 0))],
        out_specs=pl.BlockSpec((rows, IDX_LANES), lambda i: (0, 0)),
        out_shape=jax.ShapeDtypeStruct((rows, IDX_LANES), jnp.int32),
        compiler_params=pltpu.CompilerParams(vmem_limit_bytes=VMEM_LIMIT_BYTES),
        name="select_blocks_sample",
    )(imp.reshape(rows, ncp), jnp.asarray(ovl, dtype=BF16))
    return idx.reshape(db, kv, IDX_LANES)


def _token_softmax_attend(q8, s_past, allowed, v_past, k_new, v_new, new_ok):
    s_new = jnp.sum(q8.astype(F32) * k_new, axis=-1, keepdims=True)
    sm = jnp.where(allowed, s_past, NEG_INF)
    m = jnp.maximum(jnp.max(sm, axis=-1, keepdims=True), jnp.where(new_ok, s_new, NEG_INF))
    m = jnp.where(m > NEG_INF, m, 0.0)
    p = jnp.exp(sm - m)
    p_new = jnp.where(new_ok, jnp.exp(s_new - m), 0.0)
    l = jnp.sum(p, axis=-1, keepdims=True) + p_new
    o = jnp.dot(p.astype(BF16), v_past, preferred_element_type=F32) + p_new * v_new
    return o / jnp.maximum(l, 1e-30)


def _slc_sample_body(idx_ref, pt_ref, *refs, n_past):
    n = SLC_TOPN
    k_blocks, v_blocks = refs[:n], refs[n:2 * n]
    q_ref, kn_ref, vn_ref, o_ref = refs[2 * n:]
    b, g = pl.program_id(0), pl.program_id(1)
    q8 = _token_group_queries_half(q_ref[0])

    def group_rows(blocks):
        sel_g = jnp.full((SLC_TOPN * SLC_BLK, NSA_HD), g, jnp.int32)
        per_group = [jnp.concatenate([r[0, pl.ds(gg, SLC_BLK, stride=NSA_KV_HEADS), :] for r in blocks], axis=0)
                     for gg in range(NSA_KV_HEADS)]
        out = per_group[0]
        for gg in range(1, NSA_KV_HEADS):
            out = jnp.where(sel_g == gg, per_group[gg], out)
        return out.astype(BF16)

    k = group_rows(k_blocks)
    v = group_rows(v_blocks)
    s = lax.dot_general(q8, k, NT_DIMS, preferred_element_type=F32)
    slot = lax.broadcasted_iota(jnp.int32, s.shape, 1) // SLC_BLK
    slot_ok = jnp.zeros(s.shape, jnp.int32)
    n_new = jnp.int32(0)
    for i in range(n):
        ji = idx_ref[(b * NSA_KV_HEADS + g) * n + i]
        slot_ok = jnp.where(slot == i, jnp.where((ji >= 0) & (ji < n_past), 1, 0), slot_ok)
        n_new = n_new + jnp.where(ji >= n_past, 1, 0)
    new_ok = jnp.full((Q_ROWS, 1), n_new, jnp.int32) > 0
    o = _token_softmax_attend(q8, s, slot_ok > 0, v, kn_ref[0], vn_ref[0], new_ok)
    for h in range(NSA_HPG):
        o_ref[0, :, h * NSA_HD:(h + 1) * NSA_HD] = o[h:h + 1]


def _token_group_queries_half(q):
    rows = [q[:, h * NSA_HD:(h + 1) * NSA_HD] for h in range(NSA_HPG)]
    return _pad_rows(jnp.concatenate(rows, axis=0), Q_ROWS)


def slc_attention_sample(qr, idx, pool_k, pool_v, page_table, k_new, v_new):
    db, n_pages = page_table.shape
    n_pool = pool_k.shape[0]
    bpp = PAGE_SIZE // SLC_BLK
    n_past = n_pages * bpp
    n = SLC_TOPN
    gw = NSA_HPG * NSA_HD
    idx_flat = idx[:, :, :n].reshape(-1)
    pt_flat = page_table.reshape(-1)
    view = lambda p: p.reshape(n_pool * bpp, SLC_BLK * NSA_KV_HEADS, NSA_HD)

    def blk_spec(i):
        def index_map(b, g, idx_s, pt_s):
            jc = jnp.clip(idx_s[(b * NSA_KV_HEADS + g) * n + i], 0, n_past - 1)
            return (pt_s[b * n_pages + jc // bpp] * bpp + jc % bpp, 0, 0)
        return pl.BlockSpec((1, SLC_BLK * NSA_KV_HEADS, NSA_HD), index_map)

    grid_spec = pltpu.PrefetchScalarGridSpec(
        num_scalar_prefetch=2,
        grid=(db, NSA_KV_HEADS),
        in_specs=([blk_spec(i) for i in range(n)] + [blk_spec(i) for i in range(n)]
                  + [pl.BlockSpec((1, 1, gw), lambda b, g, *_: (b, 0, g)),
                     pl.BlockSpec((1, 1, NSA_HD), lambda b, g, *_: (b, 0, g)),
                     pl.BlockSpec((1, 1, NSA_HD), lambda b, g, *_: (b, 0, g))]),
        out_specs=pl.BlockSpec((1, 1, gw), lambda b, g, *_: (b, 0, g)),
    )
    return pl.pallas_call(
        functools.partial(_slc_sample_body, n_past=n_past),
        grid_spec=grid_spec,
        out_shape=jax.ShapeDtypeStruct((db, 1, NSA_Q), F32),
        compiler_params=pltpu.CompilerParams(dimension_semantics=("arbitrary", "arbitrary"),
                                             vmem_limit_bytes=VMEM_LIMIT_BYTES),
        name="slc_attention_sample",
    )(idx_flat, pt_flat, *([view(pool_k)] * n), *([view(pool_v)] * n), qr, k_new, v_new)


def _win_sample_body(q_ref, bk_ref, bv_ref, kn_ref, vn_ref, o_ref, nk_ref, nv_ref, *, q_pos, buf_start):
    rows = bk_ref.shape[1]
    wb = rows // NSA_KV_HEADS
    kn, vn = kn_ref[0], vn_ref[0]
    q = q_ref[0]
    kpos = buf_start + lax.broadcasted_iota(jnp.int32, (Q_ROWS, wb), 1)
    allowed = (kpos <= q_pos) & (kpos > q_pos - WINDOW)
    new_ok = jnp.ones((Q_ROWS, 1), jnp.bool_)
    for g in range(NSA_KV_HEADS):
        gcols = slice(g * NSA_HD, (g + 1) * NSA_HD)
        q8 = _token_group_queries(q, g)
        kg = bk_ref[0, pl.ds(g, wb, stride=NSA_KV_HEADS), :].astype(BF16)
        vg = bv_ref[0, pl.ds(g, wb, stride=NSA_KV_HEADS), :].astype(BF16)
        s = lax.dot_general(q8, kg, NT_DIMS, preferred_element_type=F32)
        o = _token_softmax_attend(q8, s, allowed, vg, kn[:, gcols], vn[:, gcols], new_ok)
        for h in range(NSA_HPG):
            c0 = (g * NSA_HPG + h) * NSA_HD
            o_ref[0, :, c0:c0 + NSA_HD] = o[h:h + 1]
    row = lax.broadcasted_iota(jnp.int32, (rows, NSA_HD), 0)

    def shifted(buf_ref, new):
        out = pltpu.roll(buf_ref[0], rows - NSA_KV_HEADS, 0)
        for g in range(NSA_KV_HEADS):
            out = jnp.where(row == rows - NSA_KV_HEADS + g, new[:, g * NSA_HD:(g + 1) * NSA_HD], out)
        return out

    nk_ref[0] = shifted(bk_ref, kn)
    nv_ref[0] = shifted(bv_ref, vn)


def win_attention_sample(qr, buf_k, buf_v, k_new, v_new, q_pos):
    db, rows, _ = buf_k.shape
    wb = rows // NSA_KV_HEADS
    tok = lambda w: pl.BlockSpec((1, 1, w), lambda b: (b, 0, 0))
    buf = pl.BlockSpec((1, rows, NSA_HD), lambda b: (b, 0, 0))
    return pl.pallas_call(
        functools.partial(_win_sample_body, q_pos=q_pos, buf_start=q_pos - wb),
        grid=(db,),
        in_specs=[tok(NSA_Q), buf, buf, tok(NSA_KV), tok(NSA_KV)],
        out_specs=[tok(NSA_Q), buf, buf],
        out_shape=[jax.ShapeDtypeStruct((db, 1, NSA_Q), F32), jax.ShapeDtypeStruct((db, rows, NSA_HD), F32),
                   jax.ShapeDtypeStruct((db, rows, NSA_HD), F32)],
        compiler_params=pltpu.CompilerParams(dimension_semantics=("arbitrary",),
                                             vmem_limit_bytes=VMEM_LIMIT_BYTES),
        name="win_attention_sample",
    )(qr, buf_k, buf_v, k_new, v_new)


def rms_norm(x, g):
    xf = x.astype(jnp.float32)
    y = xf * lax.rsqrt(jnp.mean(xf * xf, axis=-1, keepdims=True) + EPS)
    return (y * g.astype(jnp.float32)).astype(x.dtype)


def l2_norm(x):
    return x * lax.rsqrt(jnp.sum(x * x, axis=-1, keepdims=True) + EPS)


def rope_partial(x, pos):
    half = ROPE_DIM // 2
    inv = jnp.float32(ROPE_THETA) ** (-jnp.arange(half, dtype=jnp.float32) / half)
    ang = pos.astype(jnp.float32)[:, None] * inv
    shape = (pos.shape[0],) + (1,) * (x.ndim - 3) + (half,)
    cos = jnp.cos(ang).reshape(shape)
    sin = jnp.sin(ang).reshape(shape)
    xf = x.astype(jnp.float32)
    x1, x2, rest = xf[..., :half], xf[..., half:ROPE_DIM], xf[..., ROPE_DIM:]
    out = jnp.concatenate([x1 * cos - x2 * sin, x2 * cos + x1 * sin, rest], axis=-1)
    return out.astype(x.dtype)


def masked_softmax(s, mask, axis=-1):
    s = jnp.where(mask, s.astype(jnp.float32), -jnp.inf)
    m = jnp.max(s, axis=axis, keepdims=True)
    m = jnp.where(jnp.isfinite(m), m, 0.0)
    p = jnp.exp(s - m)
    return p / jnp.maximum(jnp.sum(p, axis=axis, keepdims=True), 1e-30)


def split_cols(p, sizes):
    offs = np.cumsum((0,) + tuple(sizes))
    return [p[..., int(offs[i]):int(offs[i + 1])] for i in range(len(sizes))]


def mixer_inputs(x, lp):
    h = rms_norm(x, lp['mix_norm'])
    return split_cols(h @ lp['w_in'], IN_SIZES)


def causal_conv(x, buf, w):
    t = x.shape[1]
    xp = jnp.concatenate([buf.astype(x.dtype), x], axis=1)
    out = xp[:, 0:t] * w[0]
    for j in range(1, GDN_CONV):
        out = out + xp[:, j:j + t] * w[j]
    return jax.nn.silu(out), xp[:, t:]


def gated_delta_rule(q, k, v, g, beta, s0):
    b, t, h, dk = q.shape
    dv = v.shape[-1]
    c = min(GDN_CHUNK, t)
    pad = (-t) % c

    def prep(a):
        a = jnp.pad(a, [(0, 0), (0, pad)] + [(0, 0)] * (a.ndim - 2))
        a = jnp.moveaxis(a, 2, 1)
        return a.reshape((b, h, -1, c) + a.shape[3:])

    q, k, v, g, beta = [prep(a) for a in (q * dk ** -0.5, k, v, g, beta)]
    gc = jnp.cumsum(g, axis=-1)
    ii = jnp.arange(c)
    incl = ii[:, None] >= ii[None, :]
    strict = ii[:, None] > ii[None, :]
    diff = gc[..., :, None] - gc[..., None, :]
    decay = jnp.where(incl, jnp.exp(jnp.where(incl, diff, 0.0)), 0.0)
    kb = k * beta[..., None]
    m = jnp.where(strict, jnp.einsum('bhncd,bhnsd->bhncs', kb, k) * decay, 0.0)
    a_mat = m + jnp.eye(c, dtype=m.dtype)
    rhs = jnp.concatenate([v * beta[..., None], kb * jnp.exp(gc)[..., None]], axis=-1)
    sol = lax.linalg.triangular_solve(a_mat, rhs, left_side=True, lower=True, unit_diagonal=True)
    u, w = sol[..., :dv], sol[..., dv:]
    attn = jnp.where(incl, jnp.einsum('bhncd,bhnsd->bhncs', q, k) * decay, 0.0)

    def step(s, inp):
        q_c, k_c, u_c, w_c, gc_c, attn_c = inp
        v_new = u_c - jnp.einsum('bhcd,bhde->bhce', w_c, s)
        o = (jnp.einsum('bhcd,bhde->bhce', q_c * jnp.exp(gc_c)[..., None], s)
             + jnp.einsum('bhcs,bhse->bhce', attn_c, v_new))
        g_last = gc_c[..., -1]
        k_dec = k_c * jnp.exp(g_last[..., None] - gc_c)[..., None]
        s = s * jnp.exp(g_last)[..., None, None] + jnp.einsum('bhcd,bhce->bhde', k_dec, v_new)
        return s, o

    xs = tuple(jnp.moveaxis(a, 2, 0) for a in (q, k, u, w, gc, attn))
    s_fin, o = lax.scan(step, s0, xs)
    o = jnp.moveaxis(o, 0, 2).reshape(b, h, -1, dv)[:, :, :t]
    return jnp.moveaxis(o, 1, 2), s_fin


def gdn_mixer(qkv, a, b, z, conv_buf, rec, lp):
    bsz, t, _ = qkv.shape
    qkv_c, conv_new = causal_conv(qkv, conv_buf, lp['gdn_conv_w'])
    qkv_c = qkv_c.astype(jnp.float32)
    q = l2_norm(qkv_c[..., :GDN_QK].reshape(bsz, t, GDN_HEADS, GDN_DK))
    k = l2_norm(qkv_c[..., GDN_QK:2 * GDN_QK].reshape(bsz, t, GDN_HEADS, GDN_DK))
    v = qkv_c[..., 2 * GDN_QK:].reshape(bsz, t, GDN_HEADS, GDN_DV)
    g = -jnp.exp(lp['gdn_a_log'].astype(jnp.float32)) * jax.nn.softplus(
        a.astype(jnp.float32) + lp['gdn_dt_bias'].astype(jnp.float32))
    beta = jax.nn.sigmoid(b.astype(jnp.float32))
    o, rec_new = gated_delta_rule(q, k, v, g, beta, rec.astype(jnp.float32))
    zh = z.astype(jnp.float32).reshape(bsz, t, GDN_HEADS, GDN_DV)
    o = rms_norm(o, lp['gdn_out_norm']) * jax.nn.silu(zh)
    return o.reshape(bsz, t, GDN_V).astype(qkv.dtype), conv_new, rec_new.astype(rec.dtype)


def nsa_project(q, kc, vc, ks, vs, kw, vw, pos, lp):
    b, t, _ = q.shape

    def heads(a, n):
        return a.reshape(b, t, n, NSA_HD)

    qn = rms_norm(heads(q, NSA_HEADS), lp['nsa_q_norm']).reshape(b, t, NSA_KV_HEADS, NSA_HPG, NSA_HD)
    qr = rope_partial(qn, pos)
    ks_r = rope_partial(rms_norm(heads(ks, NSA_KV_HEADS), lp['nsa_k_norm_slc']), pos)
    kw_r = rope_partial(rms_norm(heads(kw, NSA_KV_HEADS), lp['nsa_k_norm_win']), pos)
    return (qn, qr, heads(kc, NSA_KV_HEADS), heads(vc, NSA_KV_HEADS), ks_r,
            heads(vs, NSA_KV_HEADS), kw_r, heads(vw, NSA_KV_HEADS))


def subblock_proj(rows, w1):
    b, l, g, hd = rows.shape
    nh = l // CMP_STRIDE
    sub = rows[:, :nh * CMP_STRIDE].reshape(b, nh, CMP_STRIDE, g, hd)
    w = w1.reshape(CMP_LEN // CMP_STRIDE, CMP_STRIDE, hd, hd)
    return jnp.einsum('bnsgd,rsde->rbnge', sub, w)


def compress(parts, w1, w2, pe):
    r_sub = parts.shape[0]
    nc = parts.shape[2] - r_sub + 1
    hid = parts[0, :, 0:nc]
    for r in range(1, r_sub):
        hid = hid + parts[r, :, r:r + nc]
    hid = hid + jnp.einsum('ld,lde->e', pe, w1)
    return jax.nn.silu(hid) @ w2


def compressed_kv(parts_k, parts_v, lp):
    kcc = rms_norm(compress(parts_k, lp['cmp_w1_k'], lp['cmp_w2_k'], lp['cmp_pe_k']), lp['nsa_k_norm_cmp'])
    vcc = compress(parts_v, lp['cmp_w1_v'], lp['cmp_w2_v'], lp['cmp_pe_v'])
    return kcc, vcc


def cmp_attention(q, q_pos, kc, vc):
    nc = kc.shape[1]
    end = jnp.arange(nc) * CMP_STRIDE + CMP_LEN - 1
    vis = end[None, :] <= q_pos[:, None]
    s = jnp.einsum('btghd,bcgd->btghc', q, kc) * ATTN_SCALE
    p = masked_softmax(s, vis[None, :, None, None, :], axis=-1)
    o = jnp.einsum('btghc,bcgd->btghd', p, vc)
    return o, jnp.sum(p, axis=3)


def select_blocks(imp, q_pos, n_keys):
    nc = imp.shape[-1]
    ns = -(-n_keys // SLC_BLK)
    cstart = jnp.arange(nc) * CMP_STRIDE
    sstart = jnp.arange(ns) * SLC_BLK
    overlap = (cstart[:, None] < sstart[None, :] + SLC_BLK) & (cstart[:, None] + CMP_LEN > sstart[None, :])
    score = imp @ overlap.astype(imp.dtype)
    cur = q_pos // SLC_BLK
    j = jnp.arange(ns)[None, :]
    valid = j <= cur[:, None]
    forced = (j == 0) | (j == cur[:, None]) | (j == cur[:, None] - 1)
    score = jnp.where(valid[None, :, None, :],
                      jnp.where(forced[None, :, None, :], FORCE_SCORE, score), -jnp.inf)
    vals, idx = lax.top_k(score, min(SLC_TOPN, ns))
    return idx, jnp.isfinite(vals)


def slc_attention(q, q_pos, idx, ok, kg, vg):
    s = jnp.einsum('btghd,btgnsd->btghns', q, kg) * ATTN_SCALE
    kpos = idx[..., None] * SLC_BLK + jnp.arange(SLC_BLK)
    mask = ok[..., None] & (kpos <= q_pos[None, :, None, None, None])
    p = masked_softmax(s, mask[:, :, :, None], axis=(-2, -1))
    return jnp.einsum('btghns,btgnsd->btghd', p, vg)


def slc_prompt(q, q_pos, idx, ok, ks, vs):
    b, s, g, hd = ks.shape
    ns = s // SLC_BLK
    kb = ks.reshape(b, ns, SLC_BLK, g, hd).transpose(0, 3, 1, 2, 4)
    vb = vs.reshape(b, ns, SLC_BLK, g, hd).transpose(0, 3, 1, 2, 4)
    bi = jnp.arange(b)[:, None, None, None]
    gi = jnp.arange(g)[None, None, :, None]
    nqb = s // Q_BLOCK

    def to_blocks(a):
        return jnp.moveaxis(a.reshape((a.shape[0], nqb, Q_BLOCK) + a.shape[2:]), 1, 0)

    def one_block(args):
        qb, pb, ib, ob = args
        return slc_attention(qb, pb, ib, ob, kb[bi, gi, ib], vb[bi, gi, ib])

    out = lax.map(one_block, (to_blocks(q), q_pos.reshape(nqb, Q_BLOCK), to_blocks(idx), to_blocks(ok)))
    return jnp.moveaxis(out, 0, 1).reshape(q.shape)


def slc_sample(q, q_pos, idx, ok, pool_k, pool_v, layer, page_table, k_new, v_new):
    db, t, g, hd = k_new.shape
    n_past = PAST_LEN // SLC_BLK
    bpp = PAGE_SIZE // SLC_BLK
    bi = jnp.arange(db)[:, None, None, None]
    gi = jnp.arange(g)[None, None, :, None]
    jc = jnp.minimum(idx, n_past - 1)
    page = page_table[bi, jc // bpp]
    row = (jc % bpp)[..., None] * SLC_BLK + jnp.arange(SLC_BLK)
    nnb = -(-t // SLC_BLK)
    jn = jnp.clip(idx - n_past, 0, nnb - 1)
    is_new = (idx >= n_past)[..., None, None]

    def gather(pool, new):
        from_pool = pool[layer, page[..., None], row, gi[..., None]]
        newb = jnp.pad(new, ((0, 0), (0, nnb * SLC_BLK - t), (0, 0), (0, 0)))
        newb = newb.reshape(db, nnb, SLC_BLK, g, hd).transpose(0, 3, 1, 2, 4)
        return jnp.where(is_new, newb[bi, gi, jn], from_pool)

    return slc_attention(q, q_pos, idx, ok, gather(pool_k, k_new), gather(pool_v, v_new))


def win_prompt(q, kw, vw):
    b, s, g, hpg, hd = q.shape
    nqb = s // Q_BLOCK
    nw = WINDOW // Q_BLOCK
    pad = ((0, 0), (WINDOW, 0), (0, 0), (0, 0))
    kb = jnp.pad(kw, pad).reshape(b, nw + nqb, Q_BLOCK, g, hd)
    vb = jnp.pad(vw, pad).reshape(b, nw + nqb, Q_BLOCK, g, hd)
    kband = jnp.concatenate([kb[:, j:j + nqb] for j in range(nw + 1)], axis=2)
    vband = jnp.concatenate([vb[:, j:j + nqb] for j in range(nw + 1)], axis=2)
    qb = q.reshape(b, nqb, Q_BLOCK, g, hpg, hd)
    sc = jnp.einsum('bnqghd,bnkgd->bnqghk', qb, kband) * ATTN_SCALE
    start = jnp.arange(nqb)[:, None] * Q_BLOCK
    qpos = start + jnp.arange(Q_BLOCK)
    kpos = start - WINDOW + jnp.arange((nw + 1) * Q_BLOCK)
    kp, qp = kpos[:, None, :], qpos[:, :, None]
    mask = (kp <= qp) & (kp > qp - WINDOW) & (kp >= 0)
    p = masked_softmax(sc, mask[None, :, :, None, None, :], axis=-1)
    o = jnp.einsum('bnqghk,bnkgd->bnqghd', p, vband)
    return o.reshape(b, s, g, hpg, hd)


def win_sample(q, q_pos, buf_k, buf_v, k_new, v_new):
    wb = buf_k.shape[1]
    t = k_new.shape[1]
    ka = jnp.concatenate([buf_k.astype(k_new.dtype), k_new], axis=1)
    va = jnp.concatenate([buf_v.astype(v_new.dtype), v_new], axis=1)
    kpos = PAST_LEN - wb + jnp.arange(wb + t)
    mask = (kpos[None, :] <= q_pos[:, None]) & (kpos[None, :] > q_pos[:, None] - WINDOW)
    sc = jnp.einsum('btghd,bkgd->btghk', q, ka) * ATTN_SCALE
    p = masked_softmax(sc, mask[None, :, None, None, :], axis=-1)
    o = jnp.einsum('btghk,bkgd->btghd', p, va)
    return o, ka[:, t:], va[:, t:]


def nsa_combine(gates, o_c, o_s, o_w):
    b, t, _ = gates.shape
    gt = jax.nn.sigmoid(gates.astype(jnp.float32)).reshape(b, t, 3, NSA_KV_HEADS, NSA_HPG, 1)
    o = gt[:, :, 0] * o_c + gt[:, :, 1] * o_s + gt[:, :, 2] * o_w
    return o.reshape(b, t, NSA_Q)


def merge_branches(x, o_a, o_b, ga, gb, w_o):
    m = (jax.nn.sigmoid(ga.astype(jnp.float32)) * o_a.astype(jnp.float32)
         + jax.nn.sigmoid(gb.astype(jnp.float32)) * o_b.astype(jnp.float32))
    return (x + m.astype(x.dtype) @ w_o).astype(x.dtype)


def gather_pages(pool, layer, page_table):
    rows = pool[layer, page_table]
    return rows.reshape(rows.shape[0], -1, rows.shape[3], rows.shape[4])


def _kv_cols(p, i):
    c0 = CB_KV * LANE + i * NSA_KV
    return p[..., c0:c0 + NSA_KV]


def token_mix_prompt(x, lp):
    b, s, _ = x.shape
    pos = jnp.arange(s)
    x2 = x.reshape(b * s, D_MODEL)
    p2 = mixer_projection(x2, lp['mix_norm'], lp['w_in'])
    p = p2.reshape(b, s, P_COLS)
    o_a, rec_new = gdn_prompt(p, lp)
    assert s >= GDN_CONV - 1
    conv_new = p[:, s - (GDN_CONV - 1):, :GDN_CONV_DIM]
    heads = lambda a_: a_.reshape(b, -1, NSA_KV_HEADS, NSA_HD)
    qn, qr, ks_r, kw_r = nsa_prep(p, pos, lp)
    kcc, vcc, nc = compressed_kv_prompt(p, lp)
    o_c, sel = cmp_select_prompt(qn, kcc, vcc, nc)
    vs, vw = _kv_cols(p, 3), _kv_cols(p, 5)
    o_s = slc_attention_prompt(qr, sel, ks_r.astype(BF16), vs.astype(BF16))
    o_w = win_attention_prompt(qr, kw_r.astype(BF16), vw.astype(BF16))
    flat = lambda a_: a_.reshape(b * s, -1)
    y = merge_branches_fused(x2, p2, flat(o_a), flat(o_c), flat(o_s), flat(o_w), lp['w_o']).reshape(b, s, D_MODEL)
    wb = min(WINDOW, s)
    return y, (conv_new, rec_new, heads(_kv_cols(p, 0)), heads(_kv_cols(p, 1)), heads(ks_r), heads(vs),
               heads(kw_r[:, s - wb:]), heads(vw[:, s - wb:]))


def token_mix_sample(x, layer, conv_buf, rec, cache_k_cmp, cache_v_cmp, cache_k_slc, cache_v_slc,
                     win_k, win_v, page_table, lp):
    db, t, _ = x.shape
    assert t == 1 and t < CMP_STRIDE
    q_pos = PAST_LEN
    x2 = x.reshape(db, D_MODEL)
    p2 = mixer_projection(x2, lp['mix_norm'], lp['w_in'])
    p = p2.reshape(db, 1, P_COLS)
    o_a, conv_new, rec_new = gdn_sample(p, conv_buf, rec, lp)
    heads = lambda a_: a_.reshape(db, -1, NSA_KV_HEADS, NSA_HD)
    qn, qr, ks_r, kw_r = [a_.reshape(db, 1, -1) for a_ in
                          nsa_prep(p2.reshape(1, db, P_COLS), jnp.full((db,), q_pos), lp)]
    (o_c, imp), nc = cmp_attention_sample(qn, cache_k_cmp[layer], cache_v_cmp[layer], page_table, lp, q_pos)
    idx = select_blocks_sample(imp, nc, q_pos, PAST_LEN + t)
    vs, vw = _kv_cols(p, 3), _kv_cols(p, 5)
    o_s = slc_attention_sample(qr, idx, cache_k_slc[layer], cache_v_slc[layer], page_table, ks_r, vs)
    wb = win_k.shape[1]
    o_w, win_k_new, win_v_new = win_attention_sample(
        qr, win_k.reshape(db, wb * NSA_KV_HEADS, NSA_HD), win_v.reshape(db, wb * NSA_KV_HEADS, NSA_HD),
        kw_r, vw, q_pos)
    flat = lambda a_: a_.reshape(db, -1)
    y = merge_branches_fused(x2, p2, flat(o_a), flat(o_c), flat(o_s), flat(o_w), lp['w_o']).reshape(db, 1, D_MODEL)
    return y, (conv_new, rec_new, heads(_kv_cols(p, 0)), heads(_kv_cols(p, 1)), heads(ks_r), heads(vs),
               heads(win_k_new), heads(win_v_new))


def kernel(x_prompt, x_sample, state_gdn_conv, state_gdn_rec, cache_k_cmp, cache_v_cmp, cache_k_slc, cache_v_slc, cache_k_win, cache_v_win, page_table, ffn1_norm, ffn1_w_in, ffn1_w_out, mix_norm, w_in, gdn_conv_w, gdn_a_log, gdn_dt_bias, gdn_out_norm, nsa_q_norm, nsa_k_norm_cmp, nsa_k_norm_slc, nsa_k_norm_win, cmp_w1_k, cmp_w2_k, cmp_pe_k, cmp_w1_v, cmp_w2_v, cmp_pe_v, w_o, ffn2_norm, ffn2_w_in, ffn2_w_out):
    depth = w_in.shape[0]
    yp, ys = x_prompt, x_sample
    p_list, s_list = [], []
    for l in range(depth):
        lp = {'mix_norm': mix_norm[l], 'w_in': w_in[l], 'gdn_conv_w': gdn_conv_w[l],
              'gdn_a_log': gdn_a_log[l], 'gdn_dt_bias': gdn_dt_bias[l], 'gdn_out_norm': gdn_out_norm[l],
              'nsa_q_norm': nsa_q_norm[l], 'nsa_k_norm_cmp': nsa_k_norm_cmp[l],
              'nsa_k_norm_slc': nsa_k_norm_slc[l], 'nsa_k_norm_win': nsa_k_norm_win[l],
              'cmp_w1_k': cmp_w1_k[l], 'cmp_w2_k': cmp_w2_k[l], 'cmp_pe_k': cmp_pe_k[l],
              'cmp_w1_v': cmp_w1_v[l], 'cmp_w2_v': cmp_w2_v[l], 'cmp_pe_v': cmp_pe_v[l], 'w_o': w_o[l]}
        yp = swiglu_half(yp, ffn1_norm[l], ffn1_w_in[l], ffn1_w_out[l])
        yp, st_p = token_mix_prompt(yp, lp)
        yp = swiglu_half(yp, ffn2_norm[l], ffn2_w_in[l], ffn2_w_out[l])
        ys = swiglu_half(ys, ffn1_norm[l], ffn1_w_in[l], ffn1_w_out[l])
        ys, st_s = token_mix_sample(ys, l, state_gdn_conv[l], state_gdn_rec[l], cache_k_cmp, cache_v_cmp,
                                    cache_k_slc, cache_v_slc, cache_k_win[l], cache_v_win[l], page_table, lp)
        ys = swiglu_half(ys, ffn2_norm[l], ffn2_w_in[l], ffn2_w_out[l])
        p_list.append(st_p)
        s_list.append(st_s)
    (p_conv, p_rec, p_kc, p_vc, p_ks, p_vs, p_kw, p_vw) = [jnp.stack(a) for a in zip(*p_list)]
    (s_conv, s_rec, s_kc, s_vc, s_ks, s_vs, s_kw, s_vw) = [jnp.stack(a) for a in zip(*s_list)]
    return (yp, ys, p_conv, p_rec, p_kc, p_vc, p_ks, p_vs, p_kw, p_vw,
            s_conv, s_rec, s_kc, s_vc, s_ks, s_vs, s_kw, s_vw)
```

```python
import functools
import math

import numpy as np
import jax
import jax.numpy as jnp
from jax import lax
from jax.experimental import pallas as pl
from jax.experimental.pallas import tpu as pltpu

D_MODEL = 1024
PAST_LEN = 8192
PAGE_SIZE = 128

GDN_DK = 128
GDN_DV = 128
GDN_HEADS = D_MODEL // GDN_DV
GDN_QK = GDN_HEADS * GDN_DK
GDN_V = GDN_HEADS * GDN_DV
GDN_CONV_DIM = 2 * GDN_QK + GDN_V
GDN_CONV = 4
GDN_CHUNK = 64

NSA_HD = 128
NSA_HEADS = D_MODEL // NSA_HD
NSA_KV_HEADS = NSA_HEADS // 4
NSA_HPG = NSA_HEADS // NSA_KV_HEADS
NSA_Q = NSA_HEADS * NSA_HD
NSA_KV = NSA_KV_HEADS * NSA_HD
CMP_LEN = 32
CMP_STRIDE = 16
SLC_BLK = 64
SLC_TOPN = 16
WINDOW = 512
Q_BLOCK = 128
ATTN_SCALE = NSA_HD ** -0.5
FORCE_SCORE = 1e6

ROPE_DIM = NSA_HD // 4
ROPE_THETA = 500000.0

D_FF = ((8 * D_MODEL // 3 + 127) // 128) * 128
EPS = 1e-6

IN_SIZES = (GDN_CONV_DIM, GDN_HEADS, GDN_HEADS, GDN_V,
            NSA_Q, NSA_KV, NSA_KV, NSA_KV, NSA_KV, NSA_KV, NSA_KV, 3 * NSA_HEADS,
            D_MODEL, D_MODEL)

MXU_N = 256
VMEM_LIMIT_BYTES = 56 * 1024 * 1024

BF16 = jnp.bfloat16
F32 = jnp.float32


def _const_spec(shape):
    return pl.BlockSpec(shape, lambda *_: (0,) * len(shape), pipeline_mode=pl.Buffered(1))


def _ffn_body(x_ref, g_ref, wg_ref, wu_ref, wo_ref, o_ref):
    x = x_ref[...]
    ms = jnp.mean(x * x, axis=-1, keepdims=True)
    h = (x * lax.rsqrt(ms + EPS) * g_ref[...]).astype(BF16)
    acc = jnp.zeros(x.shape, F32)
    for c in range(D_FF // MXU_N):
        cols = slice(c * MXU_N, (c + 1) * MXU_N)
        gate = jnp.dot(h, wg_ref[:, cols], preferred_element_type=F32)
        up = jnp.dot(h, wu_ref[:, cols], preferred_element_type=F32)
        act = (gate * jax.nn.sigmoid(gate) * up).astype(BF16)
        acc = acc + jnp.dot(act, wo_ref[cols, :], preferred_element_type=F32)
    o_ref[...] = x + 0.5 * acc


def _ffn_tile(n_rows):
    return min(n_rows, 512)


def swiglu_half(x, g, w_in, w_out):
    shape = x.shape
    x2 = x.reshape(-1, D_MODEL)
    n = x2.shape[0]
    tm = _ffn_tile(n)
    wg = w_in[:, :D_FF].astype(BF16)
    wu = w_in[:, D_FF:].astype(BF16)
    wo = w_out.astype(BF16)
    out = pl.pallas_call(
        _ffn_body,
        grid=(n // tm,),
        in_specs=[pl.BlockSpec((tm, D_MODEL), lambda i: (i, 0)),
                  _const_spec((1, D_MODEL)),
                  _const_spec((D_MODEL, D_FF)),
                  _const_spec((D_MODEL, D_FF)),
                  _const_spec((D_FF, D_MODEL))],
        out_specs=pl.BlockSpec((tm, D_MODEL), lambda i: (i, 0)),
        out_shape=jax.ShapeDtypeStruct((n, D_MODEL), F32),
        compiler_params=pltpu.CompilerParams(dimension_semantics=("arbitrary",),
                                             vmem_limit_bytes=VMEM_LIMIT_BYTES),
        name="swiglu_half",
    )(x2, g.reshape(1, D_MODEL), wg, wu, wo)
    return out.reshape(shape)


LANE = 128
CB_QKV = 0
CB_Z = CB_QKV + GDN_CONV_DIM // LANE
CB_Q = CB_Z + GDN_V // LANE
CB_GA = CB_Q + NSA_Q // LANE
CB_GB = CB_GA + D_MODEL // LANE
CB_KV = CB_GB + D_MODEL // LANE
CB_SMALL = CB_KV + 6 * NSA_KV // LANE
P_COLS = (CB_SMALL + 1) * LANE
SMALL_A, SMALL_B, SMALL_G = 0, GDN_HEADS, 2 * GDN_HEADS
PROJ_CHUNKS = 3
assert (P_COLS // LANE) % PROJ_CHUNKS == 0 and 2 * GDN_HEADS + 3 * NSA_HEADS <= LANE


def _permuted_w_in(w_in):
    offs = np.cumsum((0,) + IN_SIZES)
    seg = lambda i: np.arange(offs[i], offs[i + 1])
    order = np.concatenate([seg(0), seg(3), seg(4), seg(12), seg(13)] + [seg(i) for i in range(5, 11)]
                           + [seg(1), seg(2), seg(11)])
    w = w_in[:, order]
    return jnp.pad(w, ((0, 0), (0, P_COLS - w.shape[1]))).astype(BF16)


def _proj_body(x_ref, g_ref, w_ref, o_ref):
    x = x_ref[...]
    ms = jnp.mean(x * x, axis=-1, keepdims=True)
    h = (x * lax.rsqrt(ms + EPS) * g_ref[...]).astype(BF16)
    o_ref[...] = jnp.dot(h, w_ref[...], preferred_element_type=F32)


def mixer_projection(x2, g, w_in):
    n = x2.shape[0]
    tm = min(n, 512)
    wc = P_COLS // PROJ_CHUNKS
    return pl.pallas_call(
        _proj_body,
        grid=(PROJ_CHUNKS, n // tm),
        in_specs=[pl.BlockSpec((tm, D_MODEL), lambda c, i: (i, 0)),
                  pl.BlockSpec((1, D_MODEL), lambda c, i: (0, 0)),
                  pl.BlockSpec((D_MODEL, wc), lambda c, i: (0, c))],
        out_specs=pl.BlockSpec((tm, wc), lambda c, i: (i, c)),
        out_shape=jax.ShapeDtypeStruct((n, P_COLS), F32),
        compiler_params=pltpu.CompilerParams(dimension_semantics=("arbitrary", "arbitrary"),
                                             vmem_limit_bytes=VMEM_LIMIT_BYTES),
        name="mixer_projection",
    )(x2, g.reshape(1, D_MODEL), _permuted_w_in(w_in))


GDN_TILE = 128
GDN_HB = 4
TN_DIMS = (((0,), (0,)), ((), ()))


def _split3(x):
    p1 = x.astype(BF16)
    r1 = x - p1.astype(F32)
    p2 = r1.astype(BF16)
    p3 = (r1 - p2.astype(F32)).astype(BF16)
    return p1, p2, p3


def _gdn_gates(small, alog_row, dt_row):
    xa = small + dt_row
    softplus = jnp.maximum(xa, 0.0) + jnp.log(1.0 + jnp.exp(-jnp.abs(xa)))
    return -jnp.exp(alog_row) * softplus, jax.nn.sigmoid(small)


def _lane_col(x, lane_idx):
    lane = lax.broadcasted_iota(jnp.int32, x.shape, 1)
    return jnp.sum(jnp.where(lane == lane_idx, x, 0.0), axis=1, keepdims=True)


def _gdn_prompt_body(q_ref, k_ref, v_ref, z_ref, sm_ref, wq_ref, wk_ref, wv_ref, alog_ref, dt_ref, gn_ref,
                     o_ref, rec_ref, s_ref, pq_ref, pk_ref, pv_ref, *, tile, hb):
    c = GDN_CHUNK
    hblk, step = pl.program_id(1), pl.program_id(2)
    width = hb * GDN_DK

    @pl.when(step == 0)
    def _():
        s_ref[...] = jnp.zeros(s_ref.shape, F32)
        for r in (pq_ref, pk_ref, pv_ref):
            r[...] = jnp.zeros(r.shape, F32)

    row8 = lax.broadcasted_iota(jnp.int32, (8, width), 0)

    def conv(x_ref, prev_ref, w_ref):
        x, prev, w = x_ref[0], prev_ref[...], w_ref[...]
        acc = x * w[GDN_CONV - 1:GDN_CONV]
        for j in range(1, GDN_CONV):
            main = pltpu.roll(x, j, 0)
            first = jnp.where(row8 < j, pltpu.roll(prev, j, 0), main[:8])
            acc = acc + jnp.concatenate([first, main[8:]], axis=0) * w[GDN_CONV - 1 - j:GDN_CONV - j]
        prev_ref[...] = x[tile - 8:]
        return acc * jax.nn.sigmoid(acc)

    qc, kc, vc = conv(q_ref, pq_ref, wq_ref), conv(k_ref, pk_ref, wk_ref), conv(v_ref, pv_ref, wv_ref)
    z = z_ref[0]
    g_all, beta_all = _gdn_gates(sm_ref[0], alog_ref[...], dt_ref[...])

    n = hb * c
    ii = lax.broadcasted_iota(jnp.int32, (n, n), 0)
    jj = lax.broadcasted_iota(jnp.int32, (n, n), 1)
    same_head = (ii // c) == (jj // c)
    incl, strict = same_head & (ii >= jj), same_head & (ii > jj)
    ci_ = lax.broadcasted_iota(jnp.int32, (c, c), 0)
    cj_ = lax.broadcasted_iota(jnp.int32, (c, c), 1)
    tri = jnp.where(ci_ >= cj_, 1.0, 0.0).astype(BF16)
    er = lax.broadcasted_iota(jnp.int32, (8, LANE), 0)
    el = lax.broadcasted_iota(jnp.int32, (8, LANE), 1)
    head_rows = jnp.where((el == SMALL_A + hblk * hb + er) & (er < hb), 1.0, 0.0).astype(BF16)
    dot = functools.partial(jnp.dot, preferred_element_type=F32)
    stack = lambda f: jnp.concatenate([f(hh) for hh in range(hb)], axis=0)
    states = [s_ref[hh] for hh in range(hb)]

    for ci in range(tile // c):
        rows = slice(ci * c, (ci + 1) * c)
        gc_all = sum(dot(tri, piece) for piece in _split3(g_all[rows]))
        gc_rows = sum(lax.dot_general(head_rows, piece, NT_DIMS, preferred_element_type=F32)
                      for piece in _split3(gc_all))
        head_cols = lambda x: stack(lambda hh: x[rows, hh * GDN_DK:(hh + 1) * GDN_DK])
        gcol = stack(lambda hh: _lane_col(gc_all, SMALL_A + hblk * hb + hh))
        bcol = stack(lambda hh: _lane_col(beta_all[rows], SMALL_B + hblk * hb + hh))
        grow = jnp.concatenate([gc_rows[hh:hh + 1] for hh in range(hb)], axis=1)
        g_last = stack(lambda hh: jnp.broadcast_to(gcol[hh * c + c - 1:hh * c + c], (c, 1)))
        q, k, v = head_cols(qc), head_cols(kc), head_cols(vc)
        q = q * lax.rsqrt(jnp.sum(q * q, axis=-1, keepdims=True) + EPS) * (GDN_DK ** -0.5)
        k = k * lax.rsqrt(jnp.sum(k * k, axis=-1, keepdims=True) + EPS)
        decay = jnp.where(incl, jnp.exp(jnp.where(incl, gcol - grow, 0.0)), 0.0)
        kb = k * bcol
        k16 = k.astype(BF16)
        m = jnp.where(strict, lax.dot_general(kb.astype(BF16), k16, NT_DIMS, preferred_element_type=F32) * decay, 0.0)
        egc = jnp.exp(gcol)
        rhs = jnp.concatenate([v * bcol, kb * egc], axis=1)
        mp = m
        sol = rhs
        for level in range(int(math.log2(c))):
            a1, a2, _ = _split3(mp)
            s1, s2, _ = _split3(sol)
            step_term = dot(a1, s1) + dot(a1, s2) + dot(a2, s1)
            sol = sol - step_term if level == 0 else sol + step_term
            if level + 1 < int(math.log2(c)):
                mp = dot(a1, a1)
        u, w = sol[:, :GDN_DV], sol[:, GDN_DV:]
        attn = jnp.where(incl, lax.dot_general(q.astype(BF16), k16, NT_DIMS, preferred_element_type=F32) * decay, 0.0)
        qe = q * egc
        k_dec = k * jnp.exp(g_last - gcol)
        s_decay = jnp.exp(g_last)
        v_new, qs = [], []
        for hh in range(hb):
            r = slice(hh * c, (hh + 1) * c)
            s16 = states[hh].astype(BF16)
            ws_qs = dot(jnp.concatenate([w[r], qe[r]], axis=0).astype(BF16), s16)
            v_new.append(u[r] - ws_qs[:c])
            qs.append(ws_qs[c:])
            states[hh] = states[hh] * s_decay[hh * c:hh * c + 1] + lax.dot_general(
                k_dec[r].astype(BF16), v_new[hh].astype(BF16), TN_DIMS, preferred_element_type=F32)
        o = jnp.concatenate(qs, axis=0) + dot(attn.astype(BF16), jnp.concatenate(v_new, axis=0).astype(BF16))
        for hh in range(hb):
            cols = slice(hh * GDN_DK, (hh + 1) * GDN_DK)
            zc = z[rows, cols]
            o_ref[0, rows, cols] = _head_norm(o[hh * c:(hh + 1) * c], gn_ref[...]) * (zc * jax.nn.sigmoid(zc))

    for hh in range(hb):
        s_ref[hh] = states[hh]

    @pl.when(step == pl.num_programs(2) - 1)
    def _():
        rec_ref[0] = s_ref[...]


def _lane_row(vec, offset):
    return jnp.pad(vec.astype(F32), (offset, LANE - offset - vec.shape[0])).reshape(1, LANE)


def gdn_prompt(p, lp):
    b, s, _ = p.shape
    tile, hb = GDN_TILE, GDN_HB
    assert s % tile == 0 and tile % GDN_CHUNK == 0 and GDN_HEADS % hb == 0 and GDN_CHUNK & (GDN_CHUNK - 1) == 0
    width = hb * GDN_DK
    wb = width // LANE
    col = lambda cb0: pl.BlockSpec((1, tile, width), lambda bi, h, i: (bi, i, cb0 // wb + h))
    wcol = lambda cb0: pl.BlockSpec((GDN_CONV, width), lambda bi, h, i: (0, cb0 // wb + h))
    vec = pl.BlockSpec((1, LANE), lambda bi, h, i: (0, 0))
    qk_blocks = GDN_QK // LANE
    return pl.pallas_call(
        functools.partial(_gdn_prompt_body, tile=tile, hb=hb),
        grid=(b, GDN_HEADS // hb, s // tile),
        in_specs=[col(CB_QKV), col(CB_QKV + qk_blocks), col(CB_QKV + 2 * qk_blocks), col(CB_Z),
                  pl.BlockSpec((1, tile, LANE), lambda bi, h, i: (bi, i, CB_SMALL)),
                  wcol(0), wcol(qk_blocks), wcol(2 * qk_blocks), vec, vec, vec],
        out_specs=[pl.BlockSpec((1, tile, width), lambda bi, h, i: (bi, i, h)),
                   pl.BlockSpec((1, hb, GDN_DK, GDN_DV), lambda bi, h, i: (bi, h, 0, 0))],
        out_shape=[jax.ShapeDtypeStruct((b, s, GDN_V), F32),
                   jax.ShapeDtypeStruct((b, GDN_HEADS, GDN_DK, GDN_DV), F32)],
        scratch_shapes=[pltpu.VMEM((hb, GDN_DK, GDN_DV), F32)] + [pltpu.VMEM((8, width), F32)] * 3,
        compiler_params=pltpu.CompilerParams(dimension_semantics=("arbitrary", "arbitrary", "arbitrary"),
                                             vmem_limit_bytes=VMEM_LIMIT_BYTES),
        name="gdn_prompt",
    )(p, p, p, p, p, lp['gdn_conv_w'], lp['gdn_conv_w'], lp['gdn_conv_w'],
      _lane_row(lp['gdn_a_log'], SMALL_A), _lane_row(lp['gdn_dt_bias'], SMALL_A),
      lp['gdn_out_norm'].reshape(1, GDN_DV))


NEG_INF = float("-inf")
NEG_BIG = -1e30
NT_DIMS = (((1,), (1,)), ((), ()))


def _rope_tables(pos):
    half = ROPE_DIM // 2
    inv = jnp.float32(ROPE_THETA) ** (-jnp.arange(half, dtype=jnp.float32) / half)
    ang = pos.astype(jnp.float32)[:, None] * inv
    cos, sin = jnp.cos(ang), jnp.sin(ang)
    rest = NSA_HD - ROPE_DIM
    cos_t = jnp.concatenate([cos, cos, jnp.ones((pos.shape[0], rest), F32)], axis=-1)
    sin_t = jnp.concatenate([-sin, sin, jnp.zeros((pos.shape[0], rest), F32)], axis=-1)
    return cos_t, sin_t


def _rope(y, cos_t, sin_t):
    half = ROPE_DIM // 2
    lane = lax.broadcasted_iota(jnp.int32, y.shape, 1)
    partner = jnp.where(lane < half, pltpu.roll(y, NSA_HD - half, 1), pltpu.roll(y, half, 1))
    return y * cos_t + partner * sin_t


def _head_norm(x, g):
    ms = jnp.mean(x * x, axis=-1, keepdims=True)
    return x * lax.rsqrt(ms + EPS) * g


def _nsa_prep_body(q_ref, ks_ref, kw_ref, cos_ref, sin_ref, gq_ref, gks_ref, gkw_ref,
                   qn_ref, qr_ref, ksr_ref, kwr_ref):
    cos_t, sin_t = cos_ref[...], sin_ref[...]
    for h in range(NSA_HEADS):
        cols = slice(h * NSA_HD, (h + 1) * NSA_HD)
        y = _head_norm(q_ref[0, :, cols], gq_ref[...])
        qn_ref[0, :, cols] = (y * ATTN_SCALE).astype(BF16)
        qr_ref[0, :, cols] = (_rope(y, cos_t, sin_t) * ATTN_SCALE).astype(BF16)
    for h in range(NSA_KV_HEADS):
        cols = slice(h * NSA_HD, (h + 1) * NSA_HD)
        ksr_ref[0, :, cols] = _rope(_head_norm(ks_ref[0, :, cols], gks_ref[...]), cos_t, sin_t)
        kwr_ref[0, :, cols] = _rope(_head_norm(kw_ref[0, :, cols], gkw_ref[...]), cos_t, sin_t)


def _col_spec(rows, width, col_block):
    assert (col_block * LANE) % width == 0
    cb = col_block * LANE // width
    return pl.BlockSpec((1, rows, width), lambda bi, i: (bi, i, cb))


def nsa_prep(p, pos, lp):
    b, t, _ = p.shape
    tp = min(t, 512)
    cos_t, sin_t = _rope_tables(pos)
    row = lambda w: pl.BlockSpec((1, tp, w), lambda bi, i: (bi, i, 0))
    kv_w = NSA_KV // LANE
    tab = pl.BlockSpec((tp, NSA_HD), lambda bi, i: (i, 0))
    gain = pl.BlockSpec((1, NSA_HD), lambda bi, i: (0, 0))
    return pl.pallas_call(
        _nsa_prep_body,
        grid=(b, t // tp),
        in_specs=[_col_spec(tp, NSA_Q, CB_Q), _col_spec(tp, NSA_KV, CB_KV + 2 * kv_w),
                  _col_spec(tp, NSA_KV, CB_KV + 4 * kv_w), tab, tab, gain, gain, gain],
        out_specs=[row(NSA_Q), row(NSA_Q), row(NSA_KV), row(NSA_KV)],
        out_shape=[jax.ShapeDtypeStruct((b, t, NSA_Q), BF16), jax.ShapeDtypeStruct((b, t, NSA_Q), BF16),
                   jax.ShapeDtypeStruct((b, t, NSA_KV), F32), jax.ShapeDtypeStruct((b, t, NSA_KV), F32)],
        compiler_params=pltpu.CompilerParams(dimension_semantics=("arbitrary", "arbitrary"),
                                             vmem_limit_bytes=VMEM_LIMIT_BYTES),
        name="nsa_prep",
    )(p, p, p, cos_t, sin_t, lp['nsa_q_norm'].reshape(1, NSA_HD),
      lp['nsa_k_norm_slc'].reshape(1, NSA_HD), lp['nsa_k_norm_win'].reshape(1, NSA_HD))


def _group_queries(q, g):
    return jnp.concatenate(
        [q[:, (g * NSA_HPG + h) * NSA_HD:(g * NSA_HPG + h + 1) * NSA_HD] for h in range(NSA_HPG)], axis=0)


def _store_group(o_ref, o, g, tq):
    for h in range(NSA_HPG):
        c0 = (g * NSA_HPG + h) * NSA_HD
        o_ref[0, :, c0:c0 + NSA_HD] = o[h * tq:(h + 1) * tq]


def _cmp_select_body(q_ref, kcc_ref, vcc_ref, ovl_ref, oc_ref, sel_ref, *, tq, nc, ns, selw):
    ncp = kcc_ref.shape[2]
    q0 = pl.program_id(1) * tq
    q = q_ref[0]
    tpos = q0 + lax.broadcasted_iota(jnp.int32, (tq, ncp), 0)
    cidx = lax.broadcasted_iota(jnp.int32, (tq, ncp), 1)
    vis = (cidx * CMP_STRIDE + (CMP_LEN - 1) <= tpos) & (cidx < nc)
    jrow = lax.broadcasted_iota(jnp.int32, (ns, tq), 0)
    cur = (q0 + lax.broadcasted_iota(jnp.int32, (ns, tq), 1)) // SLC_BLK
    valid = jrow <= cur
    forced = (jrow == 0) | (jrow == cur) | (jrow == cur - 1)
    sel_parts = []
    for g in range(NSA_KV_HEADS):
        s = lax.dot_general(_group_queries(q, g), kcc_ref[0, g], NT_DIMS, preferred_element_type=F32)
        sm = jnp.where(vis[None], s.reshape(NSA_HPG, tq, ncp), NEG_INF)
        m = jnp.max(sm, axis=-1, keepdims=True)
        m = jnp.where(m > NEG_INF, m, 0.0)
        p = jnp.exp(sm - m)
        p = p / jnp.maximum(jnp.sum(p, axis=-1, keepdims=True), 1e-30)
        o = jnp.dot(p.reshape(NSA_HPG * tq, ncp).astype(BF16), vcc_ref[0, g], preferred_element_type=F32)
        _store_group(oc_ref, o, g, tq)
        imp = p[0]
        for h in range(1, NSA_HPG):
            imp = imp + p[h]
        hi = imp.astype(BF16)
        lo = (imp - hi.astype(F32)).astype(BF16)
        score = (lax.dot_general(ovl_ref[...], hi, NT_DIMS, preferred_element_type=F32)
                 + lax.dot_general(ovl_ref[...], lo, NT_DIMS, preferred_element_type=F32))
        work = jnp.where(valid, jnp.where(forced, FORCE_SCORE, score), NEG_INF)
        chosen = jnp.zeros((ns, tq), F32)
        for _ in range(min(SLC_TOPN, ns)):
            mx = jnp.max(work, axis=0, keepdims=True)
            first = jnp.min(jnp.where(work == mx, jrow, ns), axis=0, keepdims=True)
            hit = jrow == first
            chosen = jnp.where(hit & (mx > NEG_INF), 1.0, chosen)
            work = jnp.where(hit, NEG_INF, work)
        sel_parts.append(chosen)
    if selw > NSA_KV_HEADS * ns:
        sel_parts.append(jnp.zeros((selw - NSA_KV_HEADS * ns, tq), F32))
    sel_ref[0] = jnp.concatenate(sel_parts, axis=0).T.astype(BF16)


def _sel_width(ns):
    return -(-NSA_KV_HEADS * ns // 128) * 128


def cmp_select_prompt(qn, kcc, vcc, nc):
    b, s, _ = qn.shape
    ncp = kcc.shape[2]
    assert ncp % LANE == 0
    ns = -(-s // SLC_BLK)
    selw = _sel_width(ns)
    tq = min(s, 256)
    cstart = np.arange(ncp) * CMP_STRIDE
    sstart = np.arange(ns) * SLC_BLK
    ovl = ((cstart[None, :] < sstart[:, None] + SLC_BLK) & (cstart[None, :] + CMP_LEN > sstart[:, None])
           & (np.arange(ncp)[None, :] < nc))
    ovl_t = jnp.asarray(ovl, dtype=BF16)
    kv_spec = pl.BlockSpec((1, NSA_KV_HEADS, ncp, NSA_HD), lambda bi, i: (bi, 0, 0, 0))
    return pl.pallas_call(
        functools.partial(_cmp_select_body, tq=tq, nc=nc, ns=ns, selw=selw),
        grid=(b, s // tq),
        in_specs=[pl.BlockSpec((1, tq, NSA_Q), lambda bi, i: (bi, i, 0)), kv_spec, kv_spec,
                  pl.BlockSpec((ns, ncp), lambda bi, i: (0, 0))],
        out_specs=[pl.BlockSpec((1, tq, NSA_Q), lambda bi, i: (bi, i, 0)),
                   pl.BlockSpec((1, tq, selw), lambda bi, i: (bi, i, 0))],
        out_shape=[jax.ShapeDtypeStruct((b, s, NSA_Q), F32), jax.ShapeDtypeStruct((b, s, selw), BF16)],
        compiler_params=pltpu.CompilerParams(dimension_semantics=("arbitrary", "arbitrary"),
                                             vmem_limit_bytes=VMEM_LIMIT_BYTES),
        name="cmp_select_prompt",
    )(qn, kcc, vcc, ovl_t)


SLC_TQ, SLC_TK = 256, 256


def _slc_body(q_ref, sel_ref, k_ref, v_ref, o_ref, *, tq, tk, ns):
    selw = sel_ref.shape[2]
    q0 = pl.program_id(1) * tq
    n_kv = (q0 + tq + tk - 1) // tk
    q = q_ref[0]
    selm = sel_ref[0]
    qpos = q0 + lax.broadcasted_iota(jnp.int32, (tq, tk), 0)
    kio = lax.broadcasted_iota(jnp.int32, (tq, tk), 1)
    erow = lax.broadcasted_iota(jnp.int32, (selw, tk), 0)
    ecol = lax.broadcasted_iota(jnp.int32, (selw, tk), 1)
    for g in range(NSA_KV_HEADS):
        gcols = slice(g * NSA_HD, (g + 1) * NSA_HD)
        q_heads = [q[:, (g * NSA_HPG + h) * NSA_HD:(g * NSA_HPG + h + 1) * NSA_HD] for h in range(NSA_HPG)]

        def body(j, carry, q_heads=q_heads, gcols=gcols, g=g):
            k0 = pl.multiple_of(j * tk, tk)
            kt = k_ref[0, pl.ds(k0, tk), gcols]
            vt = v_ref[0, pl.ds(k0, tk), gcols]
            expand = jnp.where(erow - g * ns == (ecol + k0) // SLC_BLK, 1.0, 0.0).astype(BF16)
            chosen = jnp.dot(selm, expand, preferred_element_type=F32)
            bias = jnp.where((chosen > 0.5) & (kio + k0 <= qpos), 0.0, NEG_BIG)
            out = []
            for h in range(NSA_HPG):
                m, l, acc = carry[h]
                s = lax.dot_general(q_heads[h], kt, NT_DIMS, preferred_element_type=F32) + bias
                m_new = jnp.maximum(m, jnp.max(s, axis=-1, keepdims=True))
                p = jnp.exp(s - m_new)
                alpha = jnp.exp(m - m_new)
                l = alpha * l + jnp.sum(p, axis=-1, keepdims=True)
                acc = alpha * acc + jnp.dot(p.astype(BF16), vt, preferred_element_type=F32)
                out.append((m_new, l, acc))
            return tuple(out)

        init = tuple((jnp.full((tq, 1), 0.5 * NEG_BIG, F32), jnp.zeros((tq, 1), F32), jnp.zeros((tq, NSA_HD), F32))
                     for _ in range(NSA_HPG))
        final = lax.fori_loop(0, n_kv, body, init)
        for h in range(NSA_HPG):
            _, l, acc = final[h]
            c0 = (g * NSA_HPG + h) * NSA_HD
            o_ref[0, :, c0:c0 + NSA_HD] = acc / jnp.maximum(l, 1e-30)


def slc_attention_prompt(qr, sel, k, v):
    b, s, _ = qr.shape
    selw = sel.shape[2]
    ns = -(-s // SLC_BLK)
    tq = min(s, SLC_TQ)
    tk = min(s, SLC_TK)
    kv_spec = pl.BlockSpec((1, s, NSA_KV), lambda bi, i: (bi, 0, 0))
    return pl.pallas_call(
        functools.partial(_slc_body, tq=tq, tk=tk, ns=ns),
        grid=(b, s // tq),
        in_specs=[pl.BlockSpec((1, tq, NSA_Q), lambda bi, i: (bi, i, 0)),
                  pl.BlockSpec((1, tq, selw), lambda bi, i: (bi, i, 0)), kv_spec, kv_spec],
        out_specs=pl.BlockSpec((1, tq, NSA_Q), lambda bi, i: (bi, i, 0)),
        out_shape=jax.ShapeDtypeStruct((b, s, NSA_Q), F32),
        compiler_params=pltpu.CompilerParams(dimension_semantics=("arbitrary", "arbitrary"),
                                             vmem_limit_bytes=VMEM_LIMIT_BYTES),
        name="slc_attention_prompt",
    )(qr, sel, k, v)


def _win_body(q_ref, k_ref, v_ref, o_ref, *, tq, span):
    q0 = pl.program_id(1) * tq
    kstart = pl.multiple_of(jnp.maximum(q0 - WINDOW, 0), tq)
    q = q_ref[0]
    qpos = q0 + lax.broadcasted_iota(jnp.int32, (tq, span), 0)
    kpos = kstart + lax.broadcasted_iota(jnp.int32, (tq, span), 1)
    allowed = ((kpos <= qpos) & (kpos > qpos - WINDOW))[None]
    for g in range(NSA_KV_HEADS):
        gcols = slice(g * NSA_HD, (g + 1) * NSA_HD)
        kt = k_ref[0, pl.ds(kstart, span), gcols]
        vt = v_ref[0, pl.ds(kstart, span), gcols]
        s = lax.dot_general(_group_queries(q, g), kt, NT_DIMS, preferred_element_type=F32)
        sm = jnp.where(allowed, s.reshape(NSA_HPG, tq, span), NEG_INF)
        p = jnp.exp(sm - jnp.max(sm, axis=-1, keepdims=True))
        l = jnp.sum(p, axis=-1, keepdims=True)
        o = jnp.dot(p.reshape(NSA_HPG * tq, span).astype(BF16), vt, preferred_element_type=F32)
        _store_group(o_ref, o / jnp.maximum(l.reshape(NSA_HPG * tq, 1), 1e-30), g, tq)


def win_attention_prompt(qr, k, v):
    b, s, _ = qr.shape
    tq = min(s, 256)
    span = min(s, WINDOW + tq)
    assert WINDOW % tq == 0 or s == tq
    kv_spec = pl.BlockSpec((1, s, NSA_KV), lambda bi, i: (bi, 0, 0))
    return pl.pallas_call(
        functools.partial(_win_body, tq=tq, span=span),
        grid=(b, s // tq),
        in_specs=[pl.BlockSpec((1, tq, NSA_Q), lambda bi, i: (bi, i, 0)), kv_spec, kv_spec],
        out_specs=pl.BlockSpec((1, tq, NSA_Q), lambda bi, i: (bi, i, 0)),
        out_shape=jax.ShapeDtypeStruct((b, s, NSA_Q), F32),
        compiler_params=pltpu.CompilerParams(dimension_semantics=("arbitrary", "arbitrary"),
                                             vmem_limit_bytes=VMEM_LIMIT_BYTES),
        name="win_attention_prompt",
    )(qr, k, v)


def nsa_combine_flat(gates, o_c, o_s, o_w):
    gt = jax.nn.sigmoid(gates.astype(jnp.float32))
    out = 0.0
    for br, o in enumerate((o_c, o_s, o_w)):
        out = out + jnp.repeat(gt[..., br * NSA_HEADS:(br + 1) * NSA_HEADS], NSA_HD, axis=-1) * o
    return out


CMP_PAGES_PER_STEP = 16
SUB_PER_PAGE = PAGE_SIZE // CMP_STRIDE
PAGE_ROWS = PAGE_SIZE * NSA_KV_HEADS
CMP_R = CMP_LEN // CMP_STRIDE
SUBLANES = 8
Q_ROWS = SUBLANES


def _pad_rows(x, rows):
    return jnp.concatenate([x, jnp.zeros((rows - x.shape[0], x.shape[1]), x.dtype)], axis=0)


def _token_group_queries(q, g):
    rows = [q[:, (g * NSA_HPG + h) * NSA_HD:(g * NSA_HPG + h + 1) * NSA_HD] for h in range(NSA_HPG)]
    return _pad_rows(jnp.concatenate(rows, axis=0), Q_ROWS)


def _subblock_weight(w1):
    w = w1.reshape(CMP_R, CMP_STRIDE, NSA_HD, NSA_HD)
    return w.transpose(1, 2, 0, 3).reshape(CMP_STRIDE * NSA_HD, CMP_R * NSA_HD).astype(BF16)


def _pe_rows(pe):
    return _pad_rows(pe.reshape(CMP_R, CMP_STRIDE * NSA_HD), Q_ROWS).astype(BF16)


def _compress_from_parts(parts, pe_rows, w_sub, w2):
    n = parts.shape[0]
    hid = parts[:, :NSA_HD]
    for r in range(1, CMP_R):
        hid = hid + pltpu.roll(parts[:, r * NSA_HD:(r + 1) * NSA_HD], n - r, 0)
    pe_proj = jnp.dot(pe_rows, w_sub, preferred_element_type=F32)
    for r in range(CMP_R):
        hid = hid + pe_proj[r:r + 1, r * NSA_HD:(r + 1) * NSA_HD]
    act = hid * jax.nn.sigmoid(hid)
    return jnp.dot(act.astype(BF16), w2, preferred_element_type=F32)


def _cmp_sample_body(pt_ref, *refs, q_pos, nc):
    npg = CMP_PAGES_PER_STEP
    k_pages, v_pages = refs[:npg], refs[npg:2 * npg]
    (wk_ref, wv_ref, pek_ref, pev_ref, w2k_ref, w2v_ref, gk_ref, q_ref,
     oc_ref, imp_ref, pk_ref, pv_ref, regroup_ref) = refs[2 * npg:]
    c = pl.program_id(1)
    rows = npg * SUB_PER_PAGE

    def project(pages, w_ref, parts_ref):
        per_sub = CMP_STRIDE * NSA_KV_HEADS
        assert per_sub % SUBLANES == 0
        for pi, p in enumerate(pages):
            for r0 in range(0, PAGE_ROWS, SUBLANES):
                nn, sg0 = r0 // per_sub, r0 % per_sub
                regroup_ref[pi, pl.ds(sg0 * SUB_PER_PAGE + nn, SUBLANES, stride=SUB_PER_PAGE), :] = (
                    p[0, r0:r0 + SUBLANES, :])
        tile = lambda pi, s, g: regroup_ref[pi, (s * NSA_KV_HEADS + g) * SUB_PER_PAGE:
                                            (s * NSA_KV_HEADS + g + 1) * SUB_PER_PAGE, :]
        xg = jnp.concatenate(
            [jnp.concatenate(
                [jnp.concatenate([tile(pi, s, g) for s in range(CMP_STRIDE)], axis=1) for pi in range(len(pages))],
                axis=0)
             for g in range(NSA_KV_HEADS)], axis=0).astype(BF16)
        parts = jnp.dot(xg, w_ref[...], preferred_element_type=F32)
        for g in range(NSA_KV_HEADS):
            parts_ref[g, pl.ds(pl.multiple_of(c * rows, rows), rows), :] = parts[g * rows:(g + 1) * rows]

    project(k_pages, wk_ref, pk_ref)
    project(v_pages, wv_ref, pv_ref)

    @pl.when(c == pl.num_programs(1) - 1)
    def _():
        ncp = pk_ref.shape[1]
        q = q_ref[0]
        cidx = lax.broadcasted_iota(jnp.int32, (Q_ROWS, ncp), 1)
        vis = (cidx < nc) & (cidx * CMP_STRIDE + (CMP_LEN - 1) <= q_pos)
        for g in range(NSA_KV_HEADS):
            kcc = _compress_from_parts(pk_ref[g], pek_ref[...], wk_ref[...], w2k_ref[...])
            kcc = _head_norm(kcc, gk_ref[...])
            vcc = _compress_from_parts(pv_ref[g], pev_ref[...], wv_ref[...], w2v_ref[...])
            s = lax.dot_general(_token_group_queries(q, g), kcc.astype(BF16), NT_DIMS, preferred_element_type=F32)
            sm = jnp.where(vis, s, NEG_INF)
            m = jnp.max(sm, axis=-1, keepdims=True)
            m = jnp.where(m > NEG_INF, m, 0.0)
            p = jnp.exp(sm - m)
            p = p / jnp.maximum(jnp.sum(p, axis=-1, keepdims=True), 1e-30)
            o = jnp.dot(p.astype(BF16), vcc.astype(BF16), preferred_element_type=F32)
            for h in range(NSA_HPG):
                c0 = (g * NSA_HPG + h) * NSA_HD
                oc_ref[0, :, c0:c0 + NSA_HD] = o[h:h + 1]
            imp_ref[0, g:g + 1, :] = jnp.sum(p[:NSA_HPG], axis=0, keepdims=True)


def cmp_attention_sample(qn, pool_k, pool_v, page_table, lp, q_pos):
    db, n_pages = page_table.shape
    n_pool = pool_k.shape[0]
    npg = CMP_PAGES_PER_STEP
    assert n_pages % npg == 0
    n_sub = n_pages * SUB_PER_PAGE
    nc = n_sub - CMP_R + 1
    view = lambda p: p.reshape(n_pool, PAGE_ROWS, NSA_HD)
    page_spec = lambda i: pl.BlockSpec((1, PAGE_ROWS, NSA_HD), lambda b, c, pt: (pt[b, c * npg + i], 0, 0))
    const = lambda shape: pl.BlockSpec(shape, lambda b, c, pt: (0,) * len(shape))
    kdim = CMP_STRIDE * NSA_HD
    grid_spec = pltpu.PrefetchScalarGridSpec(
        num_scalar_prefetch=1,
        grid=(db, n_pages // npg),
        in_specs=([page_spec(i) for i in range(npg)] + [page_spec(i) for i in range(npg)]
                  + [const((kdim, CMP_R * NSA_HD)), const((kdim, CMP_R * NSA_HD)),
                     const((Q_ROWS, kdim)), const((Q_ROWS, kdim)),
                     const((NSA_HD, NSA_HD)), const((NSA_HD, NSA_HD)), const((1, NSA_HD)),
                     pl.BlockSpec((1, 1, NSA_Q), lambda b, c, pt: (b, 0, 0))]),
        out_specs=[pl.BlockSpec((1, 1, NSA_Q), lambda b, c, pt: (b, 0, 0)),
                   pl.BlockSpec((1, NSA_KV_HEADS, n_sub), lambda b, c, pt: (b, 0, 0))],
        scratch_shapes=[pltpu.VMEM((NSA_KV_HEADS, n_sub, CMP_R * NSA_HD), F32),
                        pltpu.VMEM((NSA_KV_HEADS, n_sub, CMP_R * NSA_HD), F32),
                        pltpu.VMEM((npg, PAGE_ROWS, NSA_HD), F32)],
    )
    return pl.pallas_call(
        functools.partial(_cmp_sample_body, q_pos=q_pos, nc=nc),
        grid_spec=grid_spec,
        out_shape=[jax.ShapeDtypeStruct((db, 1, NSA_Q), F32), jax.ShapeDtypeStruct((db, NSA_KV_HEADS, n_sub), F32)],
        compiler_params=pltpu.CompilerParams(dimension_semantics=("arbitrary", "arbitrary"),
                                             vmem_limit_bytes=VMEM_LIMIT_BYTES),
        name="cmp_attention_sample",
    )(page_table, *([view(pool_k)] * npg), *([view(pool_v)] * npg),
      _subblock_weight(lp['cmp_w1_k']), _subblock_weight(lp['cmp_w1_v']),
      _pe_rows(lp['cmp_pe_k']), _pe_rows(lp['cmp_pe_v']),
      lp['cmp_w2_k'].astype(BF16), lp['cmp_w2_v'].astype(BF16),
      lp['nsa_k_norm_cmp'].reshape(1, NSA_HD), qn), nc


def _cmp_kv_prompt_body(*refs, n_sub):
    kc_refs, vc_refs = refs[:NSA_KV_HEADS], refs[NSA_KV_HEADS:2 * NSA_KV_HEADS]
    wk_ref, wv_ref, pek_ref, pev_ref, w2k_ref, w2v_ref, gk_ref, kcc_ref, vcc_ref = refs[2 * NSA_KV_HEADS:]

    def compress(x_ref, w_ref, pe_ref, w2_ref):
        x = jnp.concatenate([x_ref[0, pl.ds(s, n_sub, stride=CMP_STRIDE), :] for s in range(CMP_STRIDE)],
                            axis=1).astype(BF16)
        parts = jnp.dot(x, w_ref[...], preferred_element_type=F32)
        return _compress_from_parts(parts, pe_ref[...], w_ref[...], w2_ref[...])

    for g in range(NSA_KV_HEADS):
        kcc = _head_norm(compress(kc_refs[g], wk_ref, pek_ref, w2k_ref), gk_ref[...])
        kcc_ref[0, g] = kcc.astype(kcc_ref.dtype)
        vcc_ref[0, g] = compress(vc_refs[g], wv_ref, pev_ref, w2v_ref).astype(vcc_ref.dtype)


def compressed_kv_prompt(p, lp):
    b, s, _ = p.shape
    assert s % CMP_STRIDE == 0
    n_sub = s // CMP_STRIDE
    nc = n_sub - CMP_R + 1
    kdim = CMP_STRIDE * NSA_HD
    kv_w = NSA_KV // LANE
    const = lambda shape: pl.BlockSpec(shape, lambda bi: (0,) * len(shape))
    assert NSA_HD == LANE
    src = lambda cb: pl.BlockSpec((1, s, NSA_HD), lambda bi: (bi, 0, cb))
    out = pl.BlockSpec((1, NSA_KV_HEADS, n_sub, NSA_HD), lambda bi: (bi, 0, 0, 0))
    kcc, vcc = pl.pallas_call(
        functools.partial(_cmp_kv_prompt_body, n_sub=n_sub),
        grid=(b,),
        in_specs=[src(CB_KV + g) for g in range(NSA_KV_HEADS)] + [src(CB_KV + kv_w + g) for g in range(NSA_KV_HEADS)] + [
                  const((kdim, CMP_R * NSA_HD)), const((kdim, CMP_R * NSA_HD)),
                  const((Q_ROWS, kdim)), const((Q_ROWS, kdim)),
                  const((NSA_HD, NSA_HD)), const((NSA_HD, NSA_HD)), const((1, NSA_HD))],
        out_specs=[out, out],
        out_shape=[jax.ShapeDtypeStruct((b, NSA_KV_HEADS, n_sub, NSA_HD), BF16)] * 2,
        compiler_params=pltpu.CompilerParams(dimension_semantics=("arbitrary",),
                                             vmem_limit_bytes=VMEM_LIMIT_BYTES),
        name="compressed_kv_prompt",
    )(*([p] * (2 * NSA_KV_HEADS)), _subblock_weight(lp['cmp_w1_k']), _subblock_weight(lp['cmp_w1_v']),
      _pe_rows(lp['cmp_pe_k']), _pe_rows(lp['cmp_pe_v']),
      lp['cmp_w2_k'].astype(BF16), lp['cmp_w2_v'].astype(BF16), lp['nsa_k_norm_cmp'].reshape(1, NSA_HD))
    return kcc, vcc, nc


def _gdn_sample_body(qkv_ref, buf_ref, w_ref, sm_ref, z_ref, rec_ref, alog_ref, dt_ref, gn_ref,
                     o_ref, conv_ref, rec_out_ref):
    x = qkv_ref[0]
    buf = buf_ref[0]
    w = w_ref[...]
    acc = x * w[GDN_CONV - 1:GDN_CONV]
    for j in range(GDN_CONV - 1):
        acc = acc + buf[j:j + 1] * w[j:j + 1]
    for j in range(GDN_CONV - 2):
        conv_ref[0, j:j + 1, :] = buf[j + 1:j + 2]
    conv_ref[0, GDN_CONV - 2:GDN_CONV - 1, :] = x
    cv = acc * jax.nn.sigmoid(acc)
    g_all, beta_all = _gdn_gates(sm_ref[0], alog_ref[...], dt_ref[...])
    z = z_ref[0]
    rows = []
    heads = []
    for h in range(GDN_HEADS):
        q = cv[:, h * GDN_DK:(h + 1) * GDN_DK]
        k = cv[:, GDN_QK + h * GDN_DK:GDN_QK + (h + 1) * GDN_DK]
        v = cv[:, 2 * GDN_QK + h * GDN_DV:2 * GDN_QK + (h + 1) * GDN_DV]
        q = q * lax.rsqrt(jnp.sum(q * q, axis=-1, keepdims=True) + EPS) * (GDN_DK ** -0.5)
        k = k * lax.rsqrt(jnp.sum(k * k, axis=-1, keepdims=True) + EPS)
        g = g_all[:, SMALL_A + h:SMALL_A + h + 1]
        beta = beta_all[:, SMALL_B + h:SMALL_B + h + 1]
        eg = jnp.exp(g)
        heads.append((q, k, v * beta, eg, jnp.sum(q * k, axis=-1, keepdims=True)))
        rows += [q * eg, k * beta * eg, k]
    stacked = jnp.concatenate(rows, axis=0)
    eye = jnp.where(lax.broadcasted_iota(jnp.int32, (GDN_DK, GDN_DK), 0)
                    == lax.broadcasted_iota(jnp.int32, (GDN_DK, GDN_DK), 1), 1.0, 0.0).astype(BF16)
    cols = sum(lax.dot_general(eye, piece, NT_DIMS, preferred_element_type=F32) for piece in _split3(stacked))
    for h, (q, k, u, eg, attn) in enumerate(heads):
        s_old = rec_ref[0, h]
        qe_col, w_col, k_col = (cols[:, 3 * h + i:3 * h + i + 1] for i in range(3))
        v_new = u - jnp.sum(w_col * s_old, axis=0, keepdims=True)
        o = jnp.sum(qe_col * s_old, axis=0, keepdims=True) + attn * v_new
        rec_out_ref[0, h] = s_old * eg + k_col * v_new
        zc = z[:, h * GDN_DV:(h + 1) * GDN_DV]
        o_ref[0, :, h * GDN_DV:(h + 1) * GDN_DV] = _head_norm(o, gn_ref[...]) * (zc * jax.nn.sigmoid(zc))


def gdn_sample(p, conv_buf, rec, lp):
    db = p.shape[0]
    tok = lambda w, cb: pl.BlockSpec((1, 1, w), lambda b: (b, 0, cb * LANE // w))
    vec = pl.BlockSpec((1, LANE), lambda b: (0, 0))
    buf = pl.BlockSpec((1, GDN_CONV - 1, GDN_CONV_DIM), lambda b: (b, 0, 0))
    st = pl.BlockSpec((1, GDN_HEADS, GDN_DK, GDN_DV), lambda b: (b, 0, 0, 0))
    assert CB_QKV == 0 and (CB_Z * LANE) % GDN_V == 0
    return pl.pallas_call(
        _gdn_sample_body,
        grid=(db,),
        in_specs=[tok(GDN_CONV_DIM, CB_QKV), buf, pl.BlockSpec((GDN_CONV, GDN_CONV_DIM), lambda b: (0, 0)),
                  tok(LANE, CB_SMALL), tok(GDN_V, CB_Z), st, vec, vec, vec],
        out_specs=[pl.BlockSpec((1, 1, GDN_V), lambda b: (b, 0, 0)), buf, st],
        out_shape=[jax.ShapeDtypeStruct((db, 1, GDN_V), F32),
                   jax.ShapeDtypeStruct((db, GDN_CONV - 1, GDN_CONV_DIM), F32),
                   jax.ShapeDtypeStruct((db, GDN_HEADS, GDN_DK, GDN_DV), F32)],
        compiler_params=pltpu.CompilerParams(dimension_semantics=("arbitrary",),
                                             vmem_limit_bytes=VMEM_LIMIT_BYTES),
        name="gdn_sample",
    )(p, conv_buf, lp['gdn_conv_w'], p, p, rec,
      _lane_row(lp['gdn_a_log'], SMALL_A), _lane_row(lp['gdn_dt_bias'], SMALL_A),
      lp['gdn_out_norm'].reshape(1, GDN_DV))


def _merge_body(x_ref, oa_ref, oc_ref, os_ref, ow_ref, ga_ref, gb_ref, sm_ref, wo_ref, y_ref):
    gates = jax.nn.sigmoid(sm_ref[...])
    o_b = jnp.zeros(x_ref.shape, F32)
    for br, ref in enumerate((oc_ref, os_ref, ow_ref)):
        o = ref[...]
        o_b = o_b + jnp.concatenate(
            [gates[:, SMALL_G + br * NSA_HEADS + h:SMALL_G + br * NSA_HEADS + h + 1] * o[:, h * NSA_HD:(h + 1) * NSA_HD]
             for h in range(NSA_HEADS)], axis=1)
    m = jax.nn.sigmoid(ga_ref[...]) * oa_ref[...] + jax.nn.sigmoid(gb_ref[...]) * o_b
    y_ref[...] = x_ref[...] + jnp.dot(m.astype(BF16), wo_ref[...], preferred_element_type=F32)


def merge_branches_fused(x2, p2, o_a, o_c, o_s, o_w, w_o):
    n = x2.shape[0]
    tm = min(n, 512)
    row = pl.BlockSpec((tm, D_MODEL), lambda i: (i, 0))
    pcol = lambda w, cb: pl.BlockSpec((tm, w), lambda i: (i, cb * LANE // w))
    return pl.pallas_call(
        _merge_body,
        grid=(n // tm,),
        in_specs=[row, row, row, row, row, pcol(D_MODEL, CB_GA), pcol(D_MODEL, CB_GB), pcol(LANE, CB_SMALL),
                  _const_spec((D_MODEL, D_MODEL))],
        out_specs=row,
        out_shape=jax.ShapeDtypeStruct((n, D_MODEL), F32),
        compiler_params=pltpu.CompilerParams(dimension_semantics=("arbitrary",),
                                             vmem_limit_bytes=VMEM_LIMIT_BYTES),
        name="merge_branches",
    )(x2, o_a, o_c, o_s, o_w, p2, p2, p2, w_o.astype(BF16))


IDX_LANES = 128


def _select_sample_body(imp_ref, ovl_ref, idx_ref, *, q_pos, ns):
    imp = imp_ref[...]
    hi = imp.astype(BF16)
    lo = (imp - hi.astype(F32)).astype(BF16)
    score = (jnp.dot(hi, ovl_ref[...], preferred_element_type=F32)
             + jnp.dot(lo, ovl_ref[...], preferred_element_type=F32))
    rows, nsp = score.shape
    j = lax.broadcasted_iota(jnp.int32, (rows, nsp), 1)
    cur = q_pos // SLC_BLK
    valid = (j <= cur) & (j < ns)
    forced = (j == 0) | (j == cur) | (j == cur - 1)
    work = jnp.where(valid, jnp.where(forced, FORCE_SCORE, score), NEG_INF)
    slot = lax.broadcasted_iota(jnp.int32, (rows, IDX_LANES), 1)
    idx = jnp.full((rows, IDX_LANES), -1, jnp.int32)
    for it in range(min(SLC_TOPN, ns)):
        mx = jnp.max(work, axis=-1, keepdims=True)
        first = jnp.min(jnp.where(work == mx, j, nsp), axis=-1, keepdims=True)
        idx = jnp.where((slot == it) & (mx > NEG_INF), first, idx)
        work = jnp.where(j == first, NEG_INF, work)
    idx_ref[...] = idx


def select_blocks_sample(imp, nc, q_pos, n_keys):
    db, kv, ncp = imp.shape
    ns = -(-n_keys // SLC_BLK)
    nsp = -(-ns // 128) * 128
    cstart = np.arange(ncp) * CMP_STRIDE
    sstart = np.arange(nsp) * SLC_BLK
    ovl = ((cstart[:, None] < sstart[None, :] + SLC_BLK) & (cstart[:, None] + CMP_LEN > sstart[None, :])
           & (np.arange(ncp)[:, None] < nc) & (np.arange(nsp)[None, :] < ns))
    rows = db * kv
    idx = pl.pallas_call(
        functools.partial(_select_sample_body, q_pos=q_pos, ns=ns),
        grid=(1,),
        in_specs=[pl.BlockSpec((rows, ncp), lambda i: (0, 0)), pl.BlockSpec((ncp, nsp), lambda i: (0, 0))],
        out_specs=pl.BlockSpec((rows, IDX_LANES), lambda i: (0, 0)),
        out_shape=jax.ShapeDtypeStruct((rows, IDX_LANES), jnp.int32),
        compiler_params=pltpu.CompilerParams(vmem_limit_bytes=VMEM_LIMIT_BYTES),
        name="select_blocks_sample",
    )(imp.reshape(rows, ncp), jnp.asarray(ovl, dtype=BF16))
    return idx.reshape(db, kv, IDX_LANES)


def _token_softmax_attend(q8, s_past, allowed, v_past, k_new, v_new, new_ok):
    s_new = jnp.sum(q8.astype(F32) * k_new, axis=-1, keepdims=True)
    sm = jnp.where(allowed, s_past, NEG_INF)
    m = jnp.maximum(jnp.max(sm, axis=-1, keepdims=True), jnp.where(new_ok, s_new, NEG_INF))
    m = jnp.where(m > NEG_INF, m, 0.0)
    p = jnp.exp(sm - m)
    p_new = jnp.where(new_ok, jnp.exp(s_new - m), 0.0)
    l = jnp.sum(p, axis=-1, keepdims=True) + p_new
    o = jnp.dot(p.astype(BF16), v_past, preferred_element_type=F32) + p_new * v_new
    return o / jnp.maximum(l, 1e-30)


def _slc_sample_body(idx_ref, pt_ref, *refs, n_past):
    n = SLC_TOPN
    k_blocks, v_blocks = refs[:n], refs[n:2 * n]
    q_ref, kn_ref, vn_ref, o_ref = refs[2 * n:]
    b, g = pl.program_id(0), pl.program_id(1)
    q8 = _token_group_queries_half(q_ref[0])

    def group_rows(blocks):
        sel_g = jnp.full((SLC_TOPN * SLC_BLK, NSA_HD), g, jnp.int32)
        per_group = [jnp.concatenate([r[0, pl.ds(gg, SLC_BLK, stride=NSA_KV_HEADS), :] for r in blocks], axis=0)
                     for gg in range(NSA_KV_HEADS)]
        out = per_group[0]
        for gg in range(1, NSA_KV_HEADS):
            out = jnp.where(sel_g == gg, per_group[gg], out)
        return out.astype(BF16)

    k = group_rows(k_blocks)
    v = group_rows(v_blocks)
    s = lax.dot_general(q8, k, NT_DIMS, preferred_element_type=F32)
    slot = lax.broadcasted_iota(jnp.int32, s.shape, 1) // SLC_BLK
    slot_ok = jnp.zeros(s.shape, jnp.int32)
    n_new = jnp.int32(0)
    for i in range(n):
        ji = idx_ref[(b * NSA_KV_HEADS + g) * n + i]
        slot_ok = jnp.where(slot == i, jnp.where((ji >= 0) & (ji < n_past), 1, 0), slot_ok)
        n_new = n_new + jnp.where(ji >= n_past, 1, 0)
    new_ok = jnp.full((Q_ROWS, 1), n_new, jnp.int32) > 0
    o = _token_softmax_attend(q8, s, slot_ok > 0, v, kn_ref[0], vn_ref[0], new_ok)
    for h in range(NSA_HPG):
        o_ref[0, :, h * NSA_HD:(h + 1) * NSA_HD] = o[h:h + 1]


def _token_group_queries_half(q):
    rows = [q[:, h * NSA_HD:(h + 1) * NSA_HD] for h in range(NSA_HPG)]
    return _pad_rows(jnp.concatenate(rows, axis=0), Q_ROWS)


def slc_attention_sample(qr, idx, pool_k, pool_v, page_table, k_new, v_new):
    db, n_pages = page_table.shape
    n_pool = pool_k.shape[0]
    bpp = PAGE_SIZE // SLC_BLK
    n_past = n_pages * bpp
    n = SLC_TOPN
    gw = NSA_HPG * NSA_HD
    idx_flat = idx[:, :, :n].reshape(-1)
    pt_flat = page_table.reshape(-1)
    view = lambda p: p.reshape(n_pool * bpp, SLC_BLK * NSA_KV_HEADS, NSA_HD)

    def blk_spec(i):
        def index_map(b, g, idx_s, pt_s):
            jc = jnp.clip(idx_s[(b * NSA_KV_HEADS + g) * n + i], 0, n_past - 1)
            return (pt_s[b * n_pages + jc // bpp] * bpp + jc % bpp, 0, 0)
        return pl.BlockSpec((1, SLC_BLK * NSA_KV_HEADS, NSA_HD), index_map)

    grid_spec = pltpu.PrefetchScalarGridSpec(
        num_scalar_prefetch=2,
        grid=(db, NSA_KV_HEADS),
        in_specs=([blk_spec(i) for i in range(n)] + [blk_spec(i) for i in range(n)]
                  + [pl.BlockSpec((1, 1, gw), lambda b, g, *_: (b, 0, g)),
                     pl.BlockSpec((1, 1, NSA_HD), lambda b, g, *_: (b, 0, g)),
                     pl.BlockSpec((1, 1, NSA_HD), lambda b, g, *_: (b, 0, g))]),
        out_specs=pl.BlockSpec((1, 1, gw), lambda b, g, *_: (b, 0, g)),
    )
    return pl.pallas_call(
        functools.partial(_slc_sample_body, n_past=n_past),
        grid_spec=grid_spec,
        out_shape=jax.ShapeDtypeStruct((db, 1, NSA_Q), F32),
        compiler_params=pltpu.CompilerParams(dimension_semantics=("arbitrary", "arbitrary"),
                                             vmem_limit_bytes=VMEM_LIMIT_BYTES),
        name="slc_attention_sample",
    )(idx_flat, pt_flat, *([view(pool_k)] * n), *([view(pool_v)] * n), qr, k_new, v_new)


def _win_sample_body(q_ref, bk_ref, bv_ref, kn_ref, vn_ref, o_ref, nk_ref, nv_ref, *, q_pos, buf_start):
    rows = bk_ref.shape[1]
    wb = rows // NSA_KV_HEADS
    kn, vn = kn_ref[0], vn_ref[0]
    q = q_ref[0]
    kpos = buf_start + lax.broadcasted_iota(jnp.int32, (Q_ROWS, wb), 1)
    allowed = (kpos <= q_pos) & (kpos > q_pos - WINDOW)
    new_ok = jnp.ones((Q_ROWS, 1), jnp.bool_)
    for g in range(NSA_KV_HEADS):
        gcols = slice(g * NSA_HD, (g + 1) * NSA_HD)
        q8 = _token_group_queries(q, g)
        kg = bk_ref[0, pl.ds(g, wb, stride=NSA_KV_HEADS), :].astype(BF16)
        vg = bv_ref[0, pl.ds(g, wb, stride=NSA_KV_HEADS), :].astype(BF16)
        s = lax.dot_general(q8, kg, NT_DIMS, preferred_element_type=F32)
        o = _token_softmax_attend(q8, s, allowed, vg, kn[:, gcols], vn[:, gcols], new_ok)
        for h in range(NSA_HPG):
            c0 = (g * NSA_HPG + h) * NSA_HD
            o_ref[0, :, c0:c0 + NSA_HD] = o[h:h + 1]
    row = lax.broadcasted_iota(jnp.int32, (rows, NSA_HD), 0)

    def shifted(buf_ref, new):
        out = pltpu.roll(buf_ref[0], rows - NSA_KV_HEADS, 0)
        for g in range(NSA_KV_HEADS):
            out = jnp.where(row == rows - NSA_KV_HEADS + g, new[:, g * NSA_HD:(g + 1) * NSA_HD], out)
        return out

    nk_ref[0] = shifted(bk_ref, kn)
    nv_ref[0] = shifted(bv_ref, vn)


def win_attention_sample(qr, buf_k, buf_v, k_new, v_new, q_pos):
    db, rows, _ = buf_k.shape
    wb = rows // NSA_KV_HEADS
    tok = lambda w: pl.BlockSpec((1, 1, w), lambda b: (b, 0, 0))
    buf = pl.BlockSpec((1, rows, NSA_HD), lambda b: (b, 0, 0))
    return pl.pallas_call(
        functools.partial(_win_sample_body, q_pos=q_pos, buf_start=q_pos - wb),
        grid=(db,),
        in_specs=[tok(NSA_Q), buf, buf, tok(NSA_KV), tok(NSA_KV)],
        out_specs=[tok(NSA_Q), buf, buf],
        out_shape=[jax.ShapeDtypeStruct((db, 1, NSA_Q), F32), jax.ShapeDtypeStruct((db, rows, NSA_HD), F32),
                   jax.ShapeDtypeStruct((db, rows, NSA_HD), F32)],
        compiler_params=pltpu.CompilerParams(dimension_semantics=("arbitrary",),
                                             vmem_limit_bytes=VMEM_LIMIT_BYTES),
        name="win_attention_sample",
    )(qr, buf_k, buf_v, k_new, v_new)


def rms_norm(x, g):
    xf = x.astype(jnp.float32)
    y = xf * lax.rsqrt(jnp.mean(xf * xf, axis=-1, keepdims=True) + EPS)
    return (y * g.astype(jnp.float32)).astype(x.dtype)


def l2_norm(x):
    return x * lax.rsqrt(jnp.sum(x * x, axis=-1, keepdims=True) + EPS)


def rope_partial(x, pos):
    half = ROPE_DIM // 2
    inv = jnp.float32(ROPE_THETA) ** (-jnp.arange(half, dtype=jnp.float32) / half)
    ang = pos.astype(jnp.float32)[:, None] * inv
    shape = (pos.shape[0],) + (1,) * (x.ndim - 3) + (half,)
    cos = jnp.cos(ang).reshape(shape)
    sin = jnp.sin(ang).reshape(shape)
    xf = x.astype(jnp.float32)
    x1, x2, rest = xf[..., :half], xf[..., half:ROPE_DIM], xf[..., ROPE_DIM:]
    out = jnp.concatenate([x1 * cos - x2 * sin, x2 * cos + x1 * sin, rest], axis=-1)
    return out.astype(x.dtype)


def masked_softmax(s, mask, axis=-1):
    s = jnp.where(mask, s.astype(jnp.float32), -jnp.inf)
    m = jnp.max(s, axis=axis, keepdims=True)
    m = jnp.where(jnp.isfinite(m), m, 0.0)
    p = jnp.exp(s - m)
    return p / jnp.maximum(jnp.sum(p, axis=axis, keepdims=True), 1e-30)


def split_cols(p, sizes):
    offs = np.cumsum((0,) + tuple(sizes))
    return [p[..., int(offs[i]):int(offs[i + 1])] for i in range(len(sizes))]


def mixer_inputs(x, lp):
    h = rms_norm(x, lp['mix_norm'])
    return split_cols(h @ lp['w_in'], IN_SIZES)


def causal_conv(x, buf, w):
    t = x.shape[1]
    xp = jnp.concatenate([buf.astype(x.dtype), x], axis=1)
    out = xp[:, 0:t] * w[0]
    for j in range(1, GDN_CONV):
        out = out + xp[:, j:j + t] * w[j]
    return jax.nn.silu(out), xp[:, t:]


def gated_delta_rule(q, k, v, g, beta, s0):
    b, t, h, dk = q.shape
    dv = v.shape[-1]
    c = min(GDN_CHUNK, t)
    pad = (-t) % c

    def prep(a):
        a = jnp.pad(a, [(0, 0), (0, pad)] + [(0, 0)] * (a.ndim - 2))
        a = jnp.moveaxis(a, 2, 1)
        return a.reshape((b, h, -1, c) + a.shape[3:])

    q, k, v, g, beta = [prep(a) for a in (q * dk ** -0.5, k, v, g, beta)]
    gc = jnp.cumsum(g, axis=-1)
    ii = jnp.arange(c)
    incl = ii[:, None] >= ii[None, :]
    strict = ii[:, None] > ii[None, :]
    diff = gc[..., :, None] - gc[..., None, :]
    decay = jnp.where(incl, jnp.exp(jnp.where(incl, diff, 0.0)), 0.0)
    kb = k * beta[..., None]
    m = jnp.where(strict, jnp.einsum('bhncd,bhnsd->bhncs', kb, k) * decay, 0.0)
    a_mat = m + jnp.eye(c, dtype=m.dtype)
    rhs = jnp.concatenate([v * beta[..., None], kb * jnp.exp(gc)[..., None]], axis=-1)
    sol = lax.linalg.triangular_solve(a_mat, rhs, left_side=True, lower=True, unit_diagonal=True)
    u, w = sol[..., :dv], sol[..., dv:]
    attn = jnp.where(incl, jnp.einsum('bhncd,bhnsd->bhncs', q, k) * decay, 0.0)

    def step(s, inp):
        q_c, k_c, u_c, w_c, gc_c, attn_c = inp
        v_new = u_c - jnp.einsum('bhcd,bhde->bhce', w_c, s)
        o = (jnp.einsum('bhcd,bhde->bhce', q_c * jnp.exp(gc_c)[..., None], s)
             + jnp.einsum('bhcs,bhse->bhce', attn_c, v_new))
        g_last = gc_c[..., -1]
        k_dec = k_c * jnp.exp(g_last[..., None] - gc_c)[..., None]
        s = s * jnp.exp(g_last)[..., None, None] + jnp.einsum('bhcd,bhce->bhde', k_dec, v_new)
        return s, o

    xs = tuple(jnp.moveaxis(a, 2, 0) for a in (q, k, u, w, gc, attn))
    s_fin, o = lax.scan(step, s0, xs)
    o = jnp.moveaxis(o, 0, 2).reshape(b, h, -1, dv)[:, :, :t]
    return jnp.moveaxis(o, 1, 2), s_fin


def gdn_mixer(qkv, a, b, z, conv_buf, rec, lp):
    bsz, t, _ = qkv.shape
    qkv_c, conv_new = causal_conv(qkv, conv_buf, lp['gdn_conv_w'])
    qkv_c = qkv_c.astype(jnp.float32)
    q = l2_norm(qkv_c[..., :GDN_QK].reshape(bsz, t, GDN_HEADS, GDN_DK))
    k = l2_norm(qkv_c[..., GDN_QK:2 * GDN_QK].reshape(bsz, t, GDN_HEADS, GDN_DK))
    v = qkv_c[..., 2 * GDN_QK:].reshape(bsz, t, GDN_HEADS, GDN_DV)
    g = -jnp.exp(lp['gdn_a_log'].astype(jnp.float32)) * jax.nn.softplus(
        a.astype(jnp.float32) + lp['gdn_dt_bias'].astype(jnp.float32))
    beta = jax.nn.sigmoid(b.astype(jnp.float32))
    o, rec_new = gated_delta_rule(q, k, v, g, beta, rec.astype(jnp.float32))
    zh = z.astype(jnp.float32).reshape(bsz, t, GDN_HEADS, GDN_DV)
    o = rms_norm(o, lp['gdn_out_norm']) * jax.nn.silu(zh)
    return o.reshape(bsz, t, GDN_V).astype(qkv.dtype), conv_new, rec_new.astype(rec.dtype)


def nsa_project(q, kc, vc, ks, vs, kw, vw, pos, lp):
    b, t, _ = q.shape

    def heads(a, n):
        return a.reshape(b, t, n, NSA_HD)

    qn = rms_norm(heads(q, NSA_HEADS), lp['nsa_q_norm']).reshape(b, t, NSA_KV_HEADS, NSA_HPG, NSA_HD)
    qr = rope_partial(qn, pos)
    ks_r = rope_partial(rms_norm(heads(ks, NSA_KV_HEADS), lp['nsa_k_norm_slc']), pos)
    kw_r = rope_partial(rms_norm(heads(kw, NSA_KV_HEADS), lp['nsa_k_norm_win']), pos)
    return (qn, qr, heads(kc, NSA_KV_HEADS), heads(vc, NSA_KV_HEADS), ks_r,
            heads(vs, NSA_KV_HEADS), kw_r, heads(vw, NSA_KV_HEADS))


def subblock_proj(rows, w1):
    b, l, g, hd = rows.shape
    nh = l // CMP_STRIDE
    sub = rows[:, :nh * CMP_STRIDE].reshape(b, nh, CMP_STRIDE, g, hd)
    w = w1.reshape(CMP_LEN // CMP_STRIDE, CMP_STRIDE, hd, hd)
    return jnp.einsum('bnsgd,rsde->rbnge', sub, w)


def compress(parts, w1, w2, pe):
    r_sub = parts.shape[0]
    nc = parts.shape[2] - r_sub + 1
    hid = parts[0, :, 0:nc]
    for r in range(1, r_sub):
        hid = hid + parts[r, :, r:r + nc]
    hid = hid + jnp.einsum('ld,lde->e', pe, w1)
    return jax.nn.silu(hid) @ w2


def compressed_kv(parts_k, parts_v, lp):
    kcc = rms_norm(compress(parts_k, lp['cmp_w1_k'], lp['cmp_w2_k'], lp['cmp_pe_k']), lp['nsa_k_norm_cmp'])
    vcc = compress(parts_v, lp['cmp_w1_v'], lp['cmp_w2_v'], lp['cmp_pe_v'])
    return kcc, vcc


def cmp_attention(q, q_pos, kc, vc):
    nc = kc.shape[1]
    end = jnp.arange(nc) * CMP_STRIDE + CMP_LEN - 1
    vis = end[None, :] <= q_pos[:, None]
    s = jnp.einsum('btghd,bcgd->btghc', q, kc) * ATTN_SCALE
    p = masked_softmax(s, vis[None, :, None, None, :], axis=-1)
    o = jnp.einsum('btghc,bcgd->btghd', p, vc)
    return o, jnp.sum(p, axis=3)


def select_blocks(imp, q_pos, n_keys):
    nc = imp.shape[-1]
    ns = -(-n_keys // SLC_BLK)
    cstart = jnp.arange(nc) * CMP_STRIDE
    sstart = jnp.arange(ns) * SLC_BLK
    overlap = (cstart[:, None] < sstart[None, :] + SLC_BLK) & (cstart[:, None] + CMP_LEN > sstart[None, :])
    score = imp @ overlap.astype(imp.dtype)
    cur = q_pos // SLC_BLK
    j = jnp.arange(ns)[None, :]
    valid = j <= cur[:, None]
    forced = (j == 0) | (j == cur[:, None]) | (j == cur[:, None] - 1)
    score = jnp.where(valid[None, :, None, :],
                      jnp.where(forced[None, :, None, :], FORCE_SCORE, score), -jnp.inf)
    vals, idx = lax.top_k(score, min(SLC_TOPN, ns))
    return idx, jnp.isfinite(vals)


def slc_attention(q, q_pos, idx, ok, kg, vg):
    s = jnp.einsum('btghd,btgnsd->btghns', q, kg) * ATTN_SCALE
    kpos = idx[..., None] * SLC_BLK + jnp.arange(SLC_BLK)
    mask = ok[..., None] & (kpos <= q_pos[None, :, None, None, None])
    p = masked_softmax(s, mask[:, :, :, None], axis=(-2, -1))
    return jnp.einsum('btghns,btgnsd->btghd', p, vg)


def slc_prompt(q, q_pos, idx, ok, ks, vs):
    b, s, g, hd = ks.shape
    ns = s // SLC_BLK
    kb = ks.reshape(b, ns, SLC_BLK, g, hd).transpose(0, 3, 1, 2, 4)
    vb = vs.reshape(b, ns, SLC_BLK, g, hd).transpose(0, 3, 1, 2, 4)
    bi = jnp.arange(b)[:, None, None, None]
    gi = jnp.arange(g)[None, None, :, None]
    nqb = s // Q_BLOCK

    def to_blocks(a):
        return jnp.moveaxis(a.reshape((a.shape[0], nqb, Q_BLOCK) + a.shape[2:]), 1, 0)

    def one_block(args):
        qb, pb, ib, ob = args
        return slc_attention(qb, pb, ib, ob, kb[bi, gi, ib], vb[bi, gi, ib])

    out = lax.map(one_block, (to_blocks(q), q_pos.reshape(nqb, Q_BLOCK), to_blocks(idx), to_blocks(ok)))
    return jnp.moveaxis(out, 0, 1).reshape(q.shape)


def slc_sample(q, q_pos, idx, ok, pool_k, pool_v, layer, page_table, k_new, v_new):
    db, t, g, hd = k_new.shape
    n_past = PAST_LEN // SLC_BLK
    bpp = PAGE_SIZE // SLC_BLK
    bi = jnp.arange(db)[:, None, None, None]
    gi = jnp.arange(g)[None, None, :, None]
    jc = jnp.minimum(idx, n_past - 1)
    page = page_table[bi, jc // bpp]
    row = (jc % bpp)[..., None] * SLC_BLK + jnp.arange(SLC_BLK)
    nnb = -(-t // SLC_BLK)
    jn = jnp.clip(idx - n_past, 0, nnb - 1)
    is_new = (idx >= n_past)[..., None, None]

    def gather(pool, new):
        from_pool = pool[layer, page[..., None], row, gi[..., None]]
        newb = jnp.pad(new, ((0, 0), (0, nnb * SLC_BLK - t), (0, 0), (0, 0)))
        newb = newb.reshape(db, nnb, SLC_BLK, g, hd).transpose(0, 3, 1, 2, 4)
        return jnp.where(is_new, newb[bi, gi, jn], from_pool)

    return slc_attention(q, q_pos, idx, ok, gather(pool_k, k_new), gather(pool_v, v_new))


def win_prompt(q, kw, vw):
    b, s, g, hpg, hd = q.shape
    nqb = s // Q_BLOCK
    nw = WINDOW // Q_BLOCK
    pad = ((0, 0), (WINDOW, 0), (0, 0), (0, 0))
    kb = jnp.pad(kw, pad).reshape(b, nw + nqb, Q_BLOCK, g, hd)
    vb = jnp.pad(vw, pad).reshape(b, nw + nqb, Q_BLOCK, g, hd)
    kband = jnp.concatenate([kb[:, j:j + nqb] for j in range(nw + 1)], axis=2)
    vband = jnp.concatenate([vb[:, j:j + nqb] for j in range(nw + 1)], axis=2)
    qb = q.reshape(b, nqb, Q_BLOCK, g, hpg, hd)
    sc = jnp.einsum('bnqghd,bnkgd->bnqghk', qb, kband) * ATTN_SCALE
    start = jnp.arange(nqb)[:, None] * Q_BLOCK
    qpos = start + jnp.arange(Q_BLOCK)
    kpos = start - WINDOW + jnp.arange((nw + 1) * Q_BLOCK)
    kp, qp = kpos[:, None, :], qpos[:, :, None]
    mask = (kp <= qp) & (kp > qp - WINDOW) & (kp >= 0)
    p = masked_softmax(sc, mask[None, :, :, None, None, :], axis=-1)
    o = jnp.einsum('bnqghk,bnkgd->bnqghd', p, vband)
    return o.reshape(b, s, g, hpg, hd)


def win_sample(q, q_pos, buf_k, buf_v, k_new, v_new):
    wb = buf_k.shape[1]
    t = k_new.shape[1]
    ka = jnp.concatenate([buf_k.astype(k_new.dtype), k_new], axis=1)
    va = jnp.concatenate([buf_v.astype(v_new.dtype), v_new], axis=1)
    kpos = PAST_LEN - wb + jnp.arange(wb + t)
    mask = (kpos[None, :] <= q_pos[:, None]) & (kpos[None, :] > q_pos[:, None] - WINDOW)
    sc = jnp.einsum('btghd,bkgd->btghk', q, ka) * ATTN_SCALE
    p = masked_softmax(sc, mask[None, :, None, None, :], axis=-1)
    o = jnp.einsum('btghk,bkgd->btghd', p, va)
    return o, ka[:, t:], va[:, t:]


def nsa_combine(gates, o_c, o_s, o_w):
    b, t, _ = gates.shape
    gt = jax.nn.sigmoid(gates.astype(jnp.float32)).reshape(b, t, 3, NSA_KV_HEADS, NSA_HPG, 1)
    o = gt[:, :, 0] * o_c + gt[:, :, 1] * o_s + gt[:, :, 2] * o_w
    return o.reshape(b, t, NSA_Q)


def merge_branches(x, o_a, o_b, ga, gb, w_o):
    m = (jax.nn.sigmoid(ga.astype(jnp.float32)) * o_a.astype(jnp.float32)
         + jax.nn.sigmoid(gb.astype(jnp.float32)) * o_b.astype(jnp.float32))
    return (x + m.astype(x.dtype) @ w_o).astype(x.dtype)


def gather_pages(pool, layer, page_table):
    rows = pool[layer, page_table]
    return rows.reshape(rows.shape[0], -1, rows.shape[3], rows.shape[4])


def _kv_cols(p, i):
    c0 = CB_KV * LANE + i * NSA_KV
    return p[..., c0:c0 + NSA_KV]


def token_mix_prompt(x, lp):
    b, s, _ = x.shape
    pos = jnp.arange(s)
    x2 = x.reshape(b * s, D_MODEL)
    p2 = mixer_projection(x2, lp['mix_norm'], lp['w_in'])
    p = p2.reshape(b, s, P_COLS)
    o_a, rec_new = gdn_prompt(p, lp)
    assert s >= GDN_CONV - 1
    conv_new = p[:, s - (GDN_CONV - 1):, :GDN_CONV_DIM]
    heads = lambda a_: a_.reshape(b, -1, NSA_KV_HEADS, NSA_HD)
    qn, qr, ks_r, kw_r = nsa_prep(p, pos, lp)
    kcc, vcc, nc = compressed_kv_prompt(p, lp)
    o_c, sel = cmp_select_prompt(qn, kcc, vcc, nc)
    vs, vw = _kv_cols(p, 3), _kv_cols(p, 5)
    o_s = slc_attention_prompt(qr, sel, ks_r.astype(BF16), vs.astype(BF16))
    o_w = win_attention_prompt(qr, kw_r.astype(BF16), vw.astype(BF16))
    flat = lambda a_: a_.reshape(b * s, -1)
    y = merge_branches_fused(x2, p2, flat(o_a), flat(o_c), flat(o_s), flat(o_w), lp['w_o']).reshape(b, s, D_MODEL)
    wb = min(WINDOW, s)
    return y, (conv_new, rec_new, heads(_kv_cols(p, 0)), heads(_kv_cols(p, 1)), heads(ks_r), heads(vs),
               heads(kw_r[:, s - wb:]), heads(vw[:, s - wb:]))


def token_mix_sample(x, layer, conv_buf, rec, cache_k_cmp, cache_v_cmp, cache_k_slc, cache_v_slc,
                     win_k, win_v, page_table, lp):
    db, t, _ = x.shape
    assert t == 1 and t < CMP_STRIDE
    q_pos = PAST_LEN
    x2 = x.reshape(db, D_MODEL)
    p2 = mixer_projection(x2, lp['mix_norm'], lp['w_in'])
    p = p2.reshape(db, 1, P_COLS)
    o_a, conv_new, rec_new = gdn_sample(p, conv_buf, rec, lp)
    heads = lambda a_: a_.reshape(db, -1, NSA_KV_HEADS, NSA_HD)
    qn, qr, ks_r, kw_r = [a_.reshape(db, 1, -1) for a_ in
                          nsa_prep(p2.reshape(1, db, P_COLS), jnp.full((db,), q_pos), lp)]
    (o_c, imp), nc = cmp_attention_sample(qn, cache_k_cmp[layer], cache_v_cmp[layer], page_table, lp, q_pos)
    idx = select_blocks_sample(imp, nc, q_pos, PAST_LEN + t)
    vs, vw = _kv_cols(p, 3), _kv_cols(p, 5)
    o_s = slc_attention_sample(qr, idx, cache_k_slc[layer], cache_v_slc[layer], page_table, ks_r, vs)
    wb = win_k.shape[1]
    o_w, win_k_new, win_v_new = win_attention_sample(
        qr, win_k.reshape(db, wb * NSA_KV_HEADS, NSA_HD), win_v.reshape(db, wb * NSA_KV_HEADS, NSA_HD),
        kw_r, vw, q_pos)
    flat = lambda a_: a_.reshape(db, -1)
    y = merge_branches_fused(x2, p2, flat(o_a), flat(o_c), flat(o_s), flat(o_w), lp['w_o']).reshape(db, 1, D_MODEL)
    return y, (conv_new, rec_new, heads(_kv_cols(p, 0)), heads(_kv_cols(p, 1)), heads(ks_r), heads(vs),
               heads(win_k_new), heads(win_v_new))


def kernel(x_prompt, x_sample, state_gdn_conv, state_gdn_rec, cache_k_cmp, cache_v_cmp, cache_k_slc, cache_v_slc, cache_k_win, cache_v_win, page_table, ffn1_norm, ffn1_w_in, ffn1_w_out, mix_norm, w_in, gdn_conv_w, gdn_a_log, gdn_dt_bias, gdn_out_norm, nsa_q_norm, nsa_k_norm_cmp, nsa_k_norm_slc, nsa_k_norm_win, cmp_w1_k, cmp_w2_k, cmp_pe_k, cmp_w1_v, cmp_w2_v, cmp_pe_v, w_o, ffn2_norm, ffn2_w_in, ffn2_w_out):
    depth = w_in.shape[0]
    yp, ys = x_prompt, x_sample
    p_list, s_list = [], []
    for l in range(depth):
        lp = {'mix_norm': mix_norm[l], 'w_in': w_in[l], 'gdn_conv_w': gdn_conv_w[l],
              'gdn_a_log': gdn_a_log[l], 'gdn_dt_bias': gdn_dt_bias[l], 'gdn_out_norm': gdn_out_norm[l],
              'nsa_q_norm': nsa_q_norm[l], 'nsa_k_norm_cmp': nsa_k_norm_cmp[l],
              'nsa_k_norm_slc': nsa_k_norm_slc[l], 'nsa_k_norm_win': nsa_k_norm_win[l],
              'cmp_w1_k': cmp_w1_k[l], 'cmp_w2_k': cmp_w2_k[l], 'cmp_pe_k': cmp_pe_k[l],
              'cmp_w1_v': cmp_w1_v[l], 'cmp_w2_v': cmp_w2_v[l], 'cmp_pe_v': cmp_pe_v[l], 'w_o': w_o[l]}
        yp = swiglu_half(yp, ffn1_norm[l], ffn1_w_in[l], ffn1_w_out[l])
        yp, st_p = token_mix_prompt(yp, lp)
        yp = swiglu_half(yp, ffn2_norm[l], ffn2_w_in[l], ffn2_w_out[l])
        ys = swiglu_half(ys, ffn1_norm[l], ffn1_w_in[l], ffn1_w_out[l])
        ys, st_s = token_mix_sample(ys, l, state_gdn_conv[l], state_gdn_rec[l], cache_k_cmp, cache_v_cmp,
                                    cache_k_slc, cache_v_slc, cache_k_win[l], cache_v_win[l], page_table, lp)
        ys = swiglu_half(ys, ffn2_norm[l], ffn2_w_in[l], ffn2_w_out[l])
        p_list.append(st_p)
        s_list.append(st_s)
    (p_conv, p_rec, p_kc, p_vc, p_ks, p_vs, p_kw, p_vw) = [jnp.stack(a) for a in zip(*p_list)]
    (s_conv, s_rec, s_kc, s_vc, s_ks, s_vs, s_kw, s_vw) = [jnp.stack(a) for a in zip(*s_list)]
    return (yp, ys, p_conv, p_rec, p_kc, p_vc, p_ks, p_vs, p_kw, p_vw,
            s_conv, s_rec, s_kc, s_vc, s_ks, s_vs, s_kw, s_vw)
```

```python
import functools
import math

import numpy as np
import jax
import jax.numpy as jnp
from jax import lax
from jax.experimental import pallas as pl
from jax.experimental.pallas import tpu as pltpu

D_MODEL = 1024
PAST_LEN = 8192
PAGE_SIZE = 128

GDN_DK = 128
GDN_DV = 128
GDN_HEADS = D_MODEL // GDN_DV
GDN_QK = GDN_HEADS * GDN_DK
GDN_V = GDN_HEADS * GDN_DV
GDN_CONV_DIM = 2 * GDN_QK + GDN_V
GDN_CONV = 4
GDN_CHUNK = 64

NSA_HD = 128
NSA_HEADS = D_MODEL // NSA_HD
NSA_KV_HEADS = NSA_HEADS // 4
NSA_HPG = NSA_HEADS // NSA_KV_HEADS
NSA_Q = NSA_HEADS * NSA_HD
NSA_KV = NSA_KV_HEADS * NSA_HD
CMP_LEN = 32
CMP_STRIDE = 16
SLC_BLK = 64
SLC_TOPN = 16
WINDOW = 512
Q_BLOCK = 128
ATTN_SCALE = NSA_HD ** -0.5
FORCE_SCORE = 1e6

ROPE_DIM = NSA_HD // 4
ROPE_THETA = 500000.0

D_FF = ((8 * D_MODEL // 3 + 127) // 128) * 128
EPS = 1e-6

IN_SIZES = (GDN_CONV_DIM, GDN_HEADS, GDN_HEADS, GDN_V,
            NSA_Q, NSA_KV, NSA_KV, NSA_KV, NSA_KV, NSA_KV, NSA_KV, 3 * NSA_HEADS,
            D_MODEL, D_MODEL)

MXU_N = 256
VMEM_LIMIT_BYTES = 56 * 1024 * 1024

BF16 = jnp.bfloat16
F32 = jnp.float32


def _const_spec(shape):
    return pl.BlockSpec(shape, lambda *_: (0,) * len(shape), pipeline_mode=pl.Buffered(1))


def _ffn_body(x_ref, g_ref, wg_ref, wu_ref, wo_ref, o_ref):
    x = x_ref[...]
    ms = jnp.mean(x * x, axis=-1, keepdims=True)
    h = (x * lax.rsqrt(ms + EPS) * g_ref[...]).astype(BF16)
    acc = jnp.zeros(x.shape, F32)
    for c in range(D_FF // MXU_N):
        cols = slice(c * MXU_N, (c + 1) * MXU_N)
        gate = jnp.dot(h, wg_ref[:, cols], preferred_element_type=F32)
        up = jnp.dot(h, wu_ref[:, cols], preferred_element_type=F32)
        act = (gate * jax.nn.sigmoid(gate) * up).astype(BF16)
        acc = acc + jnp.dot(act, wo_ref[cols, :], preferred_element_type=F32)
    o_ref[...] = x + 0.5 * acc


def _ffn_tile(n_rows):
    return min(n_rows, 512)


def swiglu_half(x, g, w_in, w_out):
    shape = x.shape
    x2 = x.reshape(-1, D_MODEL)
    n = x2.shape[0]
    tm = _ffn_tile(n)
    wg = w_in[:, :D_FF].astype(BF16)
    wu = w_in[:, D_FF:].astype(BF16)
    wo = w_out.astype(BF16)
    out = pl.pallas_call(
        _ffn_body,
        grid=(n // tm,),
        in_specs=[pl.BlockSpec((tm, D_MODEL), lambda i: (i, 0)),
                  _const_spec((1, D_MODEL)),
                  _const_spec((D_MODEL, D_FF)),
                  _const_spec((D_MODEL, D_FF)),
                  _const_spec((D_FF, D_MODEL))],
        out_specs=pl.BlockSpec((tm, D_MODEL), lambda i: (i, 0)),
        out_shape=jax.ShapeDtypeStruct((n, D_MODEL), F32),
        compiler_params=pltpu.CompilerParams(dimension_semantics=("arbitrary",),
                                             vmem_limit_bytes=VMEM_LIMIT_BYTES),
        name="swiglu_half",
    )(x2, g.reshape(1, D_MODEL), wg, wu, wo)
    return out.reshape(shape)


LANE = 128
CB_QKV = 0
CB_Z = CB_QKV + GDN_CONV_DIM // LANE
CB_Q = CB_Z + GDN_V // LANE
CB_GA = CB_Q + NSA_Q // LANE
CB_GB = CB_GA + D_MODEL // LANE
CB_KV = CB_GB + D_MODEL // LANE
CB_SMALL = CB_KV + 6 * NSA_KV // LANE
P_COLS = (CB_SMALL + 1) * LANE
SMALL_A, SMALL_B, SMALL_G = 0, GDN_HEADS, 2 * GDN_HEADS
PROJ_CHUNKS = 3
assert (P_COLS // LANE) % PROJ_CHUNKS == 0 and 2 * GDN_HEADS + 3 * NSA_HEADS <= LANE


def _permuted_w_in(w_in):
    offs = [int(o) for o in np.cumsum((0,) + IN_SIZES)]
    seg = lambda i: w_in[:, offs[i]:offs[i + 1]]
    w = jnp.concatenate([seg(0), seg(3), seg(4), seg(12), seg(13)] + [seg(i) for i in range(5, 11)]
                        + [seg(1), seg(2), seg(11)], axis=1)
    return jnp.pad(w, ((0, 0), (0, P_COLS - w.shape[1]))).astype(BF16)


def _proj_body(x_ref, g_ref, w_ref, o_ref):
    x = x_ref[...]
    ms = jnp.mean(x * x, axis=-1, keepdims=True)
    h = (x * lax.rsqrt(ms + EPS) * g_ref[...]).astype(BF16)
    o_ref[...] = jnp.dot(h, w_ref[...], preferred_element_type=F32)


def mixer_projection(x2, g, w_in):
    n = x2.shape[0]
    tm = min(n, 512)
    wc = P_COLS // PROJ_CHUNKS
    return pl.pallas_call(
        _proj_body,
        grid=(PROJ_CHUNKS, n // tm),
        in_specs=[pl.BlockSpec((tm, D_MODEL), lambda c, i: (i, 0)),
                  pl.BlockSpec((1, D_MODEL), lambda c, i: (0, 0)),
                  pl.BlockSpec((D_MODEL, wc), lambda c, i: (0, c))],
        out_specs=pl.BlockSpec((tm, wc), lambda c, i: (i, c)),
        out_shape=jax.ShapeDtypeStruct((n, P_COLS), F32),
        compiler_params=pltpu.CompilerParams(dimension_semantics=("arbitrary", "arbitrary"),
                                             vmem_limit_bytes=VMEM_LIMIT_BYTES),
        name="mixer_projection",
    )(x2, g.reshape(1, D_MODEL), _permuted_w_in(w_in))


GDN_TILE = 128
GDN_HB = 4
GDN_SPLIT_LEVELS = 3
TN_DIMS = (((0,), (0,)), ((), ()))


def _split3(x):
    p1 = x.astype(BF16)
    r1 = x - p1.astype(F32)
    p2 = r1.astype(BF16)
    p3 = (r1 - p2.astype(F32)).astype(BF16)
    return p1, p2, p3


def _gdn_gates(small, alog_row, dt_row):
    xa = small + dt_row
    softplus = jnp.maximum(xa, 0.0) + jnp.log(1.0 + jnp.exp(-jnp.abs(xa)))
    return -jnp.exp(alog_row) * softplus, jax.nn.sigmoid(small)


def _lane_col(x, lane_idx):
    lane = lax.broadcasted_iota(jnp.int32, x.shape, 1)
    return jnp.sum(jnp.where(lane == lane_idx, x, 0.0), axis=1, keepdims=True)


def _gdn_prompt_body(q_ref, k_ref, v_ref, z_ref, sm_ref, wq_ref, wk_ref, wv_ref, alog_ref, dt_ref, gn_ref,
                     o_ref, rec_ref, s_ref, pq_ref, pk_ref, pv_ref, *, tile, hb):
    c = GDN_CHUNK
    hblk, step = pl.program_id(1), pl.program_id(2)
    width = hb * GDN_DK

    @pl.when(step == 0)
    def _():
        s_ref[...] = jnp.zeros(s_ref.shape, F32)
        for r in (pq_ref, pk_ref, pv_ref):
            r[...] = jnp.zeros(r.shape, F32)

    row8 = lax.broadcasted_iota(jnp.int32, (8, width), 0)

    def conv(x_ref, prev_ref, w_ref):
        x, prev, w = x_ref[0], prev_ref[...], w_ref[...]
        acc = x * w[GDN_CONV - 1:GDN_CONV]
        for j in range(1, GDN_CONV):
            main = pltpu.roll(x, j, 0)
            first = jnp.where(row8 < j, pltpu.roll(prev, j, 0), main[:8])
            acc = acc + jnp.concatenate([first, main[8:]], axis=0) * w[GDN_CONV - 1 - j:GDN_CONV - j]
        prev_ref[...] = x[tile - 8:]
        return acc * jax.nn.sigmoid(acc)

    qc, kc, vc = conv(q_ref, pq_ref, wq_ref), conv(k_ref, pk_ref, wk_ref), conv(v_ref, pv_ref, wv_ref)
    z = z_ref[0]
    g_all, beta_all = _gdn_gates(sm_ref[0], alog_ref[...], dt_ref[...])

    n = hb * c
    ii = lax.broadcasted_iota(jnp.int32, (n, n), 0)
    jj = lax.broadcasted_iota(jnp.int32, (n, n), 1)
    same_head = (ii // c) == (jj // c)
    incl, strict = same_head & (ii >= jj), same_head & (ii > jj)
    ci_ = lax.broadcasted_iota(jnp.int32, (c, c), 0)
    cj_ = lax.broadcasted_iota(jnp.int32, (c, c), 1)
    tri = jnp.where(ci_ >= cj_, 1.0, 0.0).astype(BF16)
    er = lax.broadcasted_iota(jnp.int32, (8, LANE), 0)
    el = lax.broadcasted_iota(jnp.int32, (8, LANE), 1)
    head_rows = jnp.where((el == SMALL_A + hblk * hb + er) & (er < hb), 1.0, 0.0).astype(BF16)
    dot = functools.partial(jnp.dot, preferred_element_type=F32)
    stack = lambda f: jnp.concatenate([f(hh) for hh in range(hb)], axis=0)
    states = [s_ref[hh] for hh in range(hb)]

    for ci in range(tile // c):
        rows = slice(ci * c, (ci + 1) * c)
        gc_all = sum(dot(tri, piece) for piece in _split3(g_all[rows]))
        gc_rows = sum(lax.dot_general(head_rows, piece, NT_DIMS, preferred_element_type=F32)
                      for piece in _split3(gc_all))
        head_cols = lambda x: stack(lambda hh: x[rows, hh * GDN_DK:(hh + 1) * GDN_DK])
        gcol = stack(lambda hh: _lane_col(gc_all, SMALL_A + hblk * hb + hh))
        bcol = stack(lambda hh: _lane_col(beta_all[rows], SMALL_B + hblk * hb + hh))
        grow = jnp.concatenate([gc_rows[hh:hh + 1] for hh in range(hb)], axis=1)
        g_last = stack(lambda hh: jnp.broadcast_to(gcol[hh * c + c - 1:hh * c + c], (c, 1)))
        q, k, v = head_cols(qc), head_cols(kc), head_cols(vc)
        q = q * lax.rsqrt(jnp.sum(q * q, axis=-1, keepdims=True) + EPS) * (GDN_DK ** -0.5)
        k = k * lax.rsqrt(jnp.sum(k * k, axis=-1, keepdims=True) + EPS)
        decay = jnp.where(incl, jnp.exp(jnp.where(incl, gcol - grow, 0.0)), 0.0)
        kb = k * bcol
        k16 = k.astype(BF16)
        m = jnp.where(strict, lax.dot_general(kb.astype(BF16), k16, NT_DIMS, preferred_element_type=F32) * decay, 0.0)
        egc = jnp.exp(gcol)
        rhs = jnp.concatenate([v * bcol, kb * egc], axis=1)
        mp = m
        sol = rhs
        for level in range(int(math.log2(c))):
            a1, a2, _ = _split3(mp)
            s1, s2, _ = _split3(sol)
            step_term = dot(a1, s1)
            if level < GDN_SPLIT_LEVELS:
                step_term = step_term + dot(a1, s2) + dot(a2, s1)
            sol = sol - step_term if level == 0 else sol + step_term
            if level + 1 < int(math.log2(c)):
                mp = dot(a1, a1)
        u, w = sol[:, :GDN_DV], sol[:, GDN_DV:]
        attn = jnp.where(incl, lax.dot_general(q.astype(BF16), k16, NT_DIMS, preferred_element_type=F32) * decay, 0.0)
        qe = q * egc
        k_dec = k * jnp.exp(g_last - gcol)
        s_decay = jnp.exp(g_last)
        v_new, qs = [], []
        for hh in range(hb):
            r = slice(hh * c, (hh + 1) * c)
            s16 = states[hh].astype(BF16)
            ws_qs = dot(jnp.concatenate([w[r], qe[r]], axis=0).astype(BF16), s16)
            v_new.append(u[r] - ws_qs[:c])
            qs.append(ws_qs[c:])
            states[hh] = states[hh] * s_decay[hh * c:hh * c + 1] + lax.dot_general(
                k_dec[r].astype(BF16), v_new[hh].astype(BF16), TN_DIMS, preferred_element_type=F32)
        o = jnp.concatenate(qs, axis=0) + dot(attn.astype(BF16), jnp.concatenate(v_new, axis=0).astype(BF16))
        for hh in range(hb):
            cols = slice(hh * GDN_DK, (hh + 1) * GDN_DK)
            zc = z[rows, cols]
            o_ref[0, rows, cols] = _head_norm(o[hh * c:(hh + 1) * c], gn_ref[...]) * (zc * jax.nn.sigmoid(zc))

    for hh in range(hb):
        s_ref[hh] = states[hh]

    @pl.when(step == pl.num_programs(2) - 1)
    def _():
        rec_ref[0] = s_ref[...]


def _lane_row(vec, offset):
    return jnp.pad(vec.astype(F32), (offset, LANE - offset - vec.shape[0])).reshape(1, LANE)


def gdn_prompt(p, lp):
    b, s, _ = p.shape
    tile, hb = GDN_TILE, GDN_HB
    assert s % tile == 0 and tile % GDN_CHUNK == 0 and GDN_HEADS % hb == 0 and GDN_CHUNK & (GDN_CHUNK - 1) == 0
    width = hb * GDN_DK
    wb = width // LANE
    col = lambda cb0: pl.BlockSpec((1, tile, width), lambda bi, h, i: (bi, i, cb0 // wb + h))
    wcol = lambda cb0: pl.BlockSpec((GDN_CONV, width), lambda bi, h, i: (0, cb0 // wb + h))
    vec = pl.BlockSpec((1, LANE), lambda bi, h, i: (0, 0))
    qk_blocks = GDN_QK // LANE
    return pl.pallas_call(
        functools.partial(_gdn_prompt_body, tile=tile, hb=hb),
        grid=(b, GDN_HEADS // hb, s // tile),
        in_specs=[col(CB_QKV), col(CB_QKV + qk_blocks), col(CB_QKV + 2 * qk_blocks), col(CB_Z),
                  pl.BlockSpec((1, tile, LANE), lambda bi, h, i: (bi, i, CB_SMALL)),
                  wcol(0), wcol(qk_blocks), wcol(2 * qk_blocks), vec, vec, vec],
        out_specs=[pl.BlockSpec((1, tile, width), lambda bi, h, i: (bi, i, h)),
                   pl.BlockSpec((1, hb, GDN_DK, GDN_DV), lambda bi, h, i: (bi, h, 0, 0))],
        out_shape=[jax.ShapeDtypeStruct((b, s, GDN_V), F32),
                   jax.ShapeDtypeStruct((b, GDN_HEADS, GDN_DK, GDN_DV), F32)],
        scratch_shapes=[pltpu.VMEM((hb, GDN_DK, GDN_DV), F32)] + [pltpu.VMEM((8, width), F32)] * 3,
        compiler_params=pltpu.CompilerParams(dimension_semantics=("arbitrary", "arbitrary", "arbitrary"),
                                             vmem_limit_bytes=VMEM_LIMIT_BYTES),
        name="gdn_prompt",
    )(p, p, p, p, p, lp['gdn_conv_w'], lp['gdn_conv_w'], lp['gdn_conv_w'],
      _lane_row(lp['gdn_a_log'], SMALL_A), _lane_row(lp['gdn_dt_bias'], SMALL_A),
      lp['gdn_out_norm'].reshape(1, GDN_DV))


NEG_INF = float("-inf")
NEG_BIG = -1e30
NT_DIMS = (((1,), (1,)), ((), ()))


def _rope_tables(pos):
    half = ROPE_DIM // 2
    inv = jnp.float32(ROPE_THETA) ** (-jnp.arange(half, dtype=jnp.float32) / half)
    ang = pos.astype(jnp.float32)[:, None] * inv
    cos, sin = jnp.cos(ang), jnp.sin(ang)
    rest = NSA_HD - ROPE_DIM
    cos_t = jnp.concatenate([cos, cos, jnp.ones((pos.shape[0], rest), F32)], axis=-1)
    sin_t = jnp.concatenate([-sin, sin, jnp.zeros((pos.shape[0], rest), F32)], axis=-1)
    return cos_t, sin_t


def _rope(y, cos_t, sin_t):
    half = ROPE_DIM // 2
    lane = lax.broadcasted_iota(jnp.int32, y.shape, 1)
    partner = jnp.where(lane < half, pltpu.roll(y, NSA_HD - half, 1), pltpu.roll(y, half, 1))
    return y * cos_t + partner * sin_t


def _head_norm(x, g):
    ms = jnp.mean(x * x, axis=-1, keepdims=True)
    return x * lax.rsqrt(ms + EPS) * g


def _nsa_prep_body(q_ref, ks_ref, kw_ref, cos_ref, sin_ref, gq_ref, gks_ref, gkw_ref,
                   qn_ref, qr_ref, ksr_ref, kwr_ref):
    cos_t, sin_t = cos_ref[...], sin_ref[...]
    for h in range(NSA_HEADS):
        cols = slice(h * NSA_HD, (h + 1) * NSA_HD)
        y = _head_norm(q_ref[0, :, cols], gq_ref[...])
        qn_ref[0, :, cols] = (y * ATTN_SCALE).astype(BF16)
        qr_ref[0, :, cols] = (_rope(y, cos_t, sin_t) * ATTN_SCALE).astype(BF16)
    for h in range(NSA_KV_HEADS):
        cols = slice(h * NSA_HD, (h + 1) * NSA_HD)
        ksr_ref[0, :, cols] = _rope(_head_norm(ks_ref[0, :, cols], gks_ref[...]), cos_t, sin_t)
        kwr_ref[0, :, cols] = _rope(_head_norm(kw_ref[0, :, cols], gkw_ref[...]), cos_t, sin_t)


def _col_spec(rows, width, col_block):
    assert (col_block * LANE) % width == 0
    cb = col_block * LANE // width
    return pl.BlockSpec((1, rows, width), lambda bi, i: (bi, i, cb))


def nsa_prep(p, pos, lp):
    b, t, _ = p.shape
    tp = min(t, 512)
    cos_t, sin_t = _rope_tables(pos)
    row = lambda w: pl.BlockSpec((1, tp, w), lambda bi, i: (bi, i, 0))
    kv_w = NSA_KV // LANE
    tab = pl.BlockSpec((tp, NSA_HD), lambda bi, i: (i, 0))
    gain = pl.BlockSpec((1, NSA_HD), lambda bi, i: (0, 0))
    return pl.pallas_call(
        _nsa_prep_body,
        grid=(b, t // tp),
        in_specs=[_col_spec(tp, NSA_Q, CB_Q), _col_spec(tp, NSA_KV, CB_KV + 2 * kv_w),
                  _col_spec(tp, NSA_KV, CB_KV + 4 * kv_w), tab, tab, gain, gain, gain],
        out_specs=[row(NSA_Q), row(NSA_Q), row(NSA_KV), row(NSA_KV)],
        out_shape=[jax.ShapeDtypeStruct((b, t, NSA_Q), BF16), jax.ShapeDtypeStruct((b, t, NSA_Q), BF16),
                   jax.ShapeDtypeStruct((b, t, NSA_KV), F32), jax.ShapeDtypeStruct((b, t, NSA_KV), F32)],
        compiler_params=pltpu.CompilerParams(dimension_semantics=("arbitrary", "arbitrary"),
                                             vmem_limit_bytes=VMEM_LIMIT_BYTES),
        name="nsa_prep",
    )(p, p, p, cos_t, sin_t, lp['nsa_q_norm'].reshape(1, NSA_HD),
      lp['nsa_k_norm_slc'].reshape(1, NSA_HD), lp['nsa_k_norm_win'].reshape(1, NSA_HD))


def _group_queries(q, g):
    return jnp.concatenate(
        [q[:, (g * NSA_HPG + h) * NSA_HD:(g * NSA_HPG + h + 1) * NSA_HD] for h in range(NSA_HPG)], axis=0)


def _store_group(o_ref, o, g, tq):
    for h in range(NSA_HPG):
        c0 = (g * NSA_HPG + h) * NSA_HD
        o_ref[0, :, c0:c0 + NSA_HD] = o[h * tq:(h + 1) * tq]


def _cmp_select_body(q_ref, kcc_ref, vcc_ref, ovl_ref, oc_ref, sel_ref, *, tq, nc, ns, selw):
    ncp = kcc_ref.shape[2]
    q0 = pl.program_id(1) * tq
    q = q_ref[0]
    tpos = q0 + lax.broadcasted_iota(jnp.int32, (tq, ncp), 0)
    cidx = lax.broadcasted_iota(jnp.int32, (tq, ncp), 1)
    vis = (cidx * CMP_STRIDE + (CMP_LEN - 1) <= tpos) & (cidx < nc)
    jrow = lax.broadcasted_iota(jnp.int32, (ns, tq), 0)
    cur = (q0 + lax.broadcasted_iota(jnp.int32, (ns, tq), 1)) // SLC_BLK
    valid = jrow <= cur
    forced = (jrow == 0) | (jrow == cur) | (jrow == cur - 1)
    sel_parts = []
    for g in range(NSA_KV_HEADS):
        s = lax.dot_general(_group_queries(q, g), kcc_ref[0, g], NT_DIMS, preferred_element_type=F32)
        sm = jnp.where(vis[None], s.reshape(NSA_HPG, tq, ncp), NEG_INF)
        m = jnp.max(sm, axis=-1, keepdims=True)
        m = jnp.where(m > NEG_INF, m, 0.0)
        p = jnp.exp(sm - m)
        p = p / jnp.maximum(jnp.sum(p, axis=-1, keepdims=True), 1e-30)
        o = jnp.dot(p.reshape(NSA_HPG * tq, ncp).astype(BF16), vcc_ref[0, g], preferred_element_type=F32)
        _store_group(oc_ref, o, g, tq)
        imp = p[0]
        for h in range(1, NSA_HPG):
            imp = imp + p[h]
        hi = imp.astype(BF16)
        lo = (imp - hi.astype(F32)).astype(BF16)
        score = (lax.dot_general(ovl_ref[...], hi, NT_DIMS, preferred_element_type=F32)
                 + lax.dot_general(ovl_ref[...], lo, NT_DIMS, preferred_element_type=F32))
        work = jnp.where(valid, jnp.where(forced, FORCE_SCORE, score), NEG_INF)
        chosen = jnp.zeros((ns, tq), F32)
        for _ in range(min(SLC_TOPN, ns)):
            mx = jnp.max(work, axis=0, keepdims=True)
            first = jnp.min(jnp.where(work == mx, jrow, ns), axis=0, keepdims=True)
            hit = jrow == first
            chosen = jnp.where(hit & (mx > NEG_INF), 1.0, chosen)
            work = jnp.where(hit, NEG_INF, work)
        sel_parts.append(chosen)
    if selw > NSA_KV_HEADS * ns:
        sel_parts.append(jnp.zeros((selw - NSA_KV_HEADS * ns, tq), F32))
    sel_ref[0] = jnp.concatenate(sel_parts, axis=0).T.astype(BF16)


def _sel_width(ns):
    return -(-NSA_KV_HEADS * ns // 128) * 128


def cmp_select_prompt(qn, kcc, vcc, nc):
    b, s, _ = qn.shape
    ncp = kcc.shape[2]
    assert ncp % LANE == 0
    ns = -(-s // SLC_BLK)
    selw = _sel_width(ns)
    tq = min(s, 256)
    cstart = np.arange(ncp) * CMP_STRIDE
    sstart = np.arange(ns) * SLC_BLK
    ovl = ((cstart[None, :] < sstart[:, None] + SLC_BLK) & (cstart[None, :] + CMP_LEN > sstart[:, None])
           & (np.arange(ncp)[None, :] < nc))
    ovl_t = jnp.asarray(ovl, dtype=BF16)
    kv_spec = pl.BlockSpec((1, NSA_KV_HEADS, ncp, NSA_HD), lambda bi, i: (bi, 0, 0, 0))
    return pl.pallas_call(
        functools.partial(_cmp_select_body, tq=tq, nc=nc, ns=ns, selw=selw),
        grid=(b, s // tq),
        in_specs=[pl.BlockSpec((1, tq, NSA_Q), lambda bi, i: (bi, i, 0)), kv_spec, kv_spec,
                  pl.BlockSpec((ns, ncp), lambda bi, i: (0, 0))],
        out_specs=[pl.BlockSpec((1, tq, NSA_Q), lambda bi, i: (bi, i, 0)),
                   pl.BlockSpec((1, tq, selw), lambda bi, i: (bi, i, 0))],
        out_shape=[jax.ShapeDtypeStruct((b, s, NSA_Q), F32), jax.ShapeDtypeStruct((b, s, selw), BF16)],
        compiler_params=pltpu.CompilerParams(dimension_semantics=("arbitrary", "arbitrary"),
                                             vmem_limit_bytes=VMEM_LIMIT_BYTES),
        name="cmp_select_prompt",
    )(qn, kcc, vcc, ovl_t)


SLC_TQ, SLC_TK = 256, 512


def _slc_body(q_ref, sel_ref, k_ref, v_ref, o_ref, *, tq, tk, ns):
    selw = sel_ref.shape[2]
    q0 = pl.program_id(1) * tq
    n_kv = (q0 + tq + tk - 1) // tk
    q = q_ref[0]
    selm = sel_ref[0]
    qpos = q0 + lax.broadcasted_iota(jnp.int32, (tq, tk), 0)
    kio = lax.broadcasted_iota(jnp.int32, (tq, tk), 1)
    erow = lax.broadcasted_iota(jnp.int32, (selw, tk), 0)
    ecol = lax.broadcasted_iota(jnp.int32, (selw, tk), 1)
    for g in range(NSA_KV_HEADS):
        gcols = slice(g * NSA_HD, (g + 1) * NSA_HD)
        q_heads = [q[:, (g * NSA_HPG + h) * NSA_HD:(g * NSA_HPG + h + 1) * NSA_HD] for h in range(NSA_HPG)]

        def body(j, carry, q_heads=q_heads, gcols=gcols, g=g):
            k0 = pl.multiple_of(j * tk, tk)
            kt = k_ref[0, pl.ds(k0, tk), gcols]
            vt = v_ref[0, pl.ds(k0, tk), gcols]
            expand = jnp.where(erow - g * ns == (ecol + k0) // SLC_BLK, 1.0, 0.0).astype(BF16)
            chosen = jnp.dot(selm, expand, preferred_element_type=F32)
            bias = jnp.where((chosen > 0.5) & (kio + k0 <= qpos), 0.0, NEG_BIG)
            out = []
            for h in range(NSA_HPG):
                m, l, acc = carry[h]
                s = lax.dot_general(q_heads[h], kt, NT_DIMS, preferred_element_type=F32) + bias
                m_new = jnp.maximum(m, jnp.max(s, axis=-1, keepdims=True))
                p = jnp.exp(s - m_new)
                alpha = jnp.exp(m - m_new)
                l = alpha * l + jnp.sum(p, axis=-1, keepdims=True)
                acc = alpha * acc + jnp.dot(p.astype(BF16), vt, preferred_element_type=F32)
                out.append((m_new, l, acc))
            return tuple(out)

        init = tuple((jnp.full((tq, 1), 0.5 * NEG_BIG, F32), jnp.zeros((tq, 1), F32), jnp.zeros((tq, NSA_HD), F32))
                     for _ in range(NSA_HPG))
        final = lax.fori_loop(0, n_kv, body, init)
        for h in range(NSA_HPG):
            _, l, acc = final[h]
            c0 = (g * NSA_HPG + h) * NSA_HD
            o_ref[0, :, c0:c0 + NSA_HD] = acc / jnp.maximum(l, 1e-30)


def slc_attention_prompt(qr, sel, k, v):
    b, s, _ = qr.shape
    selw = sel.shape[2]
    ns = -(-s // SLC_BLK)
    tq = min(s, SLC_TQ)
    tk = min(s, SLC_TK)
    kv_spec = pl.BlockSpec((1, s, NSA_KV), lambda bi, i: (bi, 0, 0))
    return pl.pallas_call(
        functools.partial(_slc_body, tq=tq, tk=tk, ns=ns),
        grid=(b, s // tq),
        in_specs=[pl.BlockSpec((1, tq, NSA_Q), lambda bi, i: (bi, i, 0)),
                  pl.BlockSpec((1, tq, selw), lambda bi, i: (bi, i, 0)), kv_spec, kv_spec],
        out_specs=pl.BlockSpec((1, tq, NSA_Q), lambda bi, i: (bi, i, 0)),
        out_shape=jax.ShapeDtypeStruct((b, s, NSA_Q), F32),
        compiler_params=pltpu.CompilerParams(dimension_semantics=("arbitrary", "arbitrary"),
                                             vmem_limit_bytes=VMEM_LIMIT_BYTES),
        name="slc_attention_prompt",
    )(qr, sel, k, v)


def _win_body(q_ref, k_ref, v_ref, o_ref, *, tq, span):
    q0 = pl.program_id(1) * tq
    kstart = pl.multiple_of(jnp.maximum(q0 - WINDOW, 0), tq)
    q = q_ref[0]
    qpos = q0 + lax.broadcasted_iota(jnp.int32, (tq, span), 0)
    kpos = kstart + lax.broadcasted_iota(jnp.int32, (tq, span), 1)
    allowed = ((kpos <= qpos) & (kpos > qpos - WINDOW))[None]
    for g in range(NSA_KV_HEADS):
        gcols = slice(g * NSA_HD, (g + 1) * NSA_HD)
        kt = k_ref[0, pl.ds(kstart, span), gcols]
        vt = v_ref[0, pl.ds(kstart, span), gcols]
        s = lax.dot_general(_group_queries(q, g), kt, NT_DIMS, preferred_element_type=F32)
        sm = jnp.where(allowed, s.reshape(NSA_HPG, tq, span), NEG_INF)
        p = jnp.exp(sm - jnp.max(sm, axis=-1, keepdims=True))
        l = jnp.sum(p, axis=-1, keepdims=True)
        o = jnp.dot(p.reshape(NSA_HPG * tq, span).astype(BF16), vt, preferred_element_type=F32)
        _store_group(o_ref, o / jnp.maximum(l.reshape(NSA_HPG * tq, 1), 1e-30), g, tq)


def win_attention_prompt(qr, k, v):
    b, s, _ = qr.shape
    tq = min(s, 256)
    span = min(s, WINDOW + tq)
    assert WINDOW % tq == 0 or s == tq
    kv_spec = pl.BlockSpec((1, s, NSA_KV), lambda bi, i: (bi, 0, 0))
    return pl.pallas_call(
        functools.partial(_win_body, tq=tq, span=span),
        grid=(b, s // tq),
        in_specs=[pl.BlockSpec((1, tq, NSA_Q), lambda bi, i: (bi, i, 0)), kv_spec, kv_spec],
        out_specs=pl.BlockSpec((1, tq, NSA_Q), lambda bi, i: (bi, i, 0)),
        out_shape=jax.ShapeDtypeStruct((b, s, NSA_Q), F32),
        compiler_params=pltpu.CompilerParams(dimension_semantics=("arbitrary", "arbitrary"),
                                             vmem_limit_bytes=VMEM_LIMIT_BYTES),
        name="win_attention_prompt",
    )(qr, k, v)


CMP_PAGES_PER_STEP = 16
SUB_PER_PAGE = PAGE_SIZE // CMP_STRIDE
PAGE_ROWS = PAGE_SIZE * NSA_KV_HEADS
CMP_R = CMP_LEN // CMP_STRIDE
SUBLANES = 8
Q_ROWS = SUBLANES


def _pad_rows(x, rows):
    return jnp.concatenate([x, jnp.zeros((rows - x.shape[0], x.shape[1]), x.dtype)], axis=0)


def _token_group_queries(q, g):
    rows = [q[:, (g * NSA_HPG + h) * NSA_HD:(g * NSA_HPG + h + 1) * NSA_HD] for h in range(NSA_HPG)]
    return _pad_rows(jnp.concatenate(rows, axis=0), Q_ROWS)


def _subblock_weight(w1):
    w = w1.reshape(CMP_R, CMP_STRIDE, NSA_HD, NSA_HD)
    return w.transpose(1, 2, 0, 3).reshape(CMP_STRIDE * NSA_HD, CMP_R * NSA_HD).astype(BF16)


def _pe_rows(pe):
    return _pad_rows(pe.reshape(CMP_R, CMP_STRIDE * NSA_HD), Q_ROWS).astype(BF16)


def _compress_from_parts(parts, pe_rows, w_sub, w2):
    return _compress_with_pe(parts, jnp.dot(pe_rows, w_sub, preferred_element_type=F32), w2)


def _compress_with_pe(parts, pe_proj, w2):
    n = parts.shape[0]
    hid = parts[:, :NSA_HD]
    for r in range(1, CMP_R):
        hid = hid + pltpu.roll(parts[:, r * NSA_HD:(r + 1) * NSA_HD], n - r, 0)
    for r in range(CMP_R):
        hid = hid + pe_proj[r:r + 1, r * NSA_HD:(r + 1) * NSA_HD]
    act = hid * jax.nn.sigmoid(hid)
    return jnp.dot(act.astype(BF16), w2, preferred_element_type=F32)


def _cmp_sample_body(pt_ref, *refs, q_pos, nc):
    npg = CMP_PAGES_PER_STEP
    k_pages, v_pages = refs[:npg], refs[npg:2 * npg]
    (wk_ref, wv_ref, pek_ref, pev_ref, w2k_ref, w2v_ref, gk_ref, q_ref,
     oc_ref, imp_ref, pk_ref, pv_ref, regroup_k_ref, regroup_v_ref, pe_ref) = refs[2 * npg:]
    c = pl.program_id(1)
    rows = npg * SUB_PER_PAGE

    @pl.when((pl.program_id(0) == 0) & (c == 0))
    def _():
        pe_ref[0] = jnp.dot(pek_ref[...], wk_ref[...], preferred_element_type=F32)
        pe_ref[1] = jnp.dot(pev_ref[...], wv_ref[...], preferred_element_type=F32)

    def project(pages, w_ref, parts_ref, regroup_ref):
        per_sub = CMP_STRIDE * NSA_KV_HEADS
        assert per_sub % SUBLANES == 0
        for pi, p in enumerate(pages):
            for r0 in range(0, PAGE_ROWS, SUBLANES):
                nn, sg0 = r0 // per_sub, r0 % per_sub
                regroup_ref[pi, pl.ds(sg0 * SUB_PER_PAGE + nn, SUBLANES, stride=SUB_PER_PAGE), :] = (
                    p[0, r0:r0 + SUBLANES, :])
        tile = lambda pi, s, g: regroup_ref[pi, (s * NSA_KV_HEADS + g) * SUB_PER_PAGE:
                                            (s * NSA_KV_HEADS + g + 1) * SUB_PER_PAGE, :]
        xg = jnp.concatenate(
            [jnp.concatenate(
                [jnp.concatenate([tile(pi, s, g) for s in range(CMP_STRIDE)], axis=1) for pi in range(len(pages))],
                axis=0)
             for g in range(NSA_KV_HEADS)], axis=0).astype(BF16)
        parts = jnp.dot(xg, w_ref[...], preferred_element_type=F32)
        for g in range(NSA_KV_HEADS):
            parts_ref[g, pl.ds(pl.multiple_of(c * rows, rows), rows), :] = parts[g * rows:(g + 1) * rows]

    project(k_pages, wk_ref, pk_ref, regroup_k_ref)
    project(v_pages, wv_ref, pv_ref, regroup_v_ref)

    @pl.when(c == pl.num_programs(1) - 1)
    def _():
        ncp = pk_ref.shape[1]
        q = q_ref[0]
        cidx = lax.broadcasted_iota(jnp.int32, (Q_ROWS, ncp), 1)
        vis = (cidx < nc) & (cidx * CMP_STRIDE + (CMP_LEN - 1) <= q_pos)
        for g in range(NSA_KV_HEADS):
            kcc = _head_norm(_compress_with_pe(pk_ref[g], pe_ref[0], w2k_ref[...]), gk_ref[...])
            vcc = _compress_with_pe(pv_ref[g], pe_ref[1], w2v_ref[...])
            s = lax.dot_general(_token_group_queries(q, g), kcc.astype(BF16), NT_DIMS, preferred_element_type=F32)
            sm = jnp.where(vis, s, NEG_INF)
            m = jnp.max(sm, axis=-1, keepdims=True)
            m = jnp.where(m > NEG_INF, m, 0.0)
            p = jnp.exp(sm - m)
            p = p / jnp.maximum(jnp.sum(p, axis=-1, keepdims=True), 1e-30)
            o = jnp.dot(p.astype(BF16), vcc.astype(BF16), preferred_element_type=F32)
            for h in range(NSA_HPG):
                c0 = (g * NSA_HPG + h) * NSA_HD
                oc_ref[0, :, c0:c0 + NSA_HD] = o[h:h + 1]
            imp_ref[0, g:g + 1, :] = jnp.sum(p[:NSA_HPG], axis=0, keepdims=True)


def cmp_attention_sample(qn, pool_k, pool_v, page_table, lp, q_pos):
    db, n_pages = page_table.shape
    n_pool = pool_k.shape[0]
    npg = CMP_PAGES_PER_STEP
    assert n_pages % npg == 0
    n_sub = n_pages * SUB_PER_PAGE
    nc = n_sub - CMP_R + 1
    view = lambda p: p.reshape(n_pool, PAGE_ROWS, NSA_HD)
    page_spec = lambda i: pl.BlockSpec((1, PAGE_ROWS, NSA_HD), lambda b, c, pt: (pt[b, c * npg + i], 0, 0))
    const = lambda shape: pl.BlockSpec(shape, lambda b, c, pt: (0,) * len(shape))
    kdim = CMP_STRIDE * NSA_HD
    grid_spec = pltpu.PrefetchScalarGridSpec(
        num_scalar_prefetch=1,
        grid=(db, n_pages // npg),
        in_specs=([page_spec(i) for i in range(npg)] + [page_spec(i) for i in range(npg)]
                  + [const((kdim, CMP_R * NSA_HD)), const((kdim, CMP_R * NSA_HD)),
                     const((Q_ROWS, kdim)), const((Q_ROWS, kdim)),
                     const((NSA_HD, NSA_HD)), const((NSA_HD, NSA_HD)), const((1, NSA_HD)),
                     pl.BlockSpec((1, 1, NSA_Q), lambda b, c, pt: (b, 0, 0))]),
        out_specs=[pl.BlockSpec((1, 1, NSA_Q), lambda b, c, pt: (b, 0, 0)),
                   pl.BlockSpec((1, NSA_KV_HEADS, n_sub), lambda b, c, pt: (b, 0, 0))],
        scratch_shapes=[pltpu.VMEM((NSA_KV_HEADS, n_sub, CMP_R * NSA_HD), F32),
                        pltpu.VMEM((NSA_KV_HEADS, n_sub, CMP_R * NSA_HD), F32),
                        pltpu.VMEM((npg, PAGE_ROWS, NSA_HD), F32), pltpu.VMEM((npg, PAGE_ROWS, NSA_HD), F32),
                        pltpu.VMEM((2, Q_ROWS, CMP_R * NSA_HD), F32)],
    )
    return pl.pallas_call(
        functools.partial(_cmp_sample_body, q_pos=q_pos, nc=nc),
        grid_spec=grid_spec,
        out_shape=[jax.ShapeDtypeStruct((db, 1, NSA_Q), F32), jax.ShapeDtypeStruct((db, NSA_KV_HEADS, n_sub), F32)],
        compiler_params=pltpu.CompilerParams(dimension_semantics=("arbitrary", "arbitrary"),
                                             vmem_limit_bytes=VMEM_LIMIT_BYTES),
        name="cmp_attention_sample",
    )(page_table, *([view(pool_k)] * npg), *([view(pool_v)] * npg),
      _subblock_weight(lp['cmp_w1_k']), _subblock_weight(lp['cmp_w1_v']),
      _pe_rows(lp['cmp_pe_k']), _pe_rows(lp['cmp_pe_v']),
      lp['cmp_w2_k'].astype(BF16), lp['cmp_w2_v'].astype(BF16),
      lp['nsa_k_norm_cmp'].reshape(1, NSA_HD), qn), nc


def _cmp_kv_prompt_body(*refs, n_sub):
    kc_refs, vc_refs = refs[:NSA_KV_HEADS], refs[NSA_KV_HEADS:2 * NSA_KV_HEADS]
    wk_ref, wv_ref, pek_ref, pev_ref, w2k_ref, w2v_ref, gk_ref, kcc_ref, vcc_ref = refs[2 * NSA_KV_HEADS:]

    def compress(x_ref, w_ref, pe_ref, w2_ref):
        x = jnp.concatenate([x_ref[0, pl.ds(s, n_sub, stride=CMP_STRIDE), :] for s in range(CMP_STRIDE)],
                            axis=1).astype(BF16)
        parts = jnp.dot(x, w_ref[...], preferred_element_type=F32)
        return _compress_from_parts(parts, pe_ref[...], w_ref[...], w2_ref[...])

    for g in range(NSA_KV_HEADS):
        kcc = _head_norm(compress(kc_refs[g], wk_ref, pek_ref, w2k_ref), gk_ref[...])
        kcc_ref[0, g] = kcc.astype(kcc_ref.dtype)
        vcc_ref[0, g] = compress(vc_refs[g], wv_ref, pev_ref, w2v_ref).astype(vcc_ref.dtype)


def compressed_kv_prompt(p, lp):
    b, s, _ = p.shape
    assert s % CMP_STRIDE == 0
    n_sub = s // CMP_STRIDE
    nc = n_sub - CMP_R + 1
    kdim = CMP_STRIDE * NSA_HD
    kv_w = NSA_KV // LANE
    const = lambda shape: pl.BlockSpec(shape, lambda bi: (0,) * len(shape))
    assert NSA_HD == LANE
    src = lambda cb: pl.BlockSpec((1, s, NSA_HD), lambda bi: (bi, 0, cb))
    out = pl.BlockSpec((1, NSA_KV_HEADS, n_sub, NSA_HD), lambda bi: (bi, 0, 0, 0))
    kcc, vcc = pl.pallas_call(
        functools.partial(_cmp_kv_prompt_body, n_sub=n_sub),
        grid=(b,),
        in_specs=[src(CB_KV + g) for g in range(NSA_KV_HEADS)] + [src(CB_KV + kv_w + g) for g in range(NSA_KV_HEADS)] + [
                  const((kdim, CMP_R * NSA_HD)), const((kdim, CMP_R * NSA_HD)),
                  const((Q_ROWS, kdim)), const((Q_ROWS, kdim)),
                  const((NSA_HD, NSA_HD)), const((NSA_HD, NSA_HD)), const((1, NSA_HD))],
        out_specs=[out, out],
        out_shape=[jax.ShapeDtypeStruct((b, NSA_KV_HEADS, n_sub, NSA_HD), BF16)] * 2,
        compiler_params=pltpu.CompilerParams(dimension_semantics=("arbitrary",),
                                             vmem_limit_bytes=VMEM_LIMIT_BYTES),
        name="compressed_kv_prompt",
    )(*([p] * (2 * NSA_KV_HEADS)), _subblock_weight(lp['cmp_w1_k']), _subblock_weight(lp['cmp_w1_v']),
      _pe_rows(lp['cmp_pe_k']), _pe_rows(lp['cmp_pe_v']),
      lp['cmp_w2_k'].astype(BF16), lp['cmp_w2_v'].astype(BF16), lp['nsa_k_norm_cmp'].reshape(1, NSA_HD))
    return kcc, vcc, nc


def _gdn_sample_body(qkv_ref, buf_ref, w_ref, sm_ref, z_ref, rec_ref, alog_ref, dt_ref, gn_ref,
                     o_ref, conv_ref, rec_out_ref):
    x = qkv_ref[0]
    buf = buf_ref[0]
    w = w_ref[...]
    acc = x * w[GDN_CONV - 1:GDN_CONV]
    for j in range(GDN_CONV - 1):
        acc = acc + buf[j:j + 1] * w[j:j + 1]
    for j in range(GDN_CONV - 2):
        conv_ref[0, j:j + 1, :] = buf[j + 1:j + 2]
    conv_ref[0, GDN_CONV - 2:GDN_CONV - 1, :] = x
    cv = acc * jax.nn.sigmoid(acc)
    g_all, beta_all = _gdn_gates(sm_ref[0], alog_ref[...], dt_ref[...])
    z = z_ref[0]
    rows = []
    heads = []
    for h in range(GDN_HEADS):
        q = cv[:, h * GDN_DK:(h + 1) * GDN_DK]
        k = cv[:, GDN_QK + h * GDN_DK:GDN_QK + (h + 1) * GDN_DK]
        v = cv[:, 2 * GDN_QK + h * GDN_DV:2 * GDN_QK + (h + 1) * GDN_DV]
        q = q * lax.rsqrt(jnp.sum(q * q, axis=-1, keepdims=True) + EPS) * (GDN_DK ** -0.5)
        k = k * lax.rsqrt(jnp.sum(k * k, axis=-1, keepdims=True) + EPS)
        g = g_all[:, SMALL_A + h:SMALL_A + h + 1]
        beta = beta_all[:, SMALL_B + h:SMALL_B + h + 1]
        eg = jnp.exp(g)
        heads.append((q, k, v * beta, eg, jnp.sum(q * k, axis=-1, keepdims=True)))
        rows += [q * eg, k * beta * eg, k]
    stacked = jnp.concatenate(rows, axis=0)
    eye = jnp.where(lax.broadcasted_iota(jnp.int32, (GDN_DK, GDN_DK), 0)
                    == lax.broadcasted_iota(jnp.int32, (GDN_DK, GDN_DK), 1), 1.0, 0.0).astype(BF16)
    cols = sum(lax.dot_general(eye, piece, NT_DIMS, preferred_element_type=F32) for piece in _split3(stacked))
    for h, (q, k, u, eg, attn) in enumerate(heads):
        s_old = rec_ref[0, h]
        qe_col, w_col, k_col = (cols[:, 3 * h + i:3 * h + i + 1] for i in range(3))
        v_new = u - jnp.sum(w_col * s_old, axis=0, keepdims=True)
        o = jnp.sum(qe_col * s_old, axis=0, keepdims=True) + attn * v_new
        rec_out_ref[0, h] = s_old * eg + k_col * v_new
        zc = z[:, h * GDN_DV:(h + 1) * GDN_DV]
        o_ref[0, :, h * GDN_DV:(h + 1) * GDN_DV] = _head_norm(o, gn_ref[...]) * (zc * jax.nn.sigmoid(zc))


def gdn_sample(p, conv_buf, rec, lp):
    db = p.shape[0]
    tok = lambda w, cb: pl.BlockSpec((1, 1, w), lambda b: (b, 0, cb * LANE // w))
    vec = pl.BlockSpec((1, LANE), lambda b: (0, 0))
    buf = pl.BlockSpec((1, GDN_CONV - 1, GDN_CONV_DIM), lambda b: (b, 0, 0))
    st = pl.BlockSpec((1, GDN_HEADS, GDN_DK, GDN_DV), lambda b: (b, 0, 0, 0))
    assert CB_QKV == 0 and (CB_Z * LANE) % GDN_V == 0
    return pl.pallas_call(
        _gdn_sample_body,
        grid=(db,),
        in_specs=[tok(GDN_CONV_DIM, CB_QKV), buf, pl.BlockSpec((GDN_CONV, GDN_CONV_DIM), lambda b: (0, 0)),
                  tok(LANE, CB_SMALL), tok(GDN_V, CB_Z), st, vec, vec, vec],
        out_specs=[pl.BlockSpec((1, 1, GDN_V), lambda b: (b, 0, 0)), buf, st],
        out_shape=[jax.ShapeDtypeStruct((db, 1, GDN_V), F32),
                   jax.ShapeDtypeStruct((db, GDN_CONV - 1, GDN_CONV_DIM), F32),
                   jax.ShapeDtypeStruct((db, GDN_HEADS, GDN_DK, GDN_DV), F32)],
        compiler_params=pltpu.CompilerParams(dimension_semantics=("arbitrary",),
                                             vmem_limit_bytes=VMEM_LIMIT_BYTES),
        name="gdn_sample",
    )(p, conv_buf, lp['gdn_conv_w'], p, p, rec,
      _lane_row(lp['gdn_a_log'], SMALL_A), _lane_row(lp['gdn_dt_bias'], SMALL_A),
      lp['gdn_out_norm'].reshape(1, GDN_DV))


def _merge_body(x_ref, oa_ref, oc_ref, os_ref, ow_ref, ga_ref, gb_ref, sm_ref, wo_ref, y_ref):
    gates = jax.nn.sigmoid(sm_ref[...])
    o_b = jnp.zeros(x_ref.shape, F32)
    for br, ref in enumerate((oc_ref, os_ref, ow_ref)):
        o = ref[...]
        o_b = o_b + jnp.concatenate(
            [gates[:, SMALL_G + br * NSA_HEADS + h:SMALL_G + br * NSA_HEADS + h + 1] * o[:, h * NSA_HD:(h + 1) * NSA_HD]
             for h in range(NSA_HEADS)], axis=1)
    m = jax.nn.sigmoid(ga_ref[...]) * oa_ref[...] + jax.nn.sigmoid(gb_ref[...]) * o_b
    y_ref[...] = x_ref[...] + jnp.dot(m.astype(BF16), wo_ref[...], preferred_element_type=F32)


def merge_branches_fused(x2, p2, o_a, o_c, o_s, o_w, w_o):
    n = x2.shape[0]
    tm = min(n, 512)
    row = pl.BlockSpec((tm, D_MODEL), lambda i: (i, 0))
    pcol = lambda w, cb: pl.BlockSpec((tm, w), lambda i: (i, cb * LANE // w))
    return pl.pallas_call(
        _merge_body,
        grid=(n // tm,),
        in_specs=[row, row, row, row, row, pcol(D_MODEL, CB_GA), pcol(D_MODEL, CB_GB), pcol(LANE, CB_SMALL),
                  _const_spec((D_MODEL, D_MODEL))],
        out_specs=row,
        out_shape=jax.ShapeDtypeStruct((n, D_MODEL), F32),
        compiler_params=pltpu.CompilerParams(dimension_semantics=("arbitrary",),
                                             vmem_limit_bytes=VMEM_LIMIT_BYTES),
        name="merge_branches",
    )(x2, o_a, o_c, o_s, o_w, p2, p2, p2, w_o.astype(BF16))


IDX_LANES = 128


def _select_sample_body(imp_ref, ovl_ref, idx_ref, *, q_pos, ns):
    imp = imp_ref[...]
    hi = imp.astype(BF16)
    lo = (imp - hi.astype(F32)).astype(BF16)
    score = (jnp.dot(hi, ovl_ref[...], preferred_element_type=F32)
             + jnp.dot(lo, ovl_ref[...], preferred_element_type=F32))
    rows, nsp = score.shape
    j = lax.broadcasted_iota(jnp.int32, (rows, nsp), 1)
    cur = q_pos // SLC_BLK
    valid = (j <= cur) & (j < ns)
    forced = (j == 0) | (j == cur) | (j == cur - 1)
    work = jnp.where(valid, jnp.where(forced, FORCE_SCORE, score), NEG_INF)
    slot = lax.broadcasted_iota(jnp.int32, (rows, IDX_LANES), 1)
    idx = jnp.full((rows, IDX_LANES), -1, jnp.int32)
    for it in range(min(SLC_TOPN, ns)):
        mx = jnp.max(work, axis=-1, keepdims=True)
        first = jnp.min(jnp.where(work == mx, j, nsp), axis=-1, keepdims=True)
        idx = jnp.where((slot == it) & (mx > NEG_INF), first, idx)
        work = jnp.where(j == first, NEG_INF, work)
    idx_ref[...] = idx


def select_blocks_sample(imp, nc, q_pos, n_keys):
    db, kv, ncp = imp.shape
    ns = -(-n_keys // SLC_BLK)
    nsp = -(-ns // 128) * 128
    cstart = np.arange(ncp) * CMP_STRIDE
    sstart = np.arange(nsp) * SLC_BLK
    ovl = ((cstart[:, None] < sstart[None, :] + SLC_BLK) & (cstart[:, None] + CMP_LEN > sstart[None, :])
           & (np.arange(ncp)[:, None] < nc) & (np.arange(nsp)[None, :] < ns))
    rows = db * kv
    idx = pl.pallas_call(
        functools.partial(_select_sample_body, q_pos=q_pos, ns=ns),
        grid=(1,),
        in_specs=[pl.BlockSpec((rows, ncp), lambda i: (0, 0)), pl.BlockSpec((ncp, nsp), lambda i: (0, 0))],
        out_specs=pl.BlockSpec((rows, IDX_LANES), lambda i: (0, 0)),
        out_shape=jax.ShapeDtypeStruct((rows, IDX_LANES), jnp.int32),
        compiler_params=pltpu.CompilerParams(vmem_limit_bytes=VMEM_LIMIT_BYTES),
        name="select_blocks_sample",
    )(imp.reshape(rows, ncp), jnp.asarray(ovl, dtype=BF16))
    return idx.reshape(db, kv, IDX_LANES)


def _token_softmax_attend(q8, s_past, allowed, v_past, k_new, v_new, new_ok):
    s_new = jnp.sum(q8.astype(F32) * k_new, axis=-1, keepdims=True)
    sm = jnp.where(allowed, s_past, NEG_INF)
    m = jnp.maximum(jnp.max(sm, axis=-1, keepdims=True), jnp.where(new_ok, s_new, NEG_INF))
    m = jnp.where(m > NEG_INF, m, 0.0)
    p = jnp.exp(sm - m)
    p_new = jnp.where(new_ok, jnp.exp(s_new - m), 0.0)
    l = jnp.sum(p, axis=-1, keepdims=True) + p_new
    o = jnp.dot(p.astype(BF16), v_past, preferred_element_type=F32) + p_new * v_new
    return o / jnp.maximum(l, 1e-30)


def _slc_sample_body(idx_ref, pt_ref, *refs, n_past):
    n = SLC_TOPN
    k_blocks, v_blocks = refs[:n], refs[n:2 * n]
    q_ref, kn_ref, vn_ref, o_ref = refs[2 * n:]
    b, g = pl.program_id(0), pl.program_id(1)
    q8 = _token_group_queries_half(q_ref[0])

    def group_rows(blocks):
        sel_g = jnp.full((SLC_TOPN * SLC_BLK, NSA_HD), g, jnp.int32)
        per_group = [jnp.concatenate([r[0, pl.ds(gg, SLC_BLK, stride=NSA_KV_HEADS), :] for r in blocks], axis=0)
                     for gg in range(NSA_KV_HEADS)]
        out = per_group[0]
        for gg in range(1, NSA_KV_HEADS):
            out = jnp.where(sel_g == gg, per_group[gg], out)
        return out.astype(BF16)

    k = group_rows(k_blocks)
    v = group_rows(v_blocks)
    s = lax.dot_general(q8, k, NT_DIMS, preferred_element_type=F32)
    slot = lax.broadcasted_iota(jnp.int32, s.shape, 1) // SLC_BLK
    slot_ok = jnp.zeros(s.shape, jnp.int32)
    n_new = jnp.int32(0)
    for i in range(n):
        ji = idx_ref[(b * NSA_KV_HEADS + g) * n + i]
        slot_ok = jnp.where(slot == i, jnp.where((ji >= 0) & (ji < n_past), 1, 0), slot_ok)
        n_new = n_new + jnp.where(ji >= n_past, 1, 0)
    new_ok = jnp.full((Q_ROWS, 1), n_new, jnp.int32) > 0
    o = _token_softmax_attend(q8, s, slot_ok > 0, v, kn_ref[0], vn_ref[0], new_ok)
    for h in range(NSA_HPG):
        o_ref[0, :, h * NSA_HD:(h + 1) * NSA_HD] = o[h:h + 1]


def _token_group_queries_half(q):
    rows = [q[:, h * NSA_HD:(h + 1) * NSA_HD] for h in range(NSA_HPG)]
    return _pad_rows(jnp.concatenate(rows, axis=0), Q_ROWS)


def slc_attention_sample(qr, idx, pool_k, pool_v, page_table, k_new, v_new):
    db, n_pages = page_table.shape
    n_pool = pool_k.shape[0]
    bpp = PAGE_SIZE // SLC_BLK
    n_past = n_pages * bpp
    n = SLC_TOPN
    gw = NSA_HPG * NSA_HD
    idx_flat = idx[:, :, :n].reshape(-1)
    pt_flat = page_table.reshape(-1)
    view = lambda p: p.reshape(n_pool * bpp, SLC_BLK * NSA_KV_HEADS, NSA_HD)

    def blk_spec(i):
        def index_map(b, g, idx_s, pt_s):
            jc = jnp.clip(idx_s[(b * NSA_KV_HEADS + g) * n + i], 0, n_past - 1)
            return (pt_s[b * n_pages + jc // bpp] * bpp + jc % bpp, 0, 0)
        return pl.BlockSpec((1, SLC_BLK * NSA_KV_HEADS, NSA_HD), index_map)

    grid_spec = pltpu.PrefetchScalarGridSpec(
        num_scalar_prefetch=2,
        grid=(db, NSA_KV_HEADS),
        in_specs=([blk_spec(i) for i in range(n)] + [blk_spec(i) for i in range(n)]
                  + [pl.BlockSpec((1, 1, gw), lambda b, g, *_: (b, 0, g)),
                     pl.BlockSpec((1, 1, NSA_HD), lambda b, g, *_: (b, 0, g)),
                     pl.BlockSpec((1, 1, NSA_HD), lambda b, g, *_: (b, 0, g))]),
        out_specs=pl.BlockSpec((1, 1, gw), lambda b, g, *_: (b, 0, g)),
    )
    return pl.pallas_call(
        functools.partial(_slc_sample_body, n_past=n_past),
        grid_spec=grid_spec,
        out_shape=jax.ShapeDtypeStruct((db, 1, NSA_Q), F32),
        compiler_params=pltpu.CompilerParams(dimension_semantics=("arbitrary", "arbitrary"),
                                             vmem_limit_bytes=VMEM_LIMIT_BYTES),
        name="slc_attention_sample",
    )(idx_flat, pt_flat, *([view(pool_k)] * n), *([view(pool_v)] * n), qr, k_new, v_new)


def _win_sample_body(q_ref, bk_ref, bv_ref, kn_ref, vn_ref, o_ref, nk_ref, nv_ref, *, q_pos, buf_start):
    rows = bk_ref.shape[1]
    wb = rows // NSA_KV_HEADS
    kn, vn = kn_ref[0], vn_ref[0]
    q = q_ref[0]
    kpos = buf_start + lax.broadcasted_iota(jnp.int32, (Q_ROWS, wb), 1)
    allowed = (kpos <= q_pos) & (kpos > q_pos - WINDOW)
    new_ok = jnp.ones((Q_ROWS, 1), jnp.bool_)
    for g in range(NSA_KV_HEADS):
        gcols = slice(g * NSA_HD, (g + 1) * NSA_HD)
        q8 = _token_group_queries(q, g)
        kg = bk_ref[0, pl.ds(g, wb, stride=NSA_KV_HEADS), :].astype(BF16)
        vg = bv_ref[0, pl.ds(g, wb, stride=NSA_KV_HEADS), :].astype(BF16)
        s = lax.dot_general(q8, kg, NT_DIMS, preferred_element_type=F32)
        o = _token_softmax_attend(q8, s, allowed, vg, kn[:, gcols], vn[:, gcols], new_ok)
        for h in range(NSA_HPG):
            c0 = (g * NSA_HPG + h) * NSA_HD
            o_ref[0, :, c0:c0 + NSA_HD] = o[h:h + 1]
    row = lax.broadcasted_iota(jnp.int32, (rows, NSA_HD), 0)

    def shifted(buf_ref, new):
        out = pltpu.roll(buf_ref[0], rows - NSA_KV_HEADS, 0)
        for g in range(NSA_KV_HEADS):
            out = jnp.where(row == rows - NSA_KV_HEADS + g, new[:, g * NSA_HD:(g + 1) * NSA_HD], out)
        return out

    nk_ref[0] = shifted(bk_ref, kn)
    nv_ref[0] = shifted(bv_ref, vn)


def win_attention_sample(qr, buf_k, buf_v, k_new, v_new, q_pos):
    db, rows, _ = buf_k.shape
    wb = rows // NSA_KV_HEADS
    tok = lambda w: pl.BlockSpec((1, 1, w), lambda b: (b, 0, 0))
    buf = pl.BlockSpec((1, rows, NSA_HD), lambda b: (b, 0, 0))
    return pl.pallas_call(
        functools.partial(_win_sample_body, q_pos=q_pos, buf_start=q_pos - wb),
        grid=(db,),
        in_specs=[tok(NSA_Q), buf, buf, tok(NSA_KV), tok(NSA_KV)],
        out_specs=[tok(NSA_Q), buf, buf],
        out_shape=[jax.ShapeDtypeStruct((db, 1, NSA_Q), F32), jax.ShapeDtypeStruct((db, rows, NSA_HD), F32),
                   jax.ShapeDtypeStruct((db, rows, NSA_HD), F32)],
        compiler_params=pltpu.CompilerParams(dimension_semantics=("arbitrary",),
                                             vmem_limit_bytes=VMEM_LIMIT_BYTES),
        name="win_attention_sample",
    )(qr, buf_k, buf_v, k_new, v_new)


def _kv_cols(p, i):
    c0 = CB_KV * LANE + i * NSA_KV
    return p[..., c0:c0 + NSA_KV]


def token_mix_prompt(x, lp):
    b, s, _ = x.shape
    pos = jnp.arange(s)
    x2 = x.reshape(b * s, D_MODEL)
    p2 = mixer_projection(x2, lp['mix_norm'], lp['w_in'])
    p = p2.reshape(b, s, P_COLS)
    o_a, rec_new = gdn_prompt(p, lp)
    assert s >= GDN_CONV - 1
    conv_new = p[:, s - (GDN_CONV - 1):, :GDN_CONV_DIM]
    heads = lambda a_: a_.reshape(b, -1, NSA_KV_HEADS, NSA_HD)
    qn, qr, ks_r, kw_r = nsa_prep(p, pos, lp)
    kcc, vcc, nc = compressed_kv_prompt(p, lp)
    o_c, sel = cmp_select_prompt(qn, kcc, vcc, nc)
    vs, vw = _kv_cols(p, 3), _kv_cols(p, 5)
    o_s = slc_attention_prompt(qr, sel, ks_r.astype(BF16), vs.astype(BF16))
    o_w = win_attention_prompt(qr, kw_r.astype(BF16), vw.astype(BF16))
    flat = lambda a_: a_.reshape(b * s, -1)
    y = merge_branches_fused(x2, p2, flat(o_a), flat(o_c), flat(o_s), flat(o_w), lp['w_o']).reshape(b, s, D_MODEL)
    wb = min(WINDOW, s)
    return y, (conv_new, rec_new, heads(_kv_cols(p, 0)), heads(_kv_cols(p, 1)), heads(ks_r), heads(vs),
               heads(kw_r[:, s - wb:]), heads(vw[:, s - wb:]))


def token_mix_sample(x, layer, conv_buf, rec, cache_k_cmp, cache_v_cmp, cache_k_slc, cache_v_slc,
                     win_k, win_v, page_table, lp):
    db, t, _ = x.shape
    assert t == 1 and t < CMP_STRIDE
    q_pos = PAST_LEN
    x2 = x.reshape(db, D_MODEL)
    p2 = mixer_projection(x2, lp['mix_norm'], lp['w_in'])
    p = p2.reshape(db, 1, P_COLS)
    o_a, conv_new, rec_new = gdn_sample(p, conv_buf, rec, lp)
    heads = lambda a_: a_.reshape(db, -1, NSA_KV_HEADS, NSA_HD)
    qn, qr, ks_r, kw_r = [a_.reshape(db, 1, -1) for a_ in
                          nsa_prep(p2.reshape(1, db, P_COLS), jnp.full((db,), q_pos), lp)]
    (o_c, imp), nc = cmp_attention_sample(qn, cache_k_cmp[layer], cache_v_cmp[layer], page_table, lp, q_pos)
    idx = select_blocks_sample(imp, nc, q_pos, PAST_LEN + t)
    vs, vw = _kv_cols(p, 3), _kv_cols(p, 5)
    o_s = slc_attention_sample(qr, idx, cache_k_slc[layer], cache_v_slc[layer], page_table, ks_r, vs)
    wb = win_k.shape[1]
    o_w, win_k_new, win_v_new = win_attention_sample(
        qr, win_k.reshape(db, wb * NSA_KV_HEADS, NSA_HD), win_v.reshape(db, wb * NSA_KV_HEADS, NSA_HD),
        kw_r, vw, q_pos)
    flat = lambda a_: a_.reshape(db, -1)
    y = merge_branches_fused(x2, p2, flat(o_a), flat(o_c), flat(o_s), flat(o_w), lp['w_o']).reshape(db, 1, D_MODEL)
    return y, (conv_new, rec_new, heads(_kv_cols(p, 0)), heads(_kv_cols(p, 1)), heads(ks_r), heads(vs),
               heads(win_k_new), heads(win_v_new))


def kernel(x_prompt, x_sample, state_gdn_conv, state_gdn_rec, cache_k_cmp, cache_v_cmp, cache_k_slc, cache_v_slc, cache_k_win, cache_v_win, page_table, ffn1_norm, ffn1_w_in, ffn1_w_out, mix_norm, w_in, gdn_conv_w, gdn_a_log, gdn_dt_bias, gdn_out_norm, nsa_q_norm, nsa_k_norm_cmp, nsa_k_norm_slc, nsa_k_norm_win, cmp_w1_k, cmp_w2_k, cmp_pe_k, cmp_w1_v, cmp_w2_v, cmp_pe_v, w_o, ffn2_norm, ffn2_w_in, ffn2_w_out):
    depth = w_in.shape[0]
    yp, ys = x_prompt, x_sample
    p_list, s_list = [], []
    for l in range(depth):
        lp = {'mix_norm': mix_norm[l], 'w_in': w_in[l], 'gdn_conv_w': gdn_conv_w[l],
              'gdn_a_log': gdn_a_log[l], 'gdn_dt_bias': gdn_dt_bias[l], 'gdn_out_norm': gdn_out_norm[l],
              'nsa_q_norm': nsa_q_norm[l], 'nsa_k_norm_cmp': nsa_k_norm_cmp[l],
              'nsa_k_norm_slc': nsa_k_norm_slc[l], 'nsa_k_norm_win': nsa_k_norm_win[l],
              'cmp_w1_k': cmp_w1_k[l], 'cmp_w2_k': cmp_w2_k[l], 'cmp_pe_k': cmp_pe_k[l],
              'cmp_w1_v': cmp_w1_v[l], 'cmp_w2_v': cmp_w2_v[l], 'cmp_pe_v': cmp_pe_v[l], 'w_o': w_o[l]}
        yp = swiglu_half(yp, ffn1_norm[l], ffn1_w_in[l], ffn1_w_out[l])
        yp, st_p = token_mix_prompt(yp, lp)
        yp = swiglu_half(yp, ffn2_norm[l], ffn2_w_in[l], ffn2_w_out[l])
        ys = swiglu_half(ys, ffn1_norm[l], ffn1_w_in[l], ffn1_w_out[l])
        ys, st_s = token_mix_sample(ys, l, state_gdn_conv[l], state_gdn_rec[l], cache_k_cmp, cache_v_cmp,
                                    cache_k_slc, cache_v_slc, cache_k_win[l], cache_v_win[l], page_table, lp)
        ys = swiglu_half(ys, ffn2_norm[l], ffn2_w_in[l], ffn2_w_out[l])
        p_list.append(st_p)
        s_list.append(st_s)
    (p_conv, p_rec, p_kc, p_vc, p_ks, p_vs, p_kw, p_vw) = [jnp.stack(a) for a in zip(*p_list)]
    (s_conv, s_rec, s_kc, s_vc, s_ks, s_vs, s_kw, s_vw) = [jnp.stack(a) for a in zip(*s_list)]
    return (yp, ys, p_conv, p_rec, p_kc, p_vc, p_ks, p_vs, p_kw, p_vw,
            s_conv, s_rec, s_kc, s_vc, s_ks, s_vs, s_kw, s_vw)
```

```python
import functools
import math

import numpy as np
import jax
import jax.numpy as jnp
from jax import lax
from jax.experimental import pallas as pl
from jax.experimental.pallas import tpu as pltpu

D_MODEL = 1024
PAST_LEN = 8192
PAGE_SIZE = 128

GDN_DK = 128
GDN_DV = 128
GDN_HEADS = D_MODEL // GDN_DV
GDN_QK = GDN_HEADS * GDN_DK
GDN_V = GDN_HEADS * GDN_DV
GDN_CONV_DIM = 2 * GDN_QK + GDN_V
GDN_CONV = 4
GDN_CHUNK = 64

NSA_HD = 128
NSA_HEADS = D_MODEL // NSA_HD
NSA_KV_HEADS = NSA_HEADS // 4
NSA_HPG = NSA_HEADS // NSA_KV_HEADS
NSA_Q = NSA_HEADS * NSA_HD
NSA_KV = NSA_KV_HEADS * NSA_HD
CMP_LEN = 32
CMP_STRIDE = 16
SLC_BLK = 64
SLC_TOPN = 16
WINDOW = 512
Q_BLOCK = 128
ATTN_SCALE = NSA_HD ** -0.5
FORCE_SCORE = 1e6

ROPE_DIM = NSA_HD // 4
ROPE_THETA = 500000.0

D_FF = ((8 * D_MODEL // 3 + 127) // 128) * 128
EPS = 1e-6

IN_SIZES = (GDN_CONV_DIM, GDN_HEADS, GDN_HEADS, GDN_V,
            NSA_Q, NSA_KV, NSA_KV, NSA_KV, NSA_KV, NSA_KV, NSA_KV, 3 * NSA_HEADS,
            D_MODEL, D_MODEL)

MXU_N = 256
VMEM_LIMIT_BYTES = 56 * 1024 * 1024

BF16 = jnp.bfloat16
F32 = jnp.float32


def _const_spec(shape):
    return pl.BlockSpec(shape, lambda *_: (0,) * len(shape), pipeline_mode=pl.Buffered(1))


def _ffn_body(x_ref, g_ref, wg_ref, wu_ref, wo_ref, o_ref):
    x = x_ref[...]
    ms = jnp.mean(x * x, axis=-1, keepdims=True)
    h = (x * lax.rsqrt(ms + EPS) * g_ref[...]).astype(BF16)
    acc = jnp.zeros(x.shape, F32)
    for c in range(D_FF // MXU_N):
        cols = slice(c * MXU_N, (c + 1) * MXU_N)
        gate = jnp.dot(h, wg_ref[:, cols], preferred_element_type=F32)
        up = jnp.dot(h, wu_ref[:, cols], preferred_element_type=F32)
        act = (gate * jax.nn.sigmoid(gate) * up).astype(BF16)
        acc = acc + jnp.dot(act, wo_ref[cols, :], preferred_element_type=F32)
    o_ref[...] = x + 0.5 * acc


def _ffn_tile(n_rows):
    return min(n_rows, 512)


def swiglu_half(x, g, w_in, w_out):
    shape = x.shape
    x2 = x.reshape(-1, D_MODEL)
    n = x2.shape[0]
    tm = _ffn_tile(n)
    wg = w_in[:, :D_FF].astype(BF16)
    wu = w_in[:, D_FF:].astype(BF16)
    wo = w_out.astype(BF16)
    out = pl.pallas_call(
        _ffn_body,
        grid=(n // tm,),
        in_specs=[pl.BlockSpec((tm, D_MODEL), lambda i: (i, 0)),
                  _const_spec((1, D_MODEL)),
                  _const_spec((D_MODEL, D_FF)),
                  _const_spec((D_MODEL, D_FF)),
                  _const_spec((D_FF, D_MODEL))],
        out_specs=pl.BlockSpec((tm, D_MODEL), lambda i: (i, 0)),
        out_shape=jax.ShapeDtypeStruct((n, D_MODEL), F32),
        compiler_params=pltpu.CompilerParams(dimension_semantics=("arbitrary",),
                                             vmem_limit_bytes=VMEM_LIMIT_BYTES),
        name="swiglu_half",
    )(x2, g.reshape(1, D_MODEL), wg, wu, wo)
    return out.reshape(shape)


LANE = 128
CB_QKV = 0
CB_Z = CB_QKV + GDN_CONV_DIM // LANE
CB_Q = CB_Z + GDN_V // LANE
CB_GA = CB_Q + NSA_Q // LANE
CB_GB = CB_GA + D_MODEL // LANE
CB_KV = CB_GB + D_MODEL // LANE
CB_SMALL = CB_KV + 6 * NSA_KV // LANE
P_COLS = (CB_SMALL + 1) * LANE
SMALL_A, SMALL_B, SMALL_G = 0, GDN_HEADS, 2 * GDN_HEADS
PROJ_CHUNKS = 3
assert (P_COLS // LANE) % PROJ_CHUNKS == 0 and 2 * GDN_HEADS + 3 * NSA_HEADS <= LANE


def _permuted_w_in(w_in):
    offs = [int(o) for o in np.cumsum((0,) + IN_SIZES)]
    seg = lambda i: w_in[:, offs[i]:offs[i + 1]]
    w = jnp.concatenate([seg(0), seg(3), seg(4), seg(12), seg(13)] + [seg(i) for i in range(5, 11)]
                        + [seg(1), seg(2), seg(11)], axis=1)
    return jnp.pad(w, ((0, 0), (0, P_COLS - w.shape[1]))).astype(BF16)


def _proj_body(x_ref, g_ref, w_ref, o_ref):
    x = x_ref[...]
    ms = jnp.mean(x * x, axis=-1, keepdims=True)
    h = (x * lax.rsqrt(ms + EPS) * g_ref[...]).astype(BF16)
    o_ref[...] = jnp.dot(h, w_ref[...], preferred_element_type=F32)


def mixer_projection(x2, g, w_in):
    n = x2.shape[0]
    tm = min(n, 512)
    wc = P_COLS // PROJ_CHUNKS
    return pl.pallas_call(
        _proj_body,
        grid=(PROJ_CHUNKS, n // tm),
        in_specs=[pl.BlockSpec((tm, D_MODEL), lambda c, i: (i, 0)),
                  pl.BlockSpec((1, D_MODEL), lambda c, i: (0, 0)),
                  pl.BlockSpec((D_MODEL, wc), lambda c, i: (0, c))],
        out_specs=pl.BlockSpec((tm, wc), lambda c, i: (i, c)),
        out_shape=jax.ShapeDtypeStruct((n, P_COLS), F32),
        compiler_params=pltpu.CompilerParams(dimension_semantics=("arbitrary", "arbitrary"),
                                             vmem_limit_bytes=VMEM_LIMIT_BYTES),
        name="mixer_projection",
    )(x2, g.reshape(1, D_MODEL), _permuted_w_in(w_in))


GDN_TILE = 256
GDN_HB = 4
GDN_SPLIT_LEVELS = 3
TN_DIMS = (((0,), (0,)), ((), ()))


def _split3(x):
    p1 = x.astype(BF16)
    r1 = x - p1.astype(F32)
    p2 = r1.astype(BF16)
    p3 = (r1 - p2.astype(F32)).astype(BF16)
    return p1, p2, p3


def _gdn_gates(small, alog_row, dt_row):
    xa = small + dt_row
    softplus = jnp.maximum(xa, 0.0) + jnp.log(1.0 + jnp.exp(-jnp.abs(xa)))
    return -jnp.exp(alog_row) * softplus, jax.nn.sigmoid(small)


def _lane_col(x, lane_idx):
    lane = lax.broadcasted_iota(jnp.int32, x.shape, 1)
    return jnp.sum(jnp.where(lane == lane_idx, x, 0.0), axis=1, keepdims=True)


def _gdn_prompt_body(q_ref, k_ref, v_ref, z_ref, sm_ref, wq_ref, wk_ref, wv_ref, alog_ref, dt_ref, gn_ref,
                     o_ref, rec_ref, s_ref, pq_ref, pk_ref, pv_ref, *, tile, hb):
    c = GDN_CHUNK
    hblk, step = pl.program_id(1), pl.program_id(2)
    width = hb * GDN_DK

    @pl.when(step == 0)
    def _():
        s_ref[...] = jnp.zeros(s_ref.shape, F32)
        for r in (pq_ref, pk_ref, pv_ref):
            r[...] = jnp.zeros(r.shape, F32)

    row8 = lax.broadcasted_iota(jnp.int32, (8, width), 0)

    def conv(x_ref, prev_ref, w_ref):
        x, prev, w = x_ref[0], prev_ref[...], w_ref[...]
        acc = x * w[GDN_CONV - 1:GDN_CONV]
        for j in range(1, GDN_CONV):
            main = pltpu.roll(x, j, 0)
            first = jnp.where(row8 < j, pltpu.roll(prev, j, 0), main[:8])
            acc = acc + jnp.concatenate([first, main[8:]], axis=0) * w[GDN_CONV - 1 - j:GDN_CONV - j]
        prev_ref[...] = x[tile - 8:]
        return acc * jax.nn.sigmoid(acc)

    qc, kc, vc = conv(q_ref, pq_ref, wq_ref), conv(k_ref, pk_ref, wk_ref), conv(v_ref, pv_ref, wv_ref)
    z = z_ref[0]
    g_all, beta_all = _gdn_gates(sm_ref[0], alog_ref[...], dt_ref[...])

    n = hb * c
    ii = lax.broadcasted_iota(jnp.int32, (n, n), 0)
    jj = lax.broadcasted_iota(jnp.int32, (n, n), 1)
    same_head = (ii // c) == (jj // c)
    incl, strict = same_head & (ii >= jj), same_head & (ii > jj)
    ci_ = lax.broadcasted_iota(jnp.int32, (c, c), 0)
    cj_ = lax.broadcasted_iota(jnp.int32, (c, c), 1)
    tri = jnp.where(ci_ >= cj_, 1.0, 0.0).astype(BF16)
    er = lax.broadcasted_iota(jnp.int32, (8, LANE), 0)
    el = lax.broadcasted_iota(jnp.int32, (8, LANE), 1)
    head_rows = jnp.where((el == SMALL_A + hblk * hb + er) & (er < hb), 1.0, 0.0).astype(BF16)
    dot = functools.partial(jnp.dot, preferred_element_type=F32)
    stack = lambda f: jnp.concatenate([f(hh) for hh in range(hb)], axis=0)
    states = [s_ref[hh] for hh in range(hb)]

    for ci in range(tile // c):
        rows = slice(ci * c, (ci + 1) * c)
        gc_all = sum(dot(tri, piece) for piece in _split3(g_all[rows]))
        gc_rows = sum(lax.dot_general(head_rows, piece, NT_DIMS, preferred_element_type=F32)
                      for piece in _split3(gc_all))
        head_cols = lambda x: stack(lambda hh: x[rows, hh * GDN_DK:(hh + 1) * GDN_DK])
        gcol = stack(lambda hh: _lane_col(gc_all, SMALL_A + hblk * hb + hh))
        bcol = stack(lambda hh: _lane_col(beta_all[rows], SMALL_B + hblk * hb + hh))
        grow = jnp.concatenate([gc_rows[hh:hh + 1] for hh in range(hb)], axis=1)
        g_last = stack(lambda hh: jnp.broadcast_to(gcol[hh * c + c - 1:hh * c + c], (c, 1)))
        q, k, v = head_cols(qc), head_cols(kc), head_cols(vc)
        q = q * lax.rsqrt(jnp.sum(q * q, axis=-1, keepdims=True) + EPS) * (GDN_DK ** -0.5)
        k = k * lax.rsqrt(jnp.sum(k * k, axis=-1, keepdims=True) + EPS)
        decay = jnp.where(incl, jnp.exp(jnp.where(incl, gcol - grow, 0.0)), 0.0)
        kb = k * bcol
        k16 = k.astype(BF16)
        m = jnp.where(strict, lax.dot_general(kb.astype(BF16), k16, NT_DIMS, preferred_element_type=F32) * decay, 0.0)
        egc = jnp.exp(gcol)
        rhs = jnp.concatenate([v * bcol, kb * egc], axis=1)
        mp = m
        sol = rhs
        for level in range(int(math.log2(c))):
            a1, a2, _ = _split3(mp)
            s1, s2, _ = _split3(sol)
            step_term = dot(a1, s1)
            if level < GDN_SPLIT_LEVELS:
                step_term = step_term + dot(a1, s2) + dot(a2, s1)
            sol = sol - step_term if level == 0 else sol + step_term
            if level + 1 < int(math.log2(c)):
                mp = dot(a1, a1)
        u, w = sol[:, :GDN_DV], sol[:, GDN_DV:]
        attn = jnp.where(incl, lax.dot_general(q.astype(BF16), k16, NT_DIMS, preferred_element_type=F32) * decay, 0.0)
        qe = q * egc
        k_dec = k * jnp.exp(g_last - gcol)
        s_decay = jnp.exp(g_last)
        v_new, qs = [], []
        for hh in range(hb):
            r = slice(hh * c, (hh + 1) * c)
            s16 = states[hh].astype(BF16)
            ws_qs = dot(jnp.concatenate([w[r], qe[r]], axis=0).astype(BF16), s16)
            v_new.append(u[r] - ws_qs[:c])
            qs.append(ws_qs[c:])
            states[hh] = states[hh] * s_decay[hh * c:hh * c + 1] + lax.dot_general(
                k_dec[r].astype(BF16), v_new[hh].astype(BF16), TN_DIMS, preferred_element_type=F32)
        o = jnp.concatenate(qs, axis=0) + dot(attn.astype(BF16), jnp.concatenate(v_new, axis=0).astype(BF16))
        for hh in range(hb):
            cols = slice(hh * GDN_DK, (hh + 1) * GDN_DK)
            zc = z[rows, cols]
            o_ref[0, rows, cols] = _head_norm(o[hh * c:(hh + 1) * c], gn_ref[...]) * (zc * jax.nn.sigmoid(zc))

    for hh in range(hb):
        s_ref[hh] = states[hh]

    @pl.when(step == pl.num_programs(2) - 1)
    def _():
        rec_ref[0] = s_ref[...]


def _lane_row(vec, offset):
    return jnp.pad(vec.astype(F32), (offset, LANE - offset - vec.shape[0])).reshape(1, LANE)


def gdn_prompt(p, lp):
    b, s, _ = p.shape
    tile, hb = GDN_TILE, GDN_HB
    assert s % tile == 0 and tile % GDN_CHUNK == 0 and GDN_HEADS % hb == 0 and GDN_CHUNK & (GDN_CHUNK - 1) == 0
    width = hb * GDN_DK
    wb = width // LANE
    col = lambda cb0: pl.BlockSpec((1, tile, width), lambda bi, h, i: (bi, i, cb0 // wb + h))
    wcol = lambda cb0: pl.BlockSpec((GDN_CONV, width), lambda bi, h, i: (0, cb0 // wb + h))
    vec = pl.BlockSpec((1, LANE), lambda bi, h, i: (0, 0))
    qk_blocks = GDN_QK // LANE
    return pl.pallas_call(
        functools.partial(_gdn_prompt_body, tile=tile, hb=hb),
        grid=(b, GDN_HEADS // hb, s // tile),
        in_specs=[col(CB_QKV), col(CB_QKV + qk_blocks), col(CB_QKV + 2 * qk_blocks), col(CB_Z),
                  pl.BlockSpec((1, tile, LANE), lambda bi, h, i: (bi, i, CB_SMALL)),
                  wcol(0), wcol(qk_blocks), wcol(2 * qk_blocks), vec, vec, vec],
        out_specs=[pl.BlockSpec((1, tile, width), lambda bi, h, i: (bi, i, h)),
                   pl.BlockSpec((1, hb, GDN_DK, GDN_DV), lambda bi, h, i: (bi, h, 0, 0))],
        out_shape=[jax.ShapeDtypeStruct((b, s, GDN_V), F32),
                   jax.ShapeDtypeStruct((b, GDN_HEADS, GDN_DK, GDN_DV), F32)],
        scratch_shapes=[pltpu.VMEM((hb, GDN_DK, GDN_DV), F32)] + [pltpu.VMEM((8, width), F32)] * 3,
        compiler_params=pltpu.CompilerParams(dimension_semantics=("arbitrary", "arbitrary", "arbitrary"),
                                             vmem_limit_bytes=VMEM_LIMIT_BYTES),
        name="gdn_prompt",
    )(p, p, p, p, p, lp['gdn_conv_w'], lp['gdn_conv_w'], lp['gdn_conv_w'],
      _lane_row(lp['gdn_a_log'], SMALL_A), _lane_row(lp['gdn_dt_bias'], SMALL_A),
      lp['gdn_out_norm'].reshape(1, GDN_DV))


NEG_INF = float("-inf")
NEG_BIG = -1e30
NT_DIMS = (((1,), (1,)), ((), ()))


def _rope_tables(pos):
    half = ROPE_DIM // 2
    inv = jnp.float32(ROPE_THETA) ** (-jnp.arange(half, dtype=jnp.float32) / half)
    ang = pos.astype(jnp.float32)[:, None] * inv
    cos, sin = jnp.cos(ang), jnp.sin(ang)
    rest = NSA_HD - ROPE_DIM
    cos_t = jnp.concatenate([cos, cos, jnp.ones((pos.shape[0], rest), F32)], axis=-1)
    sin_t = jnp.concatenate([-sin, sin, jnp.zeros((pos.shape[0], rest), F32)], axis=-1)
    return cos_t, sin_t


def _rope(y, cos_t, sin_t):
    half = ROPE_DIM // 2
    lane = lax.broadcasted_iota(jnp.int32, y.shape, 1)
    partner = jnp.where(lane < half, pltpu.roll(y, NSA_HD - half, 1), pltpu.roll(y, half, 1))
    return y * cos_t + partner * sin_t


def _head_norm(x, g):
    ms = jnp.mean(x * x, axis=-1, keepdims=True)
    return x * lax.rsqrt(ms + EPS) * g


def _nsa_prep_body(q_ref, ks_ref, kw_ref, cos_ref, sin_ref, gq_ref, gks_ref, gkw_ref,
                   qn_ref, qr_ref, ksr_ref, kwr_ref):
    cos_t, sin_t = cos_ref[...], sin_ref[...]
    for h in range(NSA_HEADS):
        cols = slice(h * NSA_HD, (h + 1) * NSA_HD)
        y = _head_norm(q_ref[0, :, cols], gq_ref[...])
        qn_ref[0, :, cols] = (y * ATTN_SCALE).astype(BF16)
        qr_ref[0, :, cols] = (_rope(y, cos_t, sin_t) * ATTN_SCALE).astype(BF16)
    for h in range(NSA_KV_HEADS):
        cols = slice(h * NSA_HD, (h + 1) * NSA_HD)
        ksr_ref[0, :, cols] = _rope(_head_norm(ks_ref[0, :, cols], gks_ref[...]), cos_t, sin_t)
        kwr_ref[0, :, cols] = _rope(_head_norm(kw_ref[0, :, cols], gkw_ref[...]), cos_t, sin_t)


def _col_spec(rows, width, col_block):
    assert (col_block * LANE) % width == 0
    cb = col_block * LANE // width
    return pl.BlockSpec((1, rows, width), lambda bi, i: (bi, i, cb))


def nsa_prep(p, pos, lp):
    b, t, _ = p.shape
    tp = min(t, 512)
    cos_t, sin_t = _rope_tables(pos)
    row = lambda w: pl.BlockSpec((1, tp, w), lambda bi, i: (bi, i, 0))
    kv_w = NSA_KV // LANE
    tab = pl.BlockSpec((tp, NSA_HD), lambda bi, i: (i, 0))
    gain = pl.BlockSpec((1, NSA_HD), lambda bi, i: (0, 0))
    return pl.pallas_call(
        _nsa_prep_body,
        grid=(b, t // tp),
        in_specs=[_col_spec(tp, NSA_Q, CB_Q), _col_spec(tp, NSA_KV, CB_KV + 2 * kv_w),
                  _col_spec(tp, NSA_KV, CB_KV + 4 * kv_w), tab, tab, gain, gain, gain],
        out_specs=[row(NSA_Q), row(NSA_Q), row(NSA_KV), row(NSA_KV)],
        out_shape=[jax.ShapeDtypeStruct((b, t, NSA_Q), BF16), jax.ShapeDtypeStruct((b, t, NSA_Q), BF16),
                   jax.ShapeDtypeStruct((b, t, NSA_KV), F32), jax.ShapeDtypeStruct((b, t, NSA_KV), F32)],
        compiler_params=pltpu.CompilerParams(dimension_semantics=("arbitrary", "arbitrary"),
                                             vmem_limit_bytes=VMEM_LIMIT_BYTES),
        name="nsa_prep",
    )(p, p, p, cos_t, sin_t, lp['nsa_q_norm'].reshape(1, NSA_HD),
      lp['nsa_k_norm_slc'].reshape(1, NSA_HD), lp['nsa_k_norm_win'].reshape(1, NSA_HD))


def _group_queries(q, g):
    return jnp.concatenate(
        [q[:, (g * NSA_HPG + h) * NSA_HD:(g * NSA_HPG + h + 1) * NSA_HD] for h in range(NSA_HPG)], axis=0)


def _store_group(o_ref, o, g, tq):
    for h in range(NSA_HPG):
        c0 = (g * NSA_HPG + h) * NSA_HD
        o_ref[0, :, c0:c0 + NSA_HD] = o[h * tq:(h + 1) * tq]


def _cmp_select_body(q_ref, kcc_ref, vcc_ref, ovl_ref, oc_ref, sel_ref, *, tq, nc, ns, selw):
    ncp = kcc_ref.shape[2]
    q0 = pl.program_id(1) * tq
    q = q_ref[0]
    tpos = q0 + lax.broadcasted_iota(jnp.int32, (tq, ncp), 0)
    cidx = lax.broadcasted_iota(jnp.int32, (tq, ncp), 1)
    vis = (cidx * CMP_STRIDE + (CMP_LEN - 1) <= tpos) & (cidx < nc)
    jrow = lax.broadcasted_iota(jnp.int32, (ns, tq), 0)
    cur = (q0 + lax.broadcasted_iota(jnp.int32, (ns, tq), 1)) // SLC_BLK
    valid = jrow <= cur
    forced = (jrow == 0) | (jrow == cur) | (jrow == cur - 1)
    sel_parts = []
    for g in range(NSA_KV_HEADS):
        s = lax.dot_general(_group_queries(q, g), kcc_ref[0, g], NT_DIMS, preferred_element_type=F32)
        sm = jnp.where(vis[None], s.reshape(NSA_HPG, tq, ncp), NEG_INF)
        m = jnp.max(sm, axis=-1, keepdims=True)
        m = jnp.where(m > NEG_INF, m, 0.0)
        p = jnp.exp(sm - m)
        p = p / jnp.maximum(jnp.sum(p, axis=-1, keepdims=True), 1e-30)
        o = jnp.dot(p.reshape(NSA_HPG * tq, ncp).astype(BF16), vcc_ref[0, g], preferred_element_type=F32)
        _store_group(oc_ref, o, g, tq)
        imp = p[0]
        for h in range(1, NSA_HPG):
            imp = imp + p[h]
        hi = imp.astype(BF16)
        lo = (imp - hi.astype(F32)).astype(BF16)
        score = (lax.dot_general(ovl_ref[...], hi, NT_DIMS, preferred_element_type=F32)
                 + lax.dot_general(ovl_ref[...], lo, NT_DIMS, preferred_element_type=F32))
        work = jnp.where(valid, jnp.where(forced, FORCE_SCORE, score), NEG_INF)
        chosen = jnp.zeros((ns, tq), F32)
        for _ in range(min(SLC_TOPN, ns)):
            mx = jnp.max(work, axis=0, keepdims=True)
            first = jnp.min(jnp.where(work == mx, jrow, ns), axis=0, keepdims=True)
            hit = jrow == first
            chosen = jnp.where(hit & (mx > NEG_INF), 1.0, chosen)
            work = jnp.where(hit, NEG_INF, work)
        sel_parts.append(chosen)
    if selw > NSA_KV_HEADS * ns:
        sel_parts.append(jnp.zeros((selw - NSA_KV_HEADS * ns, tq), F32))
    sel_ref[0] = jnp.concatenate(sel_parts, axis=0).T.astype(BF16)


def _sel_width(ns):
    return -(-NSA_KV_HEADS * ns // 128) * 128


def cmp_select_prompt(qn, kcc, vcc, nc):
    b, s, _ = qn.shape
    ncp = kcc.shape[2]
    assert ncp % LANE == 0
    ns = -(-s // SLC_BLK)
    selw = _sel_width(ns)
    tq = min(s, 256)
    cstart = np.arange(ncp) * CMP_STRIDE
    sstart = np.arange(ns) * SLC_BLK
    ovl = ((cstart[None, :] < sstart[:, None] + SLC_BLK) & (cstart[None, :] + CMP_LEN > sstart[:, None])
           & (np.arange(ncp)[None, :] < nc))
    ovl_t = jnp.asarray(ovl, dtype=BF16)
    kv_spec = pl.BlockSpec((1, NSA_KV_HEADS, ncp, NSA_HD), lambda bi, i: (bi, 0, 0, 0))
    return pl.pallas_call(
        functools.partial(_cmp_select_body, tq=tq, nc=nc, ns=ns, selw=selw),
        grid=(b, s // tq),
        in_specs=[pl.BlockSpec((1, tq, NSA_Q), lambda bi, i: (bi, i, 0)), kv_spec, kv_spec,
                  pl.BlockSpec((ns, ncp), lambda bi, i: (0, 0))],
        out_specs=[pl.BlockSpec((1, tq, NSA_Q), lambda bi, i: (bi, i, 0)),
                   pl.BlockSpec((1, tq, selw), lambda bi, i: (bi, i, 0))],
        out_shape=[jax.ShapeDtypeStruct((b, s, NSA_Q), F32), jax.ShapeDtypeStruct((b, s, selw), BF16)],
        compiler_params=pltpu.CompilerParams(dimension_semantics=("arbitrary", "arbitrary"),
                                             vmem_limit_bytes=VMEM_LIMIT_BYTES),
        name="cmp_select_prompt",
    )(qn, kcc, vcc, ovl_t)


SLC_TQ, SLC_TK = 256, 512


def _slc_body(q_ref, sel_ref, k_ref, v_ref, o_ref, *, tq, tk, ns):
    selw = sel_ref.shape[2]
    q0 = pl.program_id(1) * tq
    n_kv = (q0 + tq + tk - 1) // tk
    q = q_ref[0]
    selm = sel_ref[0]
    qpos = q0 + lax.broadcasted_iota(jnp.int32, (tq, tk), 0)
    kio = lax.broadcasted_iota(jnp.int32, (tq, tk), 1)
    erow = lax.broadcasted_iota(jnp.int32, (selw, tk), 0)
    ecol = lax.broadcasted_iota(jnp.int32, (selw, tk), 1)
    for g in range(NSA_KV_HEADS):
        q4 = _group_queries(q, g)
        gcols = slice(g * NSA_HD, (g + 1) * NSA_HD)

        def body(j, carry, q4=q4, gcols=gcols, g=g):
            m, l, acc = carry
            k0 = pl.multiple_of(j * tk, tk)
            kt = k_ref[0, pl.ds(k0, tk), gcols]
            vt = v_ref[0, pl.ds(k0, tk), gcols]
            s = lax.dot_general(q4, kt, NT_DIMS, preferred_element_type=F32).reshape(NSA_HPG, tq, tk)
            expand = jnp.where(erow - g * ns == (ecol + k0) // SLC_BLK, 1.0, 0.0).astype(BF16)
            chosen = jnp.dot(selm, expand, preferred_element_type=F32)
            allowed = ((chosen > 0.5) & (kio + k0 <= qpos))[None]
            m_new = jnp.maximum(m, jnp.max(jnp.where(allowed, s, NEG_BIG), axis=-1, keepdims=True))
            p = jnp.where(allowed, jnp.exp(s - m_new), 0.0)
            alpha = jnp.exp(m - m_new)
            l = alpha * l + jnp.sum(p, axis=-1, keepdims=True)
            pv = jnp.dot(p.reshape(NSA_HPG * tq, tk).astype(BF16), vt, preferred_element_type=F32)
            acc = alpha.reshape(NSA_HPG * tq, 1) * acc + pv
            return m_new, l, acc

        init = (jnp.full((NSA_HPG, tq, 1), NEG_BIG, F32), jnp.zeros((NSA_HPG, tq, 1), F32),
                jnp.zeros((NSA_HPG * tq, NSA_HD), F32))
        _, l, acc = lax.fori_loop(0, n_kv, body, init)
        _store_group(o_ref, acc / jnp.maximum(l.reshape(NSA_HPG * tq, 1), 1e-30), g, tq)


def slc_attention_prompt(qr, sel, k, v):
    b, s, _ = qr.shape
    selw = sel.shape[2]
    ns = -(-s // SLC_BLK)
    tq = min(s, SLC_TQ)
    tk = min(s, SLC_TK)
    kv_spec = pl.BlockSpec((1, s, NSA_KV), lambda bi, i: (bi, 0, 0))
    return pl.pallas_call(
        functools.partial(_slc_body, tq=tq, tk=tk, ns=ns),
        grid=(b, s // tq),
        in_specs=[pl.BlockSpec((1, tq, NSA_Q), lambda bi, i: (bi, i, 0)),
                  pl.BlockSpec((1, tq, selw), lambda bi, i: (bi, i, 0)), kv_spec, kv_spec],
        out_specs=pl.BlockSpec((1, tq, NSA_Q), lambda bi, i: (bi, i, 0)),
        out_shape=jax.ShapeDtypeStruct((b, s, NSA_Q), F32),
        compiler_params=pltpu.CompilerParams(dimension_semantics=("arbitrary", "arbitrary"),
                                             vmem_limit_bytes=VMEM_LIMIT_BYTES),
        name="slc_attention_prompt",
    )(qr, sel, k, v)


def _win_body(q_ref, k_ref, v_ref, o_ref, *, tq, span):
    q0 = pl.program_id(1) * tq
    kstart = pl.multiple_of(jnp.maximum(q0 - WINDOW, 0), tq)
    q = q_ref[0]
    qpos = q0 + lax.broadcasted_iota(jnp.int32, (tq, span), 0)
    kpos = kstart + lax.broadcasted_iota(jnp.int32, (tq, span), 1)
    allowed = ((kpos <= qpos) & (kpos > qpos - WINDOW))[None]
    for g in range(NSA_KV_HEADS):
        gcols = slice(g * NSA_HD, (g + 1) * NSA_HD)
        kt = k_ref[0, pl.ds(kstart, span), gcols]
        vt = v_ref[0, pl.ds(kstart, span), gcols]
        s = lax.dot_general(_group_queries(q, g), kt, NT_DIMS, preferred_element_type=F32)
        sm = jnp.where(allowed, s.reshape(NSA_HPG, tq, span), NEG_INF)
        p = jnp.exp(sm - jnp.max(sm, axis=-1, keepdims=True))
        l = jnp.sum(p, axis=-1, keepdims=True)
        o = jnp.dot(p.reshape(NSA_HPG * tq, span).astype(BF16), vt, preferred_element_type=F32)
        _store_group(o_ref, o / jnp.maximum(l.reshape(NSA_HPG * tq, 1), 1e-30), g, tq)


def win_attention_prompt(qr, k, v):
    b, s, _ = qr.shape
    tq = min(s, 256)
    span = min(s, WINDOW + tq)
    assert WINDOW % tq == 0 or s == tq
    kv_spec = pl.BlockSpec((1, s, NSA_KV), lambda bi, i: (bi, 0, 0))
    return pl.pallas_call(
        functools.partial(_win_body, tq=tq, span=span),
        grid=(b, s // tq),
        in_specs=[pl.BlockSpec((1, tq, NSA_Q), lambda bi, i: (bi, i, 0)), kv_spec, kv_spec],
        out_specs=pl.BlockSpec((1, tq, NSA_Q), lambda bi, i: (bi, i, 0)),
        out_shape=jax.ShapeDtypeStruct((b, s, NSA_Q), F32),
        compiler_params=pltpu.CompilerParams(dimension_semantics=("arbitrary", "arbitrary"),
                                             vmem_limit_bytes=VMEM_LIMIT_BYTES),
        name="win_attention_prompt",
    )(qr, k, v)


CMP_PAGES_PER_STEP = 16
SUB_PER_PAGE = PAGE_SIZE // CMP_STRIDE
PAGE_ROWS = PAGE_SIZE * NSA_KV_HEADS
CMP_R = CMP_LEN // CMP_STRIDE
SUBLANES = 8
Q_ROWS = SUBLANES


def _pad_rows(x, rows):
    return jnp.concatenate([x, jnp.zeros((rows - x.shape[0], x.shape[1]), x.dtype)], axis=0)


def _token_group_queries(q, g):
    rows = [q[:, (g * NSA_HPG + h) * NSA_HD:(g * NSA_HPG + h + 1) * NSA_HD] for h in range(NSA_HPG)]
    return _pad_rows(jnp.concatenate(rows, axis=0), Q_ROWS)


def _subblock_weight(w1):
    w = w1.reshape(CMP_R, CMP_STRIDE, NSA_HD, NSA_HD)
    return w.transpose(1, 2, 0, 3).reshape(CMP_STRIDE * NSA_HD, CMP_R * NSA_HD).astype(BF16)


def _pe_rows(pe):
    return _pad_rows(pe.reshape(CMP_R, CMP_STRIDE * NSA_HD), Q_ROWS).astype(BF16)


def _compress_from_parts(parts, pe_rows, w_sub, w2):
    return _compress_with_pe(parts, jnp.dot(pe_rows, w_sub, preferred_element_type=F32), w2)


def _compress_with_pe(parts, pe_proj, w2):
    n = parts.shape[0]
    hid = parts[:, :NSA_HD]
    for r in range(1, CMP_R):
        hid = hid + pltpu.roll(parts[:, r * NSA_HD:(r + 1) * NSA_HD], n - r, 0)
    for r in range(CMP_R):
        hid = hid + pe_proj[r:r + 1, r * NSA_HD:(r + 1) * NSA_HD]
    act = hid * jax.nn.sigmoid(hid)
    return jnp.dot(act.astype(BF16), w2, preferred_element_type=F32)


def _cmp_sample_body(pt_ref, *refs, q_pos, nc):
    npg = CMP_PAGES_PER_STEP
    k_pages, v_pages = refs[:npg], refs[npg:2 * npg]
    (wk_ref, wv_ref, pek_ref, pev_ref, w2k_ref, w2v_ref, gk_ref, q_ref,
     oc_ref, imp_ref, pk_ref, pv_ref, regroup_k_ref, regroup_v_ref, pe_ref) = refs[2 * npg:]
    c = pl.program_id(1)
    rows = npg * SUB_PER_PAGE

    @pl.when((pl.program_id(0) == 0) & (c == 0))
    def _():
        pe_ref[0] = jnp.dot(pek_ref[...], wk_ref[...], preferred_element_type=F32)
        pe_ref[1] = jnp.dot(pev_ref[...], wv_ref[...], preferred_element_type=F32)

    def project(pages, w_ref, parts_ref, regroup_ref):
        per_sub = CMP_STRIDE * NSA_KV_HEADS
        assert per_sub % SUBLANES == 0
        for pi, p in enumerate(pages):
            for r0 in range(0, PAGE_ROWS, SUBLANES):
                nn, sg0 = r0 // per_sub, r0 % per_sub
                regroup_ref[pi, pl.ds(sg0 * SUB_PER_PAGE + nn, SUBLANES, stride=SUB_PER_PAGE), :] = (
                    p[0, r0:r0 + SUBLANES, :])
        tile = lambda pi, s, g: regroup_ref[pi, (s * NSA_KV_HEADS + g) * SUB_PER_PAGE:
                                            (s * NSA_KV_HEADS + g + 1) * SUB_PER_PAGE, :]
        xg = jnp.concatenate(
            [jnp.concatenate(
                [jnp.concatenate([tile(pi, s, g) for s in range(CMP_STRIDE)], axis=1) for pi in range(len(pages))],
                axis=0)
             for g in range(NSA_KV_HEADS)], axis=0).astype(BF16)
        parts = jnp.dot(xg, w_ref[...], preferred_element_type=F32)
        for g in range(NSA_KV_HEADS):
            parts_ref[g, pl.ds(pl.multiple_of(c * rows, rows), rows), :] = parts[g * rows:(g + 1) * rows]

    project(k_pages, wk_ref, pk_ref, regroup_k_ref)
    project(v_pages, wv_ref, pv_ref, regroup_v_ref)

    @pl.when(c == pl.num_programs(1) - 1)
    def _():
        ncp = pk_ref.shape[1]
        q = q_ref[0]
        cidx = lax.broadcasted_iota(jnp.int32, (Q_ROWS, ncp), 1)
        vis = (cidx < nc) & (cidx * CMP_STRIDE + (CMP_LEN - 1) <= q_pos)
        for g in range(NSA_KV_HEADS):
            kcc = _head_norm(_compress_with_pe(pk_ref[g], pe_ref[0], w2k_ref[...]), gk_ref[...])
            vcc = _compress_with_pe(pv_ref[g], pe_ref[1], w2v_ref[...])
            s = lax.dot_general(_token_group_queries(q, g), kcc.astype(BF16), NT_DIMS, preferred_element_type=F32)
            sm = jnp.where(vis, s, NEG_INF)
            m = jnp.max(sm, axis=-1, keepdims=True)
            m = jnp.where(m > NEG_INF, m, 0.0)
            p = jnp.exp(sm - m)
            p = p / jnp.maximum(jnp.sum(p, axis=-1, keepdims=True), 1e-30)
            o = jnp.dot(p.astype(BF16), vcc.astype(BF16), preferred_element_type=F32)
            for h in range(NSA_HPG):
                c0 = (g * NSA_HPG + h) * NSA_HD
                oc_ref[0, :, c0:c0 + NSA_HD] = o[h:h + 1]
            imp_ref[0, g:g + 1, :] = jnp.sum(p[:NSA_HPG], axis=0, keepdims=True)


def cmp_attention_sample(qn, pool_k, pool_v, page_table, lp, q_pos):
    db, n_pages = page_table.shape
    n_pool = pool_k.shape[0]
    npg = CMP_PAGES_PER_STEP
    assert n_pages % npg == 0
    n_sub = n_pages * SUB_PER_PAGE
    nc = n_sub - CMP_R + 1
    view = lambda p: p.reshape(n_pool, PAGE_ROWS, NSA_HD)
    page_spec = lambda i: pl.BlockSpec((1, PAGE_ROWS, NSA_HD), lambda b, c, pt: (pt[b, c * npg + i], 0, 0))
    const = lambda shape: pl.BlockSpec(shape, lambda b, c, pt: (0,) * len(shape))
    kdim = CMP_STRIDE * NSA_HD
    grid_spec = pltpu.PrefetchScalarGridSpec(
        num_scalar_prefetch=1,
        grid=(db, n_pages // npg),
        in_specs=([page_spec(i) for i in range(npg)] + [page_spec(i) for i in range(npg)]
                  + [const((kdim, CMP_R * NSA_HD)), const((kdim, CMP_R * NSA_HD)),
                     const((Q_ROWS, kdim)), const((Q_ROWS, kdim)),
                     const((NSA_HD, NSA_HD)), const((NSA_HD, NSA_HD)), const((1, NSA_HD)),
                     pl.BlockSpec((1, 1, NSA_Q), lambda b, c, pt: (b, 0, 0))]),
        out_specs=[pl.BlockSpec((1, 1, NSA_Q), lambda b, c, pt: (b, 0, 0)),
                   pl.BlockSpec((1, NSA_KV_HEADS, n_sub), lambda b, c, pt: (b, 0, 0))],
        scratch_shapes=[pltpu.VMEM((NSA_KV_HEADS, n_sub, CMP_R * NSA_HD), F32),
                        pltpu.VMEM((NSA_KV_HEADS, n_sub, CMP_R * NSA_HD), F32),
                        pltpu.VMEM((npg, PAGE_ROWS, NSA_HD), F32), pltpu.VMEM((npg, PAGE_ROWS, NSA_HD), F32),
                        pltpu.VMEM((2, Q_ROWS, CMP_R * NSA_HD), F32)],
    )
    return pl.pallas_call(
        functools.partial(_cmp_sample_body, q_pos=q_pos, nc=nc),
        grid_spec=grid_spec,
        out_shape=[jax.ShapeDtypeStruct((db, 1, NSA_Q), F32), jax.ShapeDtypeStruct((db, NSA_KV_HEADS, n_sub), F32)],
        compiler_params=pltpu.CompilerParams(dimension_semantics=("arbitrary", "arbitrary"),
                                             vmem_limit_bytes=VMEM_LIMIT_BYTES),
        name="cmp_attention_sample",
    )(page_table, *([view(pool_k)] * npg), *([view(pool_v)] * npg),
      _subblock_weight(lp['cmp_w1_k']), _subblock_weight(lp['cmp_w1_v']),
      _pe_rows(lp['cmp_pe_k']), _pe_rows(lp['cmp_pe_v']),
      lp['cmp_w2_k'].astype(BF16), lp['cmp_w2_v'].astype(BF16),
      lp['nsa_k_norm_cmp'].reshape(1, NSA_HD), qn), nc


def _cmp_kv_prompt_body(*refs, n_sub):
    kc_refs, vc_refs = refs[:NSA_KV_HEADS], refs[NSA_KV_HEADS:2 * NSA_KV_HEADS]
    wk_ref, wv_ref, pek_ref, pev_ref, w2k_ref, w2v_ref, gk_ref, kcc_ref, vcc_ref = refs[2 * NSA_KV_HEADS:]

    def compress(x_ref, w_ref, pe_ref, w2_ref):
        x = jnp.concatenate([x_ref[0, pl.ds(s, n_sub, stride=CMP_STRIDE), :] for s in range(CMP_STRIDE)],
                            axis=1).astype(BF16)
        parts = jnp.dot(x, w_ref[...], preferred_element_type=F32)
        return _compress_from_parts(parts, pe_ref[...], w_ref[...], w2_ref[...])

    for g in range(NSA_KV_HEADS):
        kcc = _head_norm(compress(kc_refs[g], wk_ref, pek_ref, w2k_ref), gk_ref[...])
        kcc_ref[0, g] = kcc.astype(kcc_ref.dtype)
        vcc_ref[0, g] = compress(vc_refs[g], wv_ref, pev_ref, w2v_ref).astype(vcc_ref.dtype)


def compressed_kv_prompt(p, lp):
    b, s, _ = p.shape
    assert s % CMP_STRIDE == 0
    n_sub = s // CMP_STRIDE
    nc = n_sub - CMP_R + 1
    kdim = CMP_STRIDE * NSA_HD
    kv_w = NSA_KV // LANE
    const = lambda shape: pl.BlockSpec(shape, lambda bi: (0,) * len(shape))
    assert NSA_HD == LANE
    src = lambda cb: pl.BlockSpec((1, s, NSA_HD), lambda bi: (bi, 0, cb))
    out = pl.BlockSpec((1, NSA_KV_HEADS, n_sub, NSA_HD), lambda bi: (bi, 0, 0, 0))
    kcc, vcc = pl.pallas_call(
        functools.partial(_cmp_kv_prompt_body, n_sub=n_sub),
        grid=(b,),
        in_specs=[src(CB_KV + g) for g in range(NSA_KV_HEADS)] + [src(CB_KV + kv_w + g) for g in range(NSA_KV_HEADS)] + [
                  const((kdim, CMP_R * NSA_HD)), const((kdim, CMP_R * NSA_HD)),
                  const((Q_ROWS, kdim)), const((Q_ROWS, kdim)),
                  const((NSA_HD, NSA_HD)), const((NSA_HD, NSA_HD)), const((1, NSA_HD))],
        out_specs=[out, out],
        out_shape=[jax.ShapeDtypeStruct((b, NSA_KV_HEADS, n_sub, NSA_HD), BF16)] * 2,
        compiler_params=pltpu.CompilerParams(dimension_semantics=("arbitrary",),
                                             vmem_limit_bytes=VMEM_LIMIT_BYTES),
        name="compressed_kv_prompt",
    )(*([p] * (2 * NSA_KV_HEADS)), _subblock_weight(lp['cmp_w1_k']), _subblock_weight(lp['cmp_w1_v']),
      _pe_rows(lp['cmp_pe_k']), _pe_rows(lp['cmp_pe_v']),
      lp['cmp_w2_k'].astype(BF16), lp['cmp_w2_v'].astype(BF16), lp['nsa_k_norm_cmp'].reshape(1, NSA_HD))
    return kcc, vcc, nc


def _gdn_sample_body(qkv_ref, buf_ref, w_ref, sm_ref, z_ref, rec_ref, alog_ref, dt_ref, gn_ref,
                     o_ref, conv_ref, rec_out_ref):
    x = qkv_ref[0]
    buf = buf_ref[0]
    w = w_ref[...]
    acc = x * w[GDN_CONV - 1:GDN_CONV]
    for j in range(GDN_CONV - 1):
        acc = acc + buf[j:j + 1] * w[j:j + 1]
    for j in range(GDN_CONV - 2):
        conv_ref[0, j:j + 1, :] = buf[j + 1:j + 2]
    conv_ref[0, GDN_CONV - 2:GDN_CONV - 1, :] = x
    cv = acc * jax.nn.sigmoid(acc)
    g_all, beta_all = _gdn_gates(sm_ref[0], alog_ref[...], dt_ref[...])
    z = z_ref[0]
    rows = []
    heads = []
    for h in range(GDN_HEADS):
        q = cv[:, h * GDN_DK:(h + 1) * GDN_DK]
        k = cv[:, GDN_QK + h * GDN_DK:GDN_QK + (h + 1) * GDN_DK]
        v = cv[:, 2 * GDN_QK + h * GDN_DV:2 * GDN_QK + (h + 1) * GDN_DV]
        q = q * lax.rsqrt(jnp.sum(q * q, axis=-1, keepdims=True) + EPS) * (GDN_DK ** -0.5)
        k = k * lax.rsqrt(jnp.sum(k * k, axis=-1, keepdims=True) + EPS)
        g = g_all[:, SMALL_A + h:SMALL_A + h + 1]
        beta = beta_all[:, SMALL_B + h:SMALL_B + h + 1]
        eg = jnp.exp(g)
        heads.append((q, k, v * beta, eg, jnp.sum(q * k, axis=-1, keepdims=True)))
        rows += [q * eg, k * beta * eg, k]
    stacked = jnp.concatenate(rows, axis=0)
    eye = jnp.where(lax.broadcasted_iota(jnp.int32, (GDN_DK, GDN_DK), 0)
                    == lax.broadcasted_iota(jnp.int32, (GDN_DK, GDN_DK), 1), 1.0, 0.0).astype(BF16)
    cols = sum(lax.dot_general(eye, piece, NT_DIMS, preferred_element_type=F32) for piece in _split3(stacked))
    for h, (q, k, u, eg, attn) in enumerate(heads):
        s_old = rec_ref[0, h]
        qe_col, w_col, k_col = (cols[:, 3 * h + i:3 * h + i + 1] for i in range(3))
        v_new = u - jnp.sum(w_col * s_old, axis=0, keepdims=True)
        o = jnp.sum(qe_col * s_old, axis=0, keepdims=True) + attn * v_new
        rec_out_ref[0, h] = s_old * eg + k_col * v_new
        zc = z[:, h * GDN_DV:(h + 1) * GDN_DV]
        o_ref[0, :, h * GDN_DV:(h + 1) * GDN_DV] = _head_norm(o, gn_ref[...]) * (zc * jax.nn.sigmoid(zc))


def gdn_sample(p, conv_buf, rec, lp):
    db = p.shape[0]
    tok = lambda w, cb: pl.BlockSpec((1, 1, w), lambda b: (b, 0, cb * LANE // w))
    vec = pl.BlockSpec((1, LANE), lambda b: (0, 0))
    buf = pl.BlockSpec((1, GDN_CONV - 1, GDN_CONV_DIM), lambda b: (b, 0, 0))
    st = pl.BlockSpec((1, GDN_HEADS, GDN_DK, GDN_DV), lambda b: (b, 0, 0, 0))
    assert CB_QKV == 0 and (CB_Z * LANE) % GDN_V == 0
    return pl.pallas_call(
        _gdn_sample_body,
        grid=(db,),
        in_specs=[tok(GDN_CONV_DIM, CB_QKV), buf, pl.BlockSpec((GDN_CONV, GDN_CONV_DIM), lambda b: (0, 0)),
                  tok(LANE, CB_SMALL), tok(GDN_V, CB_Z), st, vec, vec, vec],
        out_specs=[pl.BlockSpec((1, 1, GDN_V), lambda b: (b, 0, 0)), buf, st],
        out_shape=[jax.ShapeDtypeStruct((db, 1, GDN_V), F32),
                   jax.ShapeDtypeStruct((db, GDN_CONV - 1, GDN_CONV_DIM), F32),
                   jax.ShapeDtypeStruct((db, GDN_HEADS, GDN_DK, GDN_DV), F32)],
        compiler_params=pltpu.CompilerParams(dimension_semantics=("arbitrary",),
                                             vmem_limit_bytes=VMEM_LIMIT_BYTES),
        name="gdn_sample",
    )(p, conv_buf, lp['gdn_conv_w'], p, p, rec,
      _lane_row(lp['gdn_a_log'], SMALL_A), _lane_row(lp['gdn_dt_bias'], SMALL_A),
      lp['gdn_out_norm'].reshape(1, GDN_DV))


def _merge_body(x_ref, oa_ref, oc_ref, os_ref, ow_ref, ga_ref, gb_ref, sm_ref, wo_ref, y_ref):
    gates = jax.nn.sigmoid(sm_ref[...])
    o_b = jnp.zeros(x_ref.shape, F32)
    for br, ref in enumerate((oc_ref, os_ref, ow_ref)):
        o = ref[...]
        o_b = o_b + jnp.concatenate(
            [gates[:, SMALL_G + br * NSA_HEADS + h:SMALL_G + br * NSA_HEADS + h + 1] * o[:, h * NSA_HD:(h + 1) * NSA_HD]
             for h in range(NSA_HEADS)], axis=1)
    m = jax.nn.sigmoid(ga_ref[...]) * oa_ref[...] + jax.nn.sigmoid(gb_ref[...]) * o_b
    y_ref[...] = x_ref[...] + jnp.dot(m.astype(BF16), wo_ref[...], preferred_element_type=F32)


def merge_branches_fused(x2, p2, o_a, o_c, o_s, o_w, w_o):
    n = x2.shape[0]
    tm = min(n, 512)
    row = pl.BlockSpec((tm, D_MODEL), lambda i: (i, 0))
    pcol = lambda w, cb: pl.BlockSpec((tm, w), lambda i: (i, cb * LANE // w))
    return pl.pallas_call(
        _merge_body,
        grid=(n // tm,),
        in_specs=[row, row, row, row, row, pcol(D_MODEL, CB_GA), pcol(D_MODEL, CB_GB), pcol(LANE, CB_SMALL),
                  _const_spec((D_MODEL, D_MODEL))],
        out_specs=row,
        out_shape=jax.ShapeDtypeStruct((n, D_MODEL), F32),
        compiler_params=pltpu.CompilerParams(dimension_semantics=("arbitrary",),
                                             vmem_limit_bytes=VMEM_LIMIT_BYTES),
        name="merge_branches",
    )(x2, o_a, o_c, o_s, o_w, p2, p2, p2, w_o.astype(BF16))


IDX_LANES = 128


def _select_sample_body(imp_ref, ovl_ref, idx_ref, *, q_pos, ns):
    imp = imp_ref[...]
    hi = imp.astype(BF16)
    lo = (imp - hi.astype(F32)).astype(BF16)
    score = (jnp.dot(hi, ovl_ref[...], preferred_element_type=F32)
             + jnp.dot(lo, ovl_ref[...], preferred_element_type=F32))
    rows, nsp = score.shape
    j = lax.broadcasted_iota(jnp.int32, (rows, nsp), 1)
    cur = q_pos // SLC_BLK
    valid = (j <= cur) & (j < ns)
    forced = (j == 0) | (j == cur) | (j == cur - 1)
    work = jnp.where(valid, jnp.where(forced, FORCE_SCORE, score), NEG_INF)
    slot = lax.broadcasted_iota(jnp.int32, (rows, IDX_LANES), 1)
    idx = jnp.full((rows, IDX_LANES), -1, jnp.int32)
    for it in range(min(SLC_TOPN, ns)):
        mx = jnp.max(work, axis=-1, keepdims=True)
        first = jnp.min(jnp.where(work == mx, j, nsp), axis=-1, keepdims=True)
        idx = jnp.where((slot == it) & (mx > NEG_INF), first, idx)
        work = jnp.where(j == first, NEG_INF, work)
    idx_ref[...] = idx


def select_blocks_sample(imp, nc, q_pos, n_keys):
    db, kv, ncp = imp.shape
    ns = -(-n_keys // SLC_BLK)
    nsp = -(-ns // 128) * 128
    cstart = np.arange(ncp) * CMP_STRIDE
    sstart = np.arange(nsp) * SLC_BLK
    ovl = ((cstart[:, None] < sstart[None, :] + SLC_BLK) & (cstart[:, None] + CMP_LEN > sstart[None, :])
           & (np.arange(ncp)[:, None] < nc) & (np.arange(nsp)[None, :] < ns))
    rows = db * kv
    idx = pl.pallas_call(
        functools.partial(_select_sample_body, q_pos=q_pos, ns=ns),
        grid=(1,),
        in_specs=[pl.BlockSpec((rows, ncp), lambda i: (0, 0)), pl.BlockSpec((ncp, nsp), lambda i: (0, 0))],
        out_specs=pl.BlockSpec((rows, IDX_LANES), lambda i: (0, 0)),
        out_shape=jax.ShapeDtypeStruct((rows, IDX_LANES), jnp.int32),
        compiler_params=pltpu.CompilerParams(vmem_limit_bytes=VMEM_LIMIT_BYTES),
        name="select_blocks_sample",
    )(imp.reshape(rows, ncp), jnp.asarray(ovl, dtype=BF16))
    return idx.reshape(db, kv, IDX_LANES)


def _token_softmax_attend(q8, s_past, allowed, v_past, k_new, v_new, new_ok):
    s_new = jnp.sum(q8.astype(F32) * k_new, axis=-1, keepdims=True)
    sm = jnp.where(allowed, s_past, NEG_INF)
    m = jnp.maximum(jnp.max(sm, axis=-1, keepdims=True), jnp.where(new_ok, s_new, NEG_INF))
    m = jnp.where(m > NEG_INF, m, 0.0)
    p = jnp.exp(sm - m)
    p_new = jnp.where(new_ok, jnp.exp(s_new - m), 0.0)
    l = jnp.sum(p, axis=-1, keepdims=True) + p_new
    o = jnp.dot(p.astype(BF16), v_past, preferred_element_type=F32) + p_new * v_new
    return o / jnp.maximum(l, 1e-30)


def _slc_sample_body(idx_ref, pt_ref, *refs, n_past):
    n = SLC_TOPN
    k_blocks, v_blocks = refs[:n], refs[n:2 * n]
    q_ref, kn_ref, vn_ref, o_ref = refs[2 * n:]
    b, g = pl.program_id(0), pl.program_id(1)
    q8 = _token_group_queries_half(q_ref[0])

    def group_rows(blocks):
        sel_g = jnp.full((SLC_TOPN * SLC_BLK, NSA_HD), g, jnp.int32)
        per_group = [jnp.concatenate([r[0, pl.ds(gg, SLC_BLK, stride=NSA_KV_HEADS), :] for r in blocks], axis=0)
                     for gg in range(NSA_KV_HEADS)]
        out = per_group[0]
        for gg in range(1, NSA_KV_HEADS):
            out = jnp.where(sel_g == gg, per_group[gg], out)
        return out.astype(BF16)

    k = group_rows(k_blocks)
    v = group_rows(v_blocks)
    s = lax.dot_general(q8, k, NT_DIMS, preferred_element_type=F32)
    slot = lax.broadcasted_iota(jnp.int32, s.shape, 1) // SLC_BLK
    slot_ok = jnp.zeros(s.shape, jnp.int32)
    n_new = jnp.int32(0)
    for i in range(n):
        ji = idx_ref[(b * NSA_KV_HEADS + g) * n + i]
        slot_ok = jnp.where(slot == i, jnp.where((ji >= 0) & (ji < n_past), 1, 0), slot_ok)
        n_new = n_new + jnp.where(ji >= n_past, 1, 0)
    new_ok = jnp.full((Q_ROWS, 1), n_new, jnp.int32) > 0
    o = _token_softmax_attend(q8, s, slot_ok > 0, v, kn_ref[0], vn_ref[0], new_ok)
    for h in range(NSA_HPG):
        o_ref[0, :, h * NSA_HD:(h + 1) * NSA_HD] = o[h:h + 1]


def _token_group_queries_half(q):
    rows = [q[:, h * NSA_HD:(h + 1) * NSA_HD] for h in range(NSA_HPG)]
    return _pad_rows(jnp.concatenate(rows, axis=0), Q_ROWS)


def slc_attention_sample(qr, idx, pool_k, pool_v, page_table, k_new, v_new):
    db, n_pages = page_table.shape
    n_pool = pool_k.shape[0]
    bpp = PAGE_SIZE // SLC_BLK
    n_past = n_pages * bpp
    n = SLC_TOPN
    gw = NSA_HPG * NSA_HD
    idx_flat = idx[:, :, :n].reshape(-1)
    pt_flat = page_table.reshape(-1)
    view = lambda p: p.reshape(n_pool * bpp, SLC_BLK * NSA_KV_HEADS, NSA_HD)

    def blk_spec(i):
        def index_map(b, g, idx_s, pt_s):
            jc = jnp.clip(idx_s[(b * NSA_KV_HEADS + g) * n + i], 0, n_past - 1)
            return (pt_s[b * n_pages + jc // bpp] * bpp + jc % bpp, 0, 0)
        return pl.BlockSpec((1, SLC_BLK * NSA_KV_HEADS, NSA_HD), index_map)

    grid_spec = pltpu.PrefetchScalarGridSpec(
        num_scalar_prefetch=2,
        grid=(db, NSA_KV_HEADS),
        in_specs=([blk_spec(i) for i in range(n)] + [blk_spec(i) for i in range(n)]
                  + [pl.BlockSpec((1, 1, gw), lambda b, g, *_: (b, 0, g)),
                     pl.BlockSpec((1, 1, NSA_HD), lambda b, g, *_: (b, 0, g)),
                     pl.BlockSpec((1, 1, NSA_HD), lambda b, g, *_: (b, 0, g))]),
        out_specs=pl.BlockSpec((1, 1, gw), lambda b, g, *_: (b, 0, g)),
    )
    return pl.pallas_call(
        functools.partial(_slc_sample_body, n_past=n_past),
        grid_spec=grid_spec,
        out_shape=jax.ShapeDtypeStruct((db, 1, NSA_Q), F32),
        compiler_params=pltpu.CompilerParams(dimension_semantics=("arbitrary", "arbitrary"),
                                             vmem_limit_bytes=VMEM_LIMIT_BYTES),
        name="slc_attention_sample",
    )(idx_flat, pt_flat, *([view(pool_k)] * n), *([view(pool_v)] * n), qr, k_new, v_new)


def _win_sample_body(q_ref, bk_ref, bv_ref, kn_ref, vn_ref, o_ref, nk_ref, nv_ref, *, q_pos, buf_start):
    rows = bk_ref.shape[1]
    wb = rows // NSA_KV_HEADS
    kn, vn = kn_ref[0], vn_ref[0]
    q = q_ref[0]
    kpos = buf_start + lax.broadcasted_iota(jnp.int32, (Q_ROWS, wb), 1)
    allowed = (kpos <= q_pos) & (kpos > q_pos - WINDOW)
    new_ok = jnp.ones((Q_ROWS, 1), jnp.bool_)
    for g in range(NSA_KV_HEADS):
        gcols = slice(g * NSA_HD, (g + 1) * NSA_HD)
        q8 = _token_group_queries(q, g)
        kg = bk_ref[0, pl.ds(g, wb, stride=NSA_KV_HEADS), :].astype(BF16)
        vg = bv_ref[0, pl.ds(g, wb, stride=NSA_KV_HEADS), :].astype(BF16)
        s = lax.dot_general(q8, kg, NT_DIMS, preferred_element_type=F32)
        o = _token_softmax_attend(q8, s, allowed, vg, kn[:, gcols], vn[:, gcols], new_ok)
        for h in range(NSA_HPG):
            c0 = (g * NSA_HPG + h) * NSA_HD
            o_ref[0, :, c0:c0 + NSA_HD] = o[h:h + 1]
    row = lax.broadcasted_iota(jnp.int32, (rows, NSA_HD), 0)

    def shifted(buf_ref, new):
        out = pltpu.roll(buf_ref[0], rows - NSA_KV_HEADS, 0)
        for g in range(NSA_KV_HEADS):
            out = jnp.where(row == rows - NSA_KV_HEADS + g, new[:, g * NSA_HD:(g + 1) * NSA_HD], out)
        return out

    nk_ref[0] = shifted(bk_ref, kn)
    nv_ref[0] = shifted(bv_ref, vn)


def win_attention_sample(qr, buf_k, buf_v, k_new, v_new, q_pos):
    db, rows, _ = buf_k.shape
    wb = rows // NSA_KV_HEADS
    tok = lambda w: pl.BlockSpec((1, 1, w), lambda b: (b, 0, 0))
    buf = pl.BlockSpec((1, rows, NSA_HD), lambda b: (b, 0, 0))
    return pl.pallas_call(
        functools.partial(_win_sample_body, q_pos=q_pos, buf_start=q_pos - wb),
        grid=(db,),
        in_specs=[tok(NSA_Q), buf, buf, tok(NSA_KV), tok(NSA_KV)],
        out_specs=[tok(NSA_Q), buf, buf],
        out_shape=[jax.ShapeDtypeStruct((db, 1, NSA_Q), F32), jax.ShapeDtypeStruct((db, rows, NSA_HD), F32),
                   jax.ShapeDtypeStruct((db, rows, NSA_HD), F32)],
        compiler_params=pltpu.CompilerParams(dimension_semantics=("arbitrary",),
                                             vmem_limit_bytes=VMEM_LIMIT_BYTES),
        name="win_attention_sample",
    )(qr, buf_k, buf_v, k_new, v_new)


def _kv_cols(p, i):
    c0 = CB_KV * LANE + i * NSA_KV
    return p[..., c0:c0 + NSA_KV]


def token_mix_prompt(x, lp):
    b, s, _ = x.shape
    pos = jnp.arange(s)
    x2 = x.reshape(b * s, D_MODEL)
    p2 = mixer_projection(x2, lp['mix_norm'], lp['w_in'])
    p = p2.reshape(b, s, P_COLS)
    o_a, rec_new = gdn_prompt(p, lp)
    assert s >= GDN_CONV - 1
    conv_new = p[:, s - (GDN_CONV - 1):, :GDN_CONV_DIM]
    heads = lambda a_: a_.reshape(b, -1, NSA_KV_HEADS, NSA_HD)
    qn, qr, ks_r, kw_r = nsa_prep(p, pos, lp)
    kcc, vcc, nc = compressed_kv_prompt(p, lp)
    o_c, sel = cmp_select_prompt(qn, kcc, vcc, nc)
    vs, vw = _kv_cols(p, 3), _kv_cols(p, 5)
    o_s = slc_attention_prompt(qr, sel, ks_r.astype(BF16), vs.astype(BF16))
    o_w = win_attention_prompt(qr, kw_r.astype(BF16), vw.astype(BF16))
    flat = lambda a_: a_.reshape(b * s, -1)
    y = merge_branches_fused(x2, p2, flat(o_a), flat(o_c), flat(o_s), flat(o_w), lp['w_o']).reshape(b, s, D_MODEL)
    wb = min(WINDOW, s)
    return y, (conv_new, rec_new, heads(_kv_cols(p, 0)), heads(_kv_cols(p, 1)), heads(ks_r), heads(vs),
               heads(kw_r[:, s - wb:]), heads(vw[:, s - wb:]))


def token_mix_sample(x, layer, conv_buf, rec, cache_k_cmp, cache_v_cmp, cache_k_slc, cache_v_slc,
                     win_k, win_v, page_table, lp):
    db, t, _ = x.shape
    assert t == 1 and t < CMP_STRIDE
    q_pos = PAST_LEN
    x2 = x.reshape(db, D_MODEL)
    p2 = mixer_projection(x2, lp['mix_norm'], lp['w_in'])
    p = p2.reshape(db, 1, P_COLS)
    o_a, conv_new, rec_new = gdn_sample(p, conv_buf, rec, lp)
    heads = lambda a_: a_.reshape(db, -1, NSA_KV_HEADS, NSA_HD)
    qn, qr, ks_r, kw_r = [a_.reshape(db, 1, -1) for a_ in
                          nsa_prep(p2.reshape(1, db, P_COLS), jnp.full((db,), q_pos), lp)]
    (o_c, imp), nc = cmp_attention_sample(qn, cache_k_cmp[layer], cache_v_cmp[layer], page_table, lp, q_pos)
    idx = select_blocks_sample(imp, nc, q_pos, PAST_LEN + t)
    vs, vw = _kv_cols(p, 3), _kv_cols(p, 5)
    o_s = slc_attention_sample(qr, idx, cache_k_slc[layer], cache_v_slc[layer], page_table, ks_r, vs)
    wb = win_k.shape[1]
    o_w, win_k_new, win_v_new = win_attention_sample(
        qr, win_k.reshape(db, wb * NSA_KV_HEADS, NSA_HD), win_v.reshape(db, wb * NSA_KV_HEADS, NSA_HD),
        kw_r, vw, q_pos)
    flat = lambda a_: a_.reshape(db, -1)
    y = merge_branches_fused(x2, p2, flat(o_a), flat(o_c), flat(o_s), flat(o_w), lp['w_o']).reshape(db, 1, D_MODEL)
    return y, (conv_new, rec_new, heads(_kv_cols(p, 0)), heads(_kv_cols(p, 1)), heads(ks_r), heads(vs),
               heads(win_k_new), heads(win_v_new))


def kernel(x_prompt, x_sample, state_gdn_conv, state_gdn_rec, cache_k_cmp, cache_v_cmp, cache_k_slc, cache_v_slc, cache_k_win, cache_v_win, page_table, ffn1_norm, ffn1_w_in, ffn1_w_out, mix_norm, w_in, gdn_conv_w, gdn_a_log, gdn_dt_bias, gdn_out_norm, nsa_q_norm, nsa_k_norm_cmp, nsa_k_norm_slc, nsa_k_norm_win, cmp_w1_k, cmp_w2_k, cmp_pe_k, cmp_w1_v, cmp_w2_v, cmp_pe_v, w_o, ffn2_norm, ffn2_w_in, ffn2_w_out):
    depth = w_in.shape[0]
    yp, ys = x_prompt, x_sample
    p_list, s_list = [], []
    for l in range(depth):
        lp = {'mix_norm': mix_norm[l], 'w_in': w_in[l], 'gdn_conv_w': gdn_conv_w[l],
              'gdn_a_log': gdn_a_log[l], 'gdn_dt_bias': gdn_dt_bias[l], 'gdn_out_norm': gdn_out_norm[l],
              'nsa_q_norm': nsa_q_norm[l], 'nsa_k_norm_cmp': nsa_k_norm_cmp[l],
              'nsa_k_norm_slc': nsa_k_norm_slc[l], 'nsa_k_norm_win': nsa_k_norm_win[l],
              'cmp_w1_k': cmp_w1_k[l], 'cmp_w2_k': cmp_w2_k[l], 'cmp_pe_k': cmp_pe_k[l],
              'cmp_w1_v': cmp_w1_v[l], 'cmp_w2_v': cmp_w2_v[l], 'cmp_pe_v': cmp_pe_v[l], 'w_o': w_o[l]}
        yp = swiglu_half(yp, ffn1_norm[l], ffn1_w_in[l], ffn1_w_out[l])
        yp, st_p = token_mix_prompt(yp, lp)
        yp = swiglu_half(yp, ffn2_norm[l], ffn2_w_in[l], ffn2_w_out[l])
        ys = swiglu_half(ys, ffn1_norm[l], ffn1_w_in[l], ffn1_w_out[l])
        ys, st_s = token_mix_sample(ys, l, state_gdn_conv[l], state_gdn_rec[l], cache_k_cmp, cache_v_cmp,
                                    cache_k_slc, cache_v_slc, cache_k_win[l], cache_v_win[l], page_table, lp)
        ys = swiglu_half(ys, ffn2_norm[l], ffn2_w_in[l], ffn2_w_out[l])
        p_list.append(st_p)
        s_list.append(st_s)
    (p_conv, p_rec, p_kc, p_vc, p_ks, p_vs, p_kw, p_vw) = [jnp.stack(a) for a in zip(*p_list)]
    (s_conv, s_rec, s_kc, s_vc, s_ks, s_vs, s_kw, s_vw) = [jnp.stack(a) for a in zip(*s_list)]
    return (yp, ys, p_conv, p_rec, p_kc, p_vc, p_ks, p_vs, p_kw, p_vw,
            s_conv, s_rec, s_kc, s_vc, s_ks, s_vs, s_kw, s_vw)
```

```python
import functools
import math

import numpy as np
import jax
import jax.numpy as jnp
from jax import lax
from jax.experimental import pallas as pl
from jax.experimental.pallas import tpu as pltpu

D_MODEL = 1024
PAST_LEN = 8192
PAGE_SIZE = 128

GDN_DK = 128
GDN_DV = 128
GDN_HEADS = D_MODEL // GDN_DV
GDN_QK = GDN_HEADS * GDN_DK
GDN_V = GDN_HEADS * GDN_DV
GDN_CONV_DIM = 2 * GDN_QK + GDN_V
GDN_CONV = 4
GDN_CHUNK = 64

NSA_HD = 128
NSA_HEADS = D_MODEL // NSA_HD
NSA_KV_HEADS = NSA_HEADS // 4
NSA_HPG = NSA_HEADS // NSA_KV_HEADS
NSA_Q = NSA_HEADS * NSA_HD
NSA_KV = NSA_KV_HEADS * NSA_HD
CMP_LEN = 32
CMP_STRIDE = 16
SLC_BLK = 64
SLC_TOPN = 16
WINDOW = 512
Q_BLOCK = 128
ATTN_SCALE = NSA_HD ** -0.5
FORCE_SCORE = 1e6

ROPE_DIM = NSA_HD // 4
ROPE_THETA = 500000.0

D_FF = ((8 * D_MODEL // 3 + 127) // 128) * 128
EPS = 1e-6

IN_SIZES = (GDN_CONV_DIM, GDN_HEADS, GDN_HEADS, GDN_V,
            NSA_Q, NSA_KV, NSA_KV, NSA_KV, NSA_KV, NSA_KV, NSA_KV, 3 * NSA_HEADS,
            D_MODEL, D_MODEL)

MXU_N = 256
VMEM_LIMIT_BYTES = 56 * 1024 * 1024

BF16 = jnp.bfloat16
F32 = jnp.float32


def _const_spec(shape):
    return pl.BlockSpec(shape, lambda *_: (0,) * len(shape), pipeline_mode=pl.Buffered(1))


def _ffn_body(x_ref, g_ref, wg_ref, wu_ref, wo_ref, o_ref):
    x = x_ref[...]
    ms = jnp.mean(x * x, axis=-1, keepdims=True)
    h = (x * lax.rsqrt(ms + EPS) * g_ref[...]).astype(BF16)
    acc = jnp.zeros(x.shape, F32)
    for c in range(D_FF // MXU_N):
        cols = slice(c * MXU_N, (c + 1) * MXU_N)
        gate = jnp.dot(h, wg_ref[:, cols], preferred_element_type=F32)
        up = jnp.dot(h, wu_ref[:, cols], preferred_element_type=F32)
        act = (gate * jax.nn.sigmoid(gate) * up).astype(BF16)
        acc = acc + jnp.dot(act, wo_ref[cols, :], preferred_element_type=F32)
    o_ref[...] = x + 0.5 * acc


def _ffn_tile(n_rows):
    return min(n_rows, 512)


def swiglu_half(x, g, w_in, w_out):
    shape = x.shape
    x2 = x.reshape(-1, D_MODEL)
    n = x2.shape[0]
    tm = _ffn_tile(n)
    wg = w_in[:, :D_FF].astype(BF16)
    wu = w_in[:, D_FF:].astype(BF16)
    wo = w_out.astype(BF16)
    out = pl.pallas_call(
        _ffn_body,
        grid=(n // tm,),
        in_specs=[pl.BlockSpec((tm, D_MODEL), lambda i: (i, 0)),
                  _const_spec((1, D_MODEL)),
                  _const_spec((D_MODEL, D_FF)),
                  _const_spec((D_MODEL, D_FF)),
                  _const_spec((D_FF, D_MODEL))],
        out_specs=pl.BlockSpec((tm, D_MODEL), lambda i: (i, 0)),
        out_shape=jax.ShapeDtypeStruct((n, D_MODEL), F32),
        compiler_params=pltpu.CompilerParams(dimension_semantics=("arbitrary",),
                                             vmem_limit_bytes=VMEM_LIMIT_BYTES),
        name="swiglu_half",
    )(x2, g.reshape(1, D_MODEL), wg, wu, wo)
    return out.reshape(shape)


LANE = 128
CB_QKV = 0
CB_Z = CB_QKV + GDN_CONV_DIM // LANE
CB_Q = CB_Z + GDN_V // LANE
CB_GA = CB_Q + NSA_Q // LANE
CB_GB = CB_GA + D_MODEL // LANE
CB_KV = CB_GB + D_MODEL // LANE
CB_SMALL = CB_KV + 6 * NSA_KV // LANE
P_COLS = (CB_SMALL + 1) * LANE
SMALL_A, SMALL_B, SMALL_G = 0, GDN_HEADS, 2 * GDN_HEADS
PROJ_CHUNKS = 3
assert (P_COLS // LANE) % PROJ_CHUNKS == 0 and 2 * GDN_HEADS + 3 * NSA_HEADS <= LANE


def _permuted_w_in(w_in):
    offs = [int(o) for o in np.cumsum((0,) + IN_SIZES)]
    seg = lambda i: w_in[:, offs[i]:offs[i + 1]]
    w = jnp.concatenate([seg(0), seg(3), seg(4), seg(12), seg(13)] + [seg(i) for i in range(5, 11)]
                        + [seg(1), seg(2), seg(11)], axis=1)
    return jnp.pad(w, ((0, 0), (0, P_COLS - w.shape[1]))).astype(BF16)


def _proj_body(x_ref, g_ref, w_ref, o_ref):
    x = x_ref[...]
    ms = jnp.mean(x * x, axis=-1, keepdims=True)
    h = (x * lax.rsqrt(ms + EPS) * g_ref[...]).astype(BF16)
    o_ref[...] = jnp.dot(h, w_ref[...], preferred_element_type=F32)


def mixer_projection(x2, g, w_in):
    n = x2.shape[0]
    tm = min(n, 512)
    wc = P_COLS // PROJ_CHUNKS
    return pl.pallas_call(
        _proj_body,
        grid=(PROJ_CHUNKS, n // tm),
        in_specs=[pl.BlockSpec((tm, D_MODEL), lambda c, i: (i, 0)),
                  pl.BlockSpec((1, D_MODEL), lambda c, i: (0, 0)),
                  pl.BlockSpec((D_MODEL, wc), lambda c, i: (0, c))],
        out_specs=pl.BlockSpec((tm, wc), lambda c, i: (i, c)),
        out_shape=jax.ShapeDtypeStruct((n, P_COLS), F32),
        compiler_params=pltpu.CompilerParams(dimension_semantics=("arbitrary", "arbitrary"),
                                             vmem_limit_bytes=VMEM_LIMIT_BYTES),
        name="mixer_projection",
    )(x2, g.reshape(1, D_MODEL), _permuted_w_in(w_in))


GDN_TILE = 256
GDN_HB = 4
GDN_SPLIT_LEVELS = 3
TN_DIMS = (((0,), (0,)), ((), ()))


def _split3(x):
    p1 = x.astype(BF16)
    r1 = x - p1.astype(F32)
    p2 = r1.astype(BF16)
    p3 = (r1 - p2.astype(F32)).astype(BF16)
    return p1, p2, p3


def _gdn_gates(small, alog_row, dt_row):
    xa = small + dt_row
    softplus = jnp.maximum(xa, 0.0) + jnp.log(1.0 + jnp.exp(-jnp.abs(xa)))
    return -jnp.exp(alog_row) * softplus, jax.nn.sigmoid(small)


def _lane_col(x, lane_idx):
    lane = lax.broadcasted_iota(jnp.int32, x.shape, 1)
    return jnp.sum(jnp.where(lane == lane_idx, x, 0.0), axis=1, keepdims=True)


def _gdn_prompt_body(q_ref, k_ref, v_ref, z_ref, sm_ref, wq_ref, wk_ref, wv_ref, alog_ref, dt_ref, gn_ref,
                     o_ref, rec_ref, s_ref, pq_ref, pk_ref, pv_ref, *, tile, hb):
    c = GDN_CHUNK
    hblk, step = pl.program_id(1), pl.program_id(2)
    width = hb * GDN_DK

    @pl.when(step == 0)
    def _():
        s_ref[...] = jnp.zeros(s_ref.shape, F32)
        for r in (pq_ref, pk_ref, pv_ref):
            r[...] = jnp.zeros(r.shape, F32)

    row8 = lax.broadcasted_iota(jnp.int32, (8, width), 0)

    def conv(x_ref, prev_ref, w_ref):
        x, prev, w = x_ref[0], prev_ref[...], w_ref[...]
        acc = x * w[GDN_CONV - 1:GDN_CONV]
        for j in range(1, GDN_CONV):
            main = pltpu.roll(x, j, 0)
            first = jnp.where(row8 < j, pltpu.roll(prev, j, 0), main[:8])
            acc = acc + jnp.concatenate([first, main[8:]], axis=0) * w[GDN_CONV - 1 - j:GDN_CONV - j]
        prev_ref[...] = x[tile - 8:]
        return acc * jax.nn.sigmoid(acc)

    qc, kc, vc = conv(q_ref, pq_ref, wq_ref), conv(k_ref, pk_ref, wk_ref), conv(v_ref, pv_ref, wv_ref)
    z = z_ref[0]
    g_all, beta_all = _gdn_gates(sm_ref[0], alog_ref[...], dt_ref[...])

    n = hb * c
    ii = lax.broadcasted_iota(jnp.int32, (n, n), 0)
    jj = lax.broadcasted_iota(jnp.int32, (n, n), 1)
    same_head = (ii // c) == (jj // c)
    incl, strict = same_head & (ii >= jj), same_head & (ii > jj)
    ci_ = lax.broadcasted_iota(jnp.int32, (c, c), 0)
    cj_ = lax.broadcasted_iota(jnp.int32, (c, c), 1)
    tri = jnp.where(ci_ >= cj_, 1.0, 0.0).astype(BF16)
    er = lax.broadcasted_iota(jnp.int32, (8, LANE), 0)
    el = lax.broadcasted_iota(jnp.int32, (8, LANE), 1)
    head_rows = jnp.where((el == SMALL_A + hblk * hb + er) & (er < hb), 1.0, 0.0).astype(BF16)
    dot = functools.partial(jnp.dot, preferred_element_type=F32)
    stack = lambda f: jnp.concatenate([f(hh) for hh in range(hb)], axis=0)
    states = [s_ref[hh] for hh in range(hb)]

    for ci in range(tile // c):
        rows = slice(ci * c, (ci + 1) * c)
        gc_all = sum(dot(tri, piece) for piece in _split3(g_all[rows]))
        gc_rows = sum(lax.dot_general(head_rows, piece, NT_DIMS, preferred_element_type=F32)
                      for piece in _split3(gc_all))
        head_cols = lambda x: stack(lambda hh: x[rows, hh * GDN_DK:(hh + 1) * GDN_DK])
        gcol = stack(lambda hh: _lane_col(gc_all, SMALL_A + hblk * hb + hh))
        bcol = stack(lambda hh: _lane_col(beta_all[rows], SMALL_B + hblk * hb + hh))
        grow = jnp.concatenate([gc_rows[hh:hh + 1] for hh in range(hb)], axis=1)
        g_last = stack(lambda hh: jnp.broadcast_to(gcol[hh * c + c - 1:hh * c + c], (c, 1)))
        q, k, v = head_cols(qc), head_cols(kc), head_cols(vc)
        q = q * lax.rsqrt(jnp.sum(q * q, axis=-1, keepdims=True) + EPS) * (GDN_DK ** -0.5)
        k = k * lax.rsqrt(jnp.sum(k * k, axis=-1, keepdims=True) + EPS)
        decay = jnp.where(incl, jnp.exp(jnp.where(incl, gcol - grow, 0.0)), 0.0)
        kb = k * bcol
        k16 = k.astype(BF16)
        m = jnp.where(strict, lax.dot_general(kb.astype(BF16), k16, NT_DIMS, preferred_element_type=F32) * decay, 0.0)
        egc = jnp.exp(gcol)
        rhs = jnp.concatenate([v * bcol, kb * egc], axis=1)
        mp = m
        sol = rhs
        for level in range(int(math.log2(c))):
            a1, a2, _ = _split3(mp)
            s1, s2, _ = _split3(sol)
            step_term = dot(a1, s1)
            if level < GDN_SPLIT_LEVELS:
                step_term = step_term + dot(a1, s2) + dot(a2, s1)
            sol = sol - step_term if level == 0 else sol + step_term
            if level + 1 < int(math.log2(c)):
                mp = dot(a1, a1)
        u, w = sol[:, :GDN_DV], sol[:, GDN_DV:]
        attn = jnp.where(incl, lax.dot_general(q.astype(BF16), k16, NT_DIMS, preferred_element_type=F32) * decay, 0.0)
        qe = q * egc
        k_dec = k * jnp.exp(g_last - gcol)
        s_decay = jnp.exp(g_last)
        v_new, qs = [], []
        for hh in range(hb):
            r = slice(hh * c, (hh + 1) * c)
            s16 = states[hh].astype(BF16)
            ws_qs = dot(jnp.concatenate([w[r], qe[r]], axis=0).astype(BF16), s16)
            v_new.append(u[r] - ws_qs[:c])
            qs.append(ws_qs[c:])
            states[hh] = states[hh] * s_decay[hh * c:hh * c + 1] + lax.dot_general(
                k_dec[r].astype(BF16), v_new[hh].astype(BF16), TN_DIMS, preferred_element_type=F32)
        o = jnp.concatenate(qs, axis=0) + dot(attn.astype(BF16), jnp.concatenate(v_new, axis=0).astype(BF16))
        for hh in range(hb):
            cols = slice(hh * GDN_DK, (hh + 1) * GDN_DK)
            zc = z[rows, cols]
            o_ref[0, rows, cols] = _head_norm(o[hh * c:(hh + 1) * c], gn_ref[...]) * (zc * jax.nn.sigmoid(zc))

    for hh in range(hb):
        s_ref[hh] = states[hh]

    @pl.when(step == pl.num_programs(2) - 1)
    def _():
        rec_ref[0] = s_ref[...]


def _lane_row(vec, offset):
    return jnp.pad(vec.astype(F32), (offset, LANE - offset - vec.shape[0])).reshape(1, LANE)


def gdn_prompt(p, lp):
    b, s, _ = p.shape
    tile, hb = GDN_TILE, GDN_HB
    assert s % tile == 0 and tile % GDN_CHUNK == 0 and GDN_HEADS % hb == 0 and GDN_CHUNK & (GDN_CHUNK - 1) == 0
    width = hb * GDN_DK
    wb = width // LANE
    col = lambda cb0: pl.BlockSpec((1, tile, width), lambda bi, h, i: (bi, i, cb0 // wb + h))
    wcol = lambda cb0: pl.BlockSpec((GDN_CONV, width), lambda bi, h, i: (0, cb0 // wb + h))
    vec = pl.BlockSpec((1, LANE), lambda bi, h, i: (0, 0))
    qk_blocks = GDN_QK // LANE
    return pl.pallas_call(
        functools.partial(_gdn_prompt_body, tile=tile, hb=hb),
        grid=(b, GDN_HEADS // hb, s // tile),
        in_specs=[col(CB_QKV), col(CB_QKV + qk_blocks), col(CB_QKV + 2 * qk_blocks), col(CB_Z),
                  pl.BlockSpec((1, tile, LANE), lambda bi, h, i: (bi, i, CB_SMALL)),
                  wcol(0), wcol(qk_blocks), wcol(2 * qk_blocks), vec, vec, vec],
        out_specs=[pl.BlockSpec((1, tile, width), lambda bi, h, i: (bi, i, h)),
                   pl.BlockSpec((1, hb, GDN_DK, GDN_DV), lambda bi, h, i: (bi, h, 0, 0))],
        out_shape=[jax.ShapeDtypeStruct((b, s, GDN_V), F32),
                   jax.ShapeDtypeStruct((b, GDN_HEADS, GDN_DK, GDN_DV), F32)],
        scratch_shapes=[pltpu.VMEM((hb, GDN_DK, GDN_DV), F32)] + [pltpu.VMEM((8, width), F32)] * 3,
        compiler_params=pltpu.CompilerParams(dimension_semantics=("arbitrary", "arbitrary", "arbitrary"),
                                             vmem_limit_bytes=VMEM_LIMIT_BYTES),
        name="gdn_prompt",
    )(p, p, p, p, p, lp['gdn_conv_w'], lp['gdn_conv_w'], lp['gdn_conv_w'],
      _lane_row(lp['gdn_a_log'], SMALL_A), _lane_row(lp['gdn_dt_bias'], SMALL_A),
      lp['gdn_out_norm'].reshape(1, GDN_DV))


NEG_INF = float("-inf")
NEG_BIG = -1e30
NT_DIMS = (((1,), (1,)), ((), ()))


def _rope_tables(pos):
    half = ROPE_DIM // 2
    inv = jnp.float32(ROPE_THETA) ** (-jnp.arange(half, dtype=jnp.float32) / half)
    ang = pos.astype(jnp.float32)[:, None] * inv
    cos, sin = jnp.cos(ang), jnp.sin(ang)
    rest = NSA_HD - ROPE_DIM
    cos_t = jnp.concatenate([cos, cos, jnp.ones((pos.shape[0], rest), F32)], axis=-1)
    sin_t = jnp.concatenate([-sin, sin, jnp.zeros((pos.shape[0], rest), F32)], axis=-1)
    return cos_t, sin_t


def _rope(y, cos_t, sin_t):
    half = ROPE_DIM // 2
    lane = lax.broadcasted_iota(jnp.int32, y.shape, 1)
    partner = jnp.where(lane < half, pltpu.roll(y, NSA_HD - half, 1), pltpu.roll(y, half, 1))
    return y * cos_t + partner * sin_t


def _head_norm(x, g):
    ms = jnp.mean(x * x, axis=-1, keepdims=True)
    return x * lax.rsqrt(ms + EPS) * g


def _nsa_prep_body(q_ref, ks_ref, kw_ref, cos_ref, sin_ref, gq_ref, gks_ref, gkw_ref,
                   qn_ref, qr_ref, ksr_ref, kwr_ref):
    cos_t, sin_t = cos_ref[...], sin_ref[...]
    for h in range(NSA_HEADS):
        cols = slice(h * NSA_HD, (h + 1) * NSA_HD)
        y = _head_norm(q_ref[0, :, cols], gq_ref[...])
        qn_ref[0, :, cols] = (y * ATTN_SCALE).astype(BF16)
        qr_ref[0, :, cols] = (_rope(y, cos_t, sin_t) * ATTN_SCALE).astype(BF16)
    for h in range(NSA_KV_HEADS):
        cols = slice(h * NSA_HD, (h + 1) * NSA_HD)
        ksr_ref[0, :, cols] = _rope(_head_norm(ks_ref[0, :, cols], gks_ref[...]), cos_t, sin_t)
        kwr_ref[0, :, cols] = _rope(_head_norm(kw_ref[0, :, cols], gkw_ref[...]), cos_t, sin_t)


def _col_spec(rows, width, col_block):
    assert (col_block * LANE) % width == 0
    cb = col_block * LANE // width
    return pl.BlockSpec((1, rows, width), lambda bi, i: (bi, i, cb))


def nsa_prep(p, pos, lp):
    b, t, _ = p.shape
    tp = min(t, 512)
    cos_t, sin_t = _rope_tables(pos)
    row = lambda w: pl.BlockSpec((1, tp, w), lambda bi, i: (bi, i, 0))
    kv_w = NSA_KV // LANE
    tab = pl.BlockSpec((tp, NSA_HD), lambda bi, i: (i, 0))
    gain = pl.BlockSpec((1, NSA_HD), lambda bi, i: (0, 0))
    return pl.pallas_call(
        _nsa_prep_body,
        grid=(b, t // tp),
        in_specs=[_col_spec(tp, NSA_Q, CB_Q), _col_spec(tp, NSA_KV, CB_KV + 2 * kv_w),
                  _col_spec(tp, NSA_KV, CB_KV + 4 * kv_w), tab, tab, gain, gain, gain],
        out_specs=[row(NSA_Q), row(NSA_Q), row(NSA_KV), row(NSA_KV)],
        out_shape=[jax.ShapeDtypeStruct((b, t, NSA_Q), BF16), jax.ShapeDtypeStruct((b, t, NSA_Q), BF16),
                   jax.ShapeDtypeStruct((b, t, NSA_KV), F32), jax.ShapeDtypeStruct((b, t, NSA_KV), F32)],
        compiler_params=pltpu.CompilerParams(dimension_semantics=("arbitrary", "arbitrary"),
                                             vmem_limit_bytes=VMEM_LIMIT_BYTES),
        name="nsa_prep",
    )(p, p, p, cos_t, sin_t, lp['nsa_q_norm'].reshape(1, NSA_HD),
      lp['nsa_k_norm_slc'].reshape(1, NSA_HD), lp['nsa_k_norm_win'].reshape(1, NSA_HD))


def _group_queries(q, g):
    return jnp.concatenate(
        [q[:, (g * NSA_HPG + h) * NSA_HD:(g * NSA_HPG + h + 1) * NSA_HD] for h in range(NSA_HPG)], axis=0)


def _store_group(o_ref, o, g, tq):
    for h in range(NSA_HPG):
        c0 = (g * NSA_HPG + h) * NSA_HD
        o_ref[0, :, c0:c0 + NSA_HD] = o[h * tq:(h + 1) * tq]


def _cmp_select_body(q_ref, kcc_ref, vcc_ref, ovl_ref, oc_ref, sel_ref, *, tq, nc, ns, selw):
    ncp = kcc_ref.shape[2]
    q0 = pl.program_id(1) * tq
    q = q_ref[0]
    tpos = q0 + lax.broadcasted_iota(jnp.int32, (tq, ncp), 0)
    cidx = lax.broadcasted_iota(jnp.int32, (tq, ncp), 1)
    vis = (cidx * CMP_STRIDE + (CMP_LEN - 1) <= tpos) & (cidx < nc)
    jrow = lax.broadcasted_iota(jnp.int32, (ns, tq), 0)
    cur = (q0 + lax.broadcasted_iota(jnp.int32, (ns, tq), 1)) // SLC_BLK
    valid = jrow <= cur
    forced = (jrow == 0) | (jrow == cur) | (jrow == cur - 1)
    sel_parts = []
    for g in range(NSA_KV_HEADS):
        s = lax.dot_general(_group_queries(q, g), kcc_ref[0, g], NT_DIMS, preferred_element_type=F32)
        sm = jnp.where(vis[None], s.reshape(NSA_HPG, tq, ncp), NEG_INF)
        m = jnp.max(sm, axis=-1, keepdims=True)
        m = jnp.where(m > NEG_INF, m, 0.0)
        p = jnp.exp(sm - m)
        p = p / jnp.maximum(jnp.sum(p, axis=-1, keepdims=True), 1e-30)
        o = jnp.dot(p.reshape(NSA_HPG * tq, ncp).astype(BF16), vcc_ref[0, g], preferred_element_type=F32)
        _store_group(oc_ref, o, g, tq)
        imp = p[0]
        for h in range(1, NSA_HPG):
            imp = imp + p[h]
        hi = imp.astype(BF16)
        lo = (imp - hi.astype(F32)).astype(BF16)
        score = (lax.dot_general(ovl_ref[...], hi, NT_DIMS, preferred_element_type=F32)
                 + lax.dot_general(ovl_ref[...], lo, NT_DIMS, preferred_element_type=F32))
        work = jnp.where(valid, jnp.where(forced, FORCE_SCORE, score), NEG_INF)
        chosen = jnp.zeros((ns, tq), F32)
        for _ in range(min(SLC_TOPN, ns)):
            mx = jnp.max(work, axis=0, keepdims=True)
            first = jnp.min(jnp.where(work == mx, jrow, ns), axis=0, keepdims=True)
            hit = jrow == first
            chosen = jnp.where(hit & (mx > NEG_INF), 1.0, chosen)
            work = jnp.where(hit, NEG_INF, work)
        sel_parts.append(chosen)
    if selw > NSA_KV_HEADS * ns:
        sel_parts.append(jnp.zeros((selw - NSA_KV_HEADS * ns, tq), F32))
    sel_ref[0] = jnp.concatenate(sel_parts, axis=0).T.astype(BF16)


def _sel_width(ns):
    return -(-NSA_KV_HEADS * ns // 128) * 128


def cmp_select_prompt(qn, kcc, vcc, nc):
    b, s, _ = qn.shape
    ncp = kcc.shape[2]
    assert ncp % LANE == 0
    ns = -(-s // SLC_BLK)
    selw = _sel_width(ns)
    tq = min(s, 256)
    cstart = np.arange(ncp) * CMP_STRIDE
    sstart = np.arange(ns) * SLC_BLK
    ovl = ((cstart[None, :] < sstart[:, None] + SLC_BLK) & (cstart[None, :] + CMP_LEN > sstart[:, None])
           & (np.arange(ncp)[None, :] < nc))
    ovl_t = jnp.asarray(ovl, dtype=BF16)
    kv_spec = pl.BlockSpec((1, NSA_KV_HEADS, ncp, NSA_HD), lambda bi, i: (bi, 0, 0, 0))
    return pl.pallas_call(
        functools.partial(_cmp_select_body, tq=tq, nc=nc, ns=ns, selw=selw),
        grid=(b, s // tq),
        in_specs=[pl.BlockSpec((1, tq, NSA_Q), lambda bi, i: (bi, i, 0)), kv_spec, kv_spec,
                  pl.BlockSpec((ns, ncp), lambda bi, i: (0, 0))],
        out_specs=[pl.BlockSpec((1, tq, NSA_Q), lambda bi, i: (bi, i, 0)),
                   pl.BlockSpec((1, tq, selw), lambda bi, i: (bi, i, 0))],
        out_shape=[jax.ShapeDtypeStruct((b, s, NSA_Q), F32), jax.ShapeDtypeStruct((b, s, selw), BF16)],
        compiler_params=pltpu.CompilerParams(dimension_semantics=("arbitrary", "arbitrary"),
                                             vmem_limit_bytes=VMEM_LIMIT_BYTES),
        name="cmp_select_prompt",
    )(qn, kcc, vcc, ovl_t)


SLC_TQ, SLC_TK = 256, 512


def _slc_body(q_ref, sel_ref, k_ref, v_ref, o_ref, *, tq, tk, ns):
    selw = sel_ref.shape[2]
    q0 = pl.program_id(1) * tq
    n_kv = (q0 + tq + tk - 1) // tk
    q = q_ref[0]
    selm = sel_ref[0]
    qpos = q0 + lax.broadcasted_iota(jnp.int32, (tq, tk), 0)
    kio = lax.broadcasted_iota(jnp.int32, (tq, tk), 1)
    erow = lax.broadcasted_iota(jnp.int32, (selw, tk), 0)
    ecol = lax.broadcasted_iota(jnp.int32, (selw, tk), 1)
    for g in range(NSA_KV_HEADS):
        q4 = _group_queries(q, g)
        gcols = slice(g * NSA_HD, (g + 1) * NSA_HD)

        def body(j, carry, q4=q4, gcols=gcols, g=g):
            m, l, acc = carry
            k0 = pl.multiple_of(j * tk, tk)
            kt = k_ref[0, pl.ds(k0, tk), gcols]
            vt = v_ref[0, pl.ds(k0, tk), gcols]
            s = lax.dot_general(q4, kt, NT_DIMS, preferred_element_type=F32).reshape(NSA_HPG, tq, tk)
            expand = jnp.where(erow - g * ns == (ecol + k0) // SLC_BLK, 1.0, 0.0).astype(BF16)
            chosen = jnp.dot(selm, expand, preferred_element_type=F32)
            allowed = ((chosen > 0.5) & (kio + k0 <= qpos))[None]
            m_new = jnp.maximum(m, jnp.max(jnp.where(allowed, s, NEG_BIG), axis=-1, keepdims=True))
            p = jnp.where(allowed, jnp.exp(s - m_new), 0.0)
            alpha = jnp.exp(m - m_new)
            l = alpha * l + jnp.sum(p, axis=-1, keepdims=True)
            pv = jnp.dot(p.reshape(NSA_HPG * tq, tk).astype(BF16), vt, preferred_element_type=F32)
            acc = alpha.reshape(NSA_HPG * tq, 1) * acc + pv
            return m_new, l, acc

        init = (jnp.full((NSA_HPG, tq, 1), NEG_BIG, F32), jnp.zeros((NSA_HPG, tq, 1), F32),
                jnp.zeros((NSA_HPG * tq, NSA_HD), F32))
        _, l, acc = lax.fori_loop(0, n_kv, body, init)
        _store_group(o_ref, acc / jnp.maximum(l.reshape(NSA_HPG * tq, 1), 1e-30), g, tq)


def slc_attention_prompt(qr, sel, k, v):
    b, s, _ = qr.shape
    selw = sel.shape[2]
    ns = -(-s // SLC_BLK)
    tq = min(s, SLC_TQ)
    tk = min(s, SLC_TK)
    kv_spec = pl.BlockSpec((1, s, NSA_KV), lambda bi, i: (bi, 0, 0))
    return pl.pallas_call(
        functools.partial(_slc_body, tq=tq, tk=tk, ns=ns),
        grid=(b, s // tq),
        in_specs=[pl.BlockSpec((1, tq, NSA_Q), lambda bi, i: (bi, i, 0)),
                  pl.BlockSpec((1, tq, selw), lambda bi, i: (bi, i, 0)), kv_spec, kv_spec],
        out_specs=pl.BlockSpec((1, tq, NSA_Q), lambda bi, i: (bi, i, 0)),
        out_shape=jax.ShapeDtypeStruct((b, s, NSA_Q), F32),
        compiler_params=pltpu.CompilerParams(dimension_semantics=("arbitrary", "arbitrary"),
                                             vmem_limit_bytes=VMEM_LIMIT_BYTES),
        name="slc_attention_prompt",
    )(qr, sel, k, v)


def _win_body(q_ref, k_ref, v_ref, o_ref, *, tq, span):
    q0 = pl.program_id(1) * tq
    kstart = pl.multiple_of(jnp.maximum(q0 - WINDOW, 0), tq)
    q = q_ref[0]
    qpos = q0 + lax.broadcasted_iota(jnp.int32, (tq, span), 0)
    kpos = kstart + lax.broadcasted_iota(jnp.int32, (tq, span), 1)
    allowed = ((kpos <= qpos) & (kpos > qpos - WINDOW))[None]
    for g in range(NSA_KV_HEADS):
        gcols = slice(g * NSA_HD, (g + 1) * NSA_HD)
        kt = k_ref[0, pl.ds(kstart, span), gcols]
        vt = v_ref[0, pl.ds(kstart, span), gcols]
        s = lax.dot_general(_group_queries(q, g), kt, NT_DIMS, preferred_element_type=F32)
        sm = jnp.where(allowed, s.reshape(NSA_HPG, tq, span), NEG_INF)
        p = jnp.exp(sm - jnp.max(sm, axis=-1, keepdims=True))
        l = jnp.sum(p, axis=-1, keepdims=True)
        o = jnp.dot(p.reshape(NSA_HPG * tq, span).astype(BF16), vt, preferred_element_type=F32)
        _store_group(o_ref, o / jnp.maximum(l.reshape(NSA_HPG * tq, 1), 1e-30), g, tq)


def win_attention_prompt(qr, k, v):
    b, s, _ = qr.shape
    tq = min(s, 256)
    span = min(s, WINDOW + tq)
    assert WINDOW % tq == 0 or s == tq
    kv_spec = pl.BlockSpec((1, s, NSA_KV), lambda bi, i: (bi, 0, 0))
    return pl.pallas_call(
        functools.partial(_win_body, tq=tq, span=span),
        grid=(b, s // tq),
        in_specs=[pl.BlockSpec((1, tq, NSA_Q), lambda bi, i: (bi, i, 0)), kv_spec, kv_spec],
        out_specs=pl.BlockSpec((1, tq, NSA_Q), lambda bi, i: (bi, i, 0)),
        out_shape=jax.ShapeDtypeStruct((b, s, NSA_Q), F32),
        compiler_params=pltpu.CompilerParams(dimension_semantics=("arbitrary", "arbitrary"),
                                             vmem_limit_bytes=VMEM_LIMIT_BYTES),
        name="win_attention_prompt",
    )(qr, k, v)


CMP_PAGES_PER_STEP = 32
SUB_PER_PAGE = PAGE_SIZE // CMP_STRIDE
PAGE_ROWS = PAGE_SIZE * NSA_KV_HEADS
CMP_R = CMP_LEN // CMP_STRIDE
SUBLANES = 8
Q_ROWS = SUBLANES


def _pad_rows(x, rows):
    return jnp.concatenate([x, jnp.zeros((rows - x.shape[0], x.shape[1]), x.dtype)], axis=0)


def _token_group_queries(q, g):
    rows = [q[:, (g * NSA_HPG + h) * NSA_HD:(g * NSA_HPG + h + 1) * NSA_HD] for h in range(NSA_HPG)]
    return _pad_rows(jnp.concatenate(rows, axis=0), Q_ROWS)


def _subblock_weight(w1):
    w = w1.reshape(CMP_R, CMP_STRIDE, NSA_HD, NSA_HD)
    return w.transpose(1, 2, 0, 3).reshape(CMP_STRIDE * NSA_HD, CMP_R * NSA_HD).astype(BF16)


def _pe_rows(pe):
    return _pad_rows(pe.reshape(CMP_R, CMP_STRIDE * NSA_HD), Q_ROWS).astype(BF16)


def _compress_from_parts(parts, pe_rows, w_sub, w2):
    return _compress_with_pe(parts, jnp.dot(pe_rows, w_sub, preferred_element_type=F32), w2)


def _compress_with_pe(parts, pe_proj, w2):
    n = parts.shape[0]
    hid = parts[:, :NSA_HD]
    for r in range(1, CMP_R):
        hid = hid + pltpu.roll(parts[:, r * NSA_HD:(r + 1) * NSA_HD], n - r, 0)
    for r in range(CMP_R):
        hid = hid + pe_proj[r:r + 1, r * NSA_HD:(r + 1) * NSA_HD]
    act = hid * jax.nn.sigmoid(hid)
    return jnp.dot(act.astype(BF16), w2, preferred_element_type=F32)


def _cmp_sample_body(pt_ref, *refs, q_pos, nc):
    npg = CMP_PAGES_PER_STEP
    k_pages, v_pages = refs[:npg], refs[npg:2 * npg]
    (wk_ref, wv_ref, pek_ref, pev_ref, w2k_ref, w2v_ref, gk_ref, q_ref,
     oc_ref, imp_ref, pk_ref, pv_ref, regroup_k_ref, regroup_v_ref, pe_ref) = refs[2 * npg:]
    c = pl.program_id(1)
    rows = npg * SUB_PER_PAGE

    @pl.when((pl.program_id(0) == 0) & (c == 0))
    def _():
        pe_ref[0] = jnp.dot(pek_ref[...], wk_ref[...], preferred_element_type=F32)
        pe_ref[1] = jnp.dot(pev_ref[...], wv_ref[...], preferred_element_type=F32)

    def project(pages, w_ref, parts_ref, regroup_ref):
        per_sub = CMP_STRIDE * NSA_KV_HEADS
        assert per_sub % SUBLANES == 0
        for pi, p in enumerate(pages):
            for r0 in range(0, PAGE_ROWS, SUBLANES):
                nn, sg0 = r0 // per_sub, r0 % per_sub
                regroup_ref[pi, pl.ds(sg0 * SUB_PER_PAGE + nn, SUBLANES, stride=SUB_PER_PAGE), :] = (
                    p[0, r0:r0 + SUBLANES, :])
        tile = lambda pi, s, g: regroup_ref[pi, (s * NSA_KV_HEADS + g) * SUB_PER_PAGE:
                                            (s * NSA_KV_HEADS + g + 1) * SUB_PER_PAGE, :]
        xg = jnp.concatenate(
            [jnp.concatenate(
                [jnp.concatenate([tile(pi, s, g) for s in range(CMP_STRIDE)], axis=1) for pi in range(len(pages))],
                axis=0)
             for g in range(NSA_KV_HEADS)], axis=0).astype(BF16)
        parts = jnp.dot(xg, w_ref[...], preferred_element_type=F32)
        for g in range(NSA_KV_HEADS):
            parts_ref[g, pl.ds(pl.multiple_of(c * rows, rows), rows), :] = parts[g * rows:(g + 1) * rows]

    project(k_pages, wk_ref, pk_ref, regroup_k_ref)
    project(v_pages, wv_ref, pv_ref, regroup_v_ref)

    @pl.when(c == pl.num_programs(1) - 1)
    def _():
        ncp = pk_ref.shape[1]
        q = q_ref[0]
        cidx = lax.broadcasted_iota(jnp.int32, (Q_ROWS, ncp), 1)
        vis = (cidx < nc) & (cidx * CMP_STRIDE + (CMP_LEN - 1) <= q_pos)
        for g in range(NSA_KV_HEADS):
            kcc = _head_norm(_compress_with_pe(pk_ref[g], pe_ref[0], w2k_ref[...]), gk_ref[...])
            vcc = _compress_with_pe(pv_ref[g], pe_ref[1], w2v_ref[...])
            s = lax.dot_general(_token_group_queries(q, g), kcc.astype(BF16), NT_DIMS, preferred_element_type=F32)
            sm = jnp.where(vis, s, NEG_INF)
            m = jnp.max(sm, axis=-1, keepdims=True)
            m = jnp.where(m > NEG_INF, m, 0.0)
            p = jnp.exp(sm - m)
            p = p / jnp.maximum(jnp.sum(p, axis=-1, keepdims=True), 1e-30)
            o = jnp.dot(p.astype(BF16), vcc.astype(BF16), preferred_element_type=F32)
            for h in range(NSA_HPG):
                c0 = (g * NSA_HPG + h) * NSA_HD
                oc_ref[0, :, c0:c0 + NSA_HD] = o[h:h + 1]
            imp_ref[0, g:g + 1, :] = jnp.sum(p[:NSA_HPG], axis=0, keepdims=True)


def cmp_attention_sample(qn, pool_k, pool_v, page_table, lp, q_pos):
    db, n_pages = page_table.shape
    n_pool = pool_k.shape[0]
    npg = CMP_PAGES_PER_STEP
    assert n_pages % npg == 0
    n_sub = n_pages * SUB_PER_PAGE
    nc = n_sub - CMP_R + 1
    view = lambda p: p.reshape(n_pool, PAGE_ROWS, NSA_HD)
    page_spec = lambda i: pl.BlockSpec((1, PAGE_ROWS, NSA_HD), lambda b, c, pt: (pt[b, c * npg + i], 0, 0))
    const = lambda shape: pl.BlockSpec(shape, lambda b, c, pt: (0,) * len(shape))
    kdim = CMP_STRIDE * NSA_HD
    grid_spec = pltpu.PrefetchScalarGridSpec(
        num_scalar_prefetch=1,
        grid=(db, n_pages // npg),
        in_specs=([page_spec(i) for i in range(npg)] + [page_spec(i) for i in range(npg)]
                  + [const((kdim, CMP_R * NSA_HD)), const((kdim, CMP_R * NSA_HD)),
                     const((Q_ROWS, kdim)), const((Q_ROWS, kdim)),
                     const((NSA_HD, NSA_HD)), const((NSA_HD, NSA_HD)), const((1, NSA_HD)),
                     pl.BlockSpec((1, 1, NSA_Q), lambda b, c, pt: (b, 0, 0))]),
        out_specs=[pl.BlockSpec((1, 1, NSA_Q), lambda b, c, pt: (b, 0, 0)),
                   pl.BlockSpec((1, NSA_KV_HEADS, n_sub), lambda b, c, pt: (b, 0, 0))],
        scratch_shapes=[pltpu.VMEM((NSA_KV_HEADS, n_sub, CMP_R * NSA_HD), F32),
                        pltpu.VMEM((NSA_KV_HEADS, n_sub, CMP_R * NSA_HD), F32),
                        pltpu.VMEM((npg, PAGE_ROWS, NSA_HD), F32), pltpu.VMEM((npg, PAGE_ROWS, NSA_HD), F32),
                        pltpu.VMEM((2, Q_ROWS, CMP_R * NSA_HD), F32)],
    )
    return pl.pallas_call(
        functools.partial(_cmp_sample_body, q_pos=q_pos, nc=nc),
        grid_spec=grid_spec,
        out_shape=[jax.ShapeDtypeStruct((db, 1, NSA_Q), F32), jax.ShapeDtypeStruct((db, NSA_KV_HEADS, n_sub), F32)],
        compiler_params=pltpu.CompilerParams(dimension_semantics=("arbitrary", "arbitrary"),
                                             vmem_limit_bytes=VMEM_LIMIT_BYTES),
        name="cmp_attention_sample",
    )(page_table, *([view(pool_k)] * npg), *([view(pool_v)] * npg),
      _subblock_weight(lp['cmp_w1_k']), _subblock_weight(lp['cmp_w1_v']),
      _pe_rows(lp['cmp_pe_k']), _pe_rows(lp['cmp_pe_v']),
      lp['cmp_w2_k'].astype(BF16), lp['cmp_w2_v'].astype(BF16),
      lp['nsa_k_norm_cmp'].reshape(1, NSA_HD), qn), nc


def _cmp_kv_prompt_body(*refs, n_sub):
    kc_refs, vc_refs = refs[:NSA_KV_HEADS], refs[NSA_KV_HEADS:2 * NSA_KV_HEADS]
    wk_ref, wv_ref, pek_ref, pev_ref, w2k_ref, w2v_ref, gk_ref, kcc_ref, vcc_ref = refs[2 * NSA_KV_HEADS:]

    def compress(x_ref, w_ref, pe_ref, w2_ref):
        x = jnp.concatenate([x_ref[0, pl.ds(s, n_sub, stride=CMP_STRIDE), :] for s in range(CMP_STRIDE)],
                            axis=1).astype(BF16)
        parts = jnp.dot(x, w_ref[...], preferred_element_type=F32)
        return _compress_from_parts(parts, pe_ref[...], w_ref[...], w2_ref[...])

    for g in range(NSA_KV_HEADS):
        kcc = _head_norm(compress(kc_refs[g], wk_ref, pek_ref, w2k_ref), gk_ref[...])
        kcc_ref[0, g] = kcc.astype(kcc_ref.dtype)
        vcc_ref[0, g] = compress(vc_refs[g], wv_ref, pev_ref, w2v_ref).astype(vcc_ref.dtype)


def compressed_kv_prompt(p, lp):
    b, s, _ = p.shape
    assert s % CMP_STRIDE == 0
    n_sub = s // CMP_STRIDE
    nc = n_sub - CMP_R + 1
    kdim = CMP_STRIDE * NSA_HD
    kv_w = NSA_KV // LANE
    const = lambda shape: pl.BlockSpec(shape, lambda bi: (0,) * len(shape))
    assert NSA_HD == LANE
    src = lambda cb: pl.BlockSpec((1, s, NSA_HD), lambda bi: (bi, 0, cb))
    out = pl.BlockSpec((1, NSA_KV_HEADS, n_sub, NSA_HD), lambda bi: (bi, 0, 0, 0))
    kcc, vcc = pl.pallas_call(
        functools.partial(_cmp_kv_prompt_body, n_sub=n_sub),
        grid=(b,),
        in_specs=[src(CB_KV + g) for g in range(NSA_KV_HEADS)] + [src(CB_KV + kv_w + g) for g in range(NSA_KV_HEADS)] + [
                  const((kdim, CMP_R * NSA_HD)), const((kdim, CMP_R * NSA_HD)),
                  const((Q_ROWS, kdim)), const((Q_ROWS, kdim)),
                  const((NSA_HD, NSA_HD)), const((NSA_HD, NSA_HD)), const((1, NSA_HD))],
        out_specs=[out, out],
        out_shape=[jax.ShapeDtypeStruct((b, NSA_KV_HEADS, n_sub, NSA_HD), BF16)] * 2,
        compiler_params=pltpu.CompilerParams(dimension_semantics=("arbitrary",),
                                             vmem_limit_bytes=VMEM_LIMIT_BYTES),
        name="compressed_kv_prompt",
    )(*([p] * (2 * NSA_KV_HEADS)), _subblock_weight(lp['cmp_w1_k']), _subblock_weight(lp['cmp_w1_v']),
      _pe_rows(lp['cmp_pe_k']), _pe_rows(lp['cmp_pe_v']),
      lp['cmp_w2_k'].astype(BF16), lp['cmp_w2_v'].astype(BF16), lp['nsa_k_norm_cmp'].reshape(1, NSA_HD))
    return kcc, vcc, nc


def _gdn_sample_body(qkv_ref, buf_ref, w_ref, sm_ref, z_ref, rec_ref, alog_ref, dt_ref, gn_ref,
                     o_ref, conv_ref, rec_out_ref):
    x = qkv_ref[0]
    buf = buf_ref[0]
    w = w_ref[...]
    acc = x * w[GDN_CONV - 1:GDN_CONV]
    for j in range(GDN_CONV - 1):
        acc = acc + buf[j:j + 1] * w[j:j + 1]
    for j in range(GDN_CONV - 2):
        conv_ref[0, j:j + 1, :] = buf[j + 1:j + 2]
    conv_ref[0, GDN_CONV - 2:GDN_CONV - 1, :] = x
    cv = acc * jax.nn.sigmoid(acc)
    g_all, beta_all = _gdn_gates(sm_ref[0], alog_ref[...], dt_ref[...])
    z = z_ref[0]
    rows = []
    heads = []
    for h in range(GDN_HEADS):
        q = cv[:, h * GDN_DK:(h + 1) * GDN_DK]
        k = cv[:, GDN_QK + h * GDN_DK:GDN_QK + (h + 1) * GDN_DK]
        v = cv[:, 2 * GDN_QK + h * GDN_DV:2 * GDN_QK + (h + 1) * GDN_DV]
        q = q * lax.rsqrt(jnp.sum(q * q, axis=-1, keepdims=True) + EPS) * (GDN_DK ** -0.5)
        k = k * lax.rsqrt(jnp.sum(k * k, axis=-1, keepdims=True) + EPS)
        g = g_all[:, SMALL_A + h:SMALL_A + h + 1]
        beta = beta_all[:, SMALL_B + h:SMALL_B + h + 1]
        eg = jnp.exp(g)
        heads.append((q, k, v * beta, eg, jnp.sum(q * k, axis=-1, keepdims=True)))
        rows += [q * eg, k * beta * eg, k]
    stacked = jnp.concatenate(rows, axis=0)
    eye = jnp.where(lax.broadcasted_iota(jnp.int32, (GDN_DK, GDN_DK), 0)
                    == lax.broadcasted_iota(jnp.int32, (GDN_DK, GDN_DK), 1), 1.0, 0.0).astype(BF16)
    cols = sum(lax.dot_general(eye, piece, NT_DIMS, preferred_element_type=F32) for piece in _split3(stacked))
    for h, (q, k, u, eg, attn) in enumerate(heads):
        s_old = rec_ref[0, h]
        qe_col, w_col, k_col = (cols[:, 3 * h + i:3 * h + i + 1] for i in range(3))
        v_new = u - jnp.sum(w_col * s_old, axis=0, keepdims=True)
        o = jnp.sum(qe_col * s_old, axis=0, keepdims=True) + attn * v_new
        rec_out_ref[0, h] = s_old * eg + k_col * v_new
        zc = z[:, h * GDN_DV:(h + 1) * GDN_DV]
        o_ref[0, :, h * GDN_DV:(h + 1) * GDN_DV] = _head_norm(o, gn_ref[...]) * (zc * jax.nn.sigmoid(zc))


def gdn_sample(p, conv_buf, rec, lp):
    db = p.shape[0]
    tok = lambda w, cb: pl.BlockSpec((1, 1, w), lambda b: (b, 0, cb * LANE // w))
    vec = pl.BlockSpec((1, LANE), lambda b: (0, 0))
    buf = pl.BlockSpec((1, GDN_CONV - 1, GDN_CONV_DIM), lambda b: (b, 0, 0))
    st = pl.BlockSpec((1, GDN_HEADS, GDN_DK, GDN_DV), lambda b: (b, 0, 0, 0))
    assert CB_QKV == 0 and (CB_Z * LANE) % GDN_V == 0
    return pl.pallas_call(
        _gdn_sample_body,
        grid=(db,),
        in_specs=[tok(GDN_CONV_DIM, CB_QKV), buf, pl.BlockSpec((GDN_CONV, GDN_CONV_DIM), lambda b: (0, 0)),
                  tok(LANE, CB_SMALL), tok(GDN_V, CB_Z), st, vec, vec, vec],
        out_specs=[pl.BlockSpec((1, 1, GDN_V), lambda b: (b, 0, 0)), buf, st],
        out_shape=[jax.ShapeDtypeStruct((db, 1, GDN_V), F32),
                   jax.ShapeDtypeStruct((db, GDN_CONV - 1, GDN_CONV_DIM), F32),
                   jax.ShapeDtypeStruct((db, GDN_HEADS, GDN_DK, GDN_DV), F32)],
        compiler_params=pltpu.CompilerParams(dimension_semantics=("arbitrary",),
                                             vmem_limit_bytes=VMEM_LIMIT_BYTES),
        name="gdn_sample",
    )(p, conv_buf, lp['gdn_conv_w'], p, p, rec,
      _lane_row(lp['gdn_a_log'], SMALL_A), _lane_row(lp['gdn_dt_bias'], SMALL_A),
      lp['gdn_out_norm'].reshape(1, GDN_DV))


def _merge_body(x_ref, oa_ref, oc_ref, os_ref, ow_ref, ga_ref, gb_ref, sm_ref, wo_ref, y_ref):
    gates = jax.nn.sigmoid(sm_ref[...])
    o_b = jnp.zeros(x_ref.shape, F32)
    for br, ref in enumerate((oc_ref, os_ref, ow_ref)):
        o = ref[...]
        o_b = o_b + jnp.concatenate(
            [gates[:, SMALL_G + br * NSA_HEADS + h:SMALL_G + br * NSA_HEADS + h + 1] * o[:, h * NSA_HD:(h + 1) * NSA_HD]
             for h in range(NSA_HEADS)], axis=1)
    m = jax.nn.sigmoid(ga_ref[...]) * oa_ref[...] + jax.nn.sigmoid(gb_ref[...]) * o_b
    y_ref[...] = x_ref[...] + jnp.dot(m.astype(BF16), wo_ref[...], preferred_element_type=F32)


def merge_branches_fused(x2, p2, o_a, o_c, o_s, o_w, w_o):
    n = x2.shape[0]
    tm = min(n, 512)
    row = pl.BlockSpec((tm, D_MODEL), lambda i: (i, 0))
    pcol = lambda w, cb: pl.BlockSpec((tm, w), lambda i: (i, cb * LANE // w))
    return pl.pallas_call(
        _merge_body,
        grid=(n // tm,),
        in_specs=[row, row, row, row, row, pcol(D_MODEL, CB_GA), pcol(D_MODEL, CB_GB), pcol(LANE, CB_SMALL),
                  _const_spec((D_MODEL, D_MODEL))],
        out_specs=row,
        out_shape=jax.ShapeDtypeStruct((n, D_MODEL), F32),
        compiler_params=pltpu.CompilerParams(dimension_semantics=("arbitrary",),
                                             vmem_limit_bytes=VMEM_LIMIT_BYTES),
        name="merge_branches",
    )(x2, o_a, o_c, o_s, o_w, p2, p2, p2, w_o.astype(BF16))


IDX_LANES = 128


def _select_sample_body(imp_ref, ovl_ref, idx_ref, *, q_pos, ns):
    imp = imp_ref[...]
    hi = imp.astype(BF16)
    lo = (imp - hi.astype(F32)).astype(BF16)
    score = (jnp.dot(hi, ovl_ref[...], preferred_element_type=F32)
             + jnp.dot(lo, ovl_ref[...], preferred_element_type=F32))
    rows, nsp = score.shape
    j = lax.broadcasted_iota(jnp.int32, (rows, nsp), 1)
    cur = q_pos // SLC_BLK
    valid = (j <= cur) & (j < ns)
    forced = (j == 0) | (j == cur) | (j == cur - 1)
    work = jnp.where(valid, jnp.where(forced, FORCE_SCORE, score), NEG_INF)
    slot = lax.broadcasted_iota(jnp.int32, (rows, IDX_LANES), 1)
    idx = jnp.full((rows, IDX_LANES), -1, jnp.int32)
    for it in range(min(SLC_TOPN, ns)):
        mx = jnp.max(work, axis=-1, keepdims=True)
        first = jnp.min(jnp.where(work == mx, j, nsp), axis=-1, keepdims=True)
        idx = jnp.where((slot == it) & (mx > NEG_INF), first, idx)
        work = jnp.where(j == first, NEG_INF, work)
    idx_ref[...] = idx


def select_blocks_sample(imp, nc, q_pos, n_keys):
    db, kv, ncp = imp.shape
    ns = -(-n_keys // SLC_BLK)
    nsp = -(-ns // 128) * 128
    cstart = np.arange(ncp) * CMP_STRIDE
    sstart = np.arange(nsp) * SLC_BLK
    ovl = ((cstart[:, None] < sstart[None, :] + SLC_BLK) & (cstart[:, None] + CMP_LEN > sstart[None, :])
           & (np.arange(ncp)[:, None] < nc) & (np.arange(nsp)[None, :] < ns))
    rows = db * kv
    idx = pl.pallas_call(
        functools.partial(_select_sample_body, q_pos=q_pos, ns=ns),
        grid=(1,),
        in_specs=[pl.BlockSpec((rows, ncp), lambda i: (0, 0)), pl.BlockSpec((ncp, nsp), lambda i: (0, 0))],
        out_specs=pl.BlockSpec((rows, IDX_LANES), lambda i: (0, 0)),
        out_shape=jax.ShapeDtypeStruct((rows, IDX_LANES), jnp.int32),
        compiler_params=pltpu.CompilerParams(vmem_limit_bytes=VMEM_LIMIT_BYTES),
        name="select_blocks_sample",
    )(imp.reshape(rows, ncp), jnp.asarray(ovl, dtype=BF16))
    return idx.reshape(db, kv, IDX_LANES)


def _token_softmax_attend(q8, s_past, allowed, v_past, k_new, v_new, new_ok):
    s_new = jnp.sum(q8.astype(F32) * k_new, axis=-1, keepdims=True)
    sm = jnp.where(allowed, s_past, NEG_INF)
    m = jnp.maximum(jnp.max(sm, axis=-1, keepdims=True), jnp.where(new_ok, s_new, NEG_INF))
    m = jnp.where(m > NEG_INF, m, 0.0)
    p = jnp.exp(sm - m)
    p_new = jnp.where(new_ok, jnp.exp(s_new - m), 0.0)
    l = jnp.sum(p, axis=-1, keepdims=True) + p_new
    o = jnp.dot(p.astype(BF16), v_past, preferred_element_type=F32) + p_new * v_new
    return o / jnp.maximum(l, 1e-30)


def _slc_sample_body(idx_ref, pt_ref, *refs, n_past):
    n = SLC_TOPN
    k_blocks, v_blocks = refs[:n], refs[n:2 * n]
    q_ref, kn_ref, vn_ref, o_ref = refs[2 * n:]
    b, g = pl.program_id(0), pl.program_id(1)
    q8 = _token_group_queries_half(q_ref[0])

    def group_rows(blocks):
        sel_g = jnp.full((SLC_TOPN * SLC_BLK, NSA_HD), g, jnp.int32)
        per_group = [jnp.concatenate([r[0, pl.ds(gg, SLC_BLK, stride=NSA_KV_HEADS), :] for r in blocks], axis=0)
                     for gg in range(NSA_KV_HEADS)]
        out = per_group[0]
        for gg in range(1, NSA_KV_HEADS):
            out = jnp.where(sel_g == gg, per_group[gg], out)
        return out.astype(BF16)

    k = group_rows(k_blocks)
    v = group_rows(v_blocks)
    s = lax.dot_general(q8, k, NT_DIMS, preferred_element_type=F32)
    slot = lax.broadcasted_iota(jnp.int32, s.shape, 1) // SLC_BLK
    slot_ok = jnp.zeros(s.shape, jnp.int32)
    n_new = jnp.int32(0)
    for i in range(n):
        ji = idx_ref[(b * NSA_KV_HEADS + g) * n + i]
        slot_ok = jnp.where(slot == i, jnp.where((ji >= 0) & (ji < n_past), 1, 0), slot_ok)
        n_new = n_new + jnp.where(ji >= n_past, 1, 0)
    new_ok = jnp.full((Q_ROWS, 1), n_new, jnp.int32) > 0
    o = _token_softmax_attend(q8, s, slot_ok > 0, v, kn_ref[0], vn_ref[0], new_ok)
    for h in range(NSA_HPG):
        o_ref[0, :, h * NSA_HD:(h + 1) * NSA_HD] = o[h:h + 1]


def _token_group_queries_half(q):
    rows = [q[:, h * NSA_HD:(h + 1) * NSA_HD] for h in range(NSA_HPG)]
    return _pad_rows(jnp.concatenate(rows, axis=0), Q_ROWS)


def slc_attention_sample(qr, idx, pool_k, pool_v, page_table, k_new, v_new):
    db, n_pages = page_table.shape
    n_pool = pool_k.shape[0]
    bpp = PAGE_SIZE // SLC_BLK
    n_past = n_pages * bpp
    n = SLC_TOPN
    gw = NSA_HPG * NSA_HD
    idx_flat = idx[:, :, :n].reshape(-1)
    pt_flat = page_table.reshape(-1)
    view = lambda p: p.reshape(n_pool * bpp, SLC_BLK * NSA_KV_HEADS, NSA_HD)

    def blk_spec(i):
        def index_map(b, g, idx_s, pt_s):
            jc = jnp.clip(idx_s[(b * NSA_KV_HEADS + g) * n + i], 0, n_past - 1)
            return (pt_s[b * n_pages + jc // bpp] * bpp + jc % bpp, 0, 0)
        return pl.BlockSpec((1, SLC_BLK * NSA_KV_HEADS, NSA_HD), index_map)

    grid_spec = pltpu.PrefetchScalarGridSpec(
        num_scalar_prefetch=2,
        grid=(db, NSA_KV_HEADS),
        in_specs=([blk_spec(i) for i in range(n)] + [blk_spec(i) for i in range(n)]
                  + [pl.BlockSpec((1, 1, gw), lambda b, g, *_: (b, 0, g)),
                     pl.BlockSpec((1, 1, NSA_HD), lambda b, g, *_: (b, 0, g)),
                     pl.BlockSpec((1, 1, NSA_HD), lambda b, g, *_: (b, 0, g))]),
        out_specs=pl.BlockSpec((1, 1, gw), lambda b, g, *_: (b, 0, g)),
    )
    return pl.pallas_call(
        functools.partial(_slc_sample_body, n_past=n_past),
        grid_spec=grid_spec,
        out_shape=jax.ShapeDtypeStruct((db, 1, NSA_Q), F32),
        compiler_params=pltpu.CompilerParams(dimension_semantics=("arbitrary", "arbitrary"),
                                             vmem_limit_bytes=VMEM_LIMIT_BYTES),
        name="slc_attention_sample",
    )(idx_flat, pt_flat, *([view(pool_k)] * n), *([view(pool_v)] * n), qr, k_new, v_new)


def _win_sample_body(q_ref, bk_ref, bv_ref, kn_ref, vn_ref, o_ref, nk_ref, nv_ref, *, q_pos, buf_start):
    rows = bk_ref.shape[1]
    wb = rows // NSA_KV_HEADS
    kn, vn = kn_ref[0], vn_ref[0]
    q = q_ref[0]
    kpos = buf_start + lax.broadcasted_iota(jnp.int32, (Q_ROWS, wb), 1)
    allowed = (kpos <= q_pos) & (kpos > q_pos - WINDOW)
    new_ok = jnp.ones((Q_ROWS, 1), jnp.bool_)
    for g in range(NSA_KV_HEADS):
        gcols = slice(g * NSA_HD, (g + 1) * NSA_HD)
        q8 = _token_group_queries(q, g)
        kg = bk_ref[0, pl.ds(g, wb, stride=NSA_KV_HEADS), :].astype(BF16)
        vg = bv_ref[0, pl.ds(g, wb, stride=NSA_KV_HEADS), :].astype(BF16)
        s = lax.dot_general(q8, kg, NT_DIMS, preferred_element_type=F32)
        o = _token_softmax_attend(q8, s, allowed, vg, kn[:, gcols], vn[:, gcols], new_ok)
        for h in range(NSA_HPG):
            c0 = (g * NSA_HPG + h) * NSA_HD
            o_ref[0, :, c0:c0 + NSA_HD] = o[h:h + 1]
    row = lax.broadcasted_iota(jnp.int32, (rows, NSA_HD), 0)

    def shifted(buf_ref, new):
        out = pltpu.roll(buf_ref[0], rows - NSA_KV_HEADS, 0)
        for g in range(NSA_KV_HEADS):
            out = jnp.where(row == rows - NSA_KV_HEADS + g, new[:, g * NSA_HD:(g + 1) * NSA_HD], out)
        return out

    nk_ref[0] = shifted(bk_ref, kn)
    nv_ref[0] = shifted(bv_ref, vn)


def win_attention_sample(qr, buf_k, buf_v, k_new, v_new, q_pos):
    db, rows, _ = buf_k.shape
    wb = rows // NSA_KV_HEADS
    tok = lambda w: pl.BlockSpec((1, 1, w), lambda b: (b, 0, 0))
    buf = pl.BlockSpec((1, rows, NSA_HD), lambda b: (b, 0, 0))
    return pl.pallas_call(
        functools.partial(_win_sample_body, q_pos=q_pos, buf_start=q_pos - wb),
        grid=(db,),
        in_specs=[tok(NSA_Q), buf, buf, tok(NSA_KV), tok(NSA_KV)],
        out_specs=[tok(NSA_Q), buf, buf],
        out_shape=[jax.ShapeDtypeStruct((db, 1, NSA_Q), F32), jax.ShapeDtypeStruct((db, rows, NSA_HD), F32),
                   jax.ShapeDtypeStruct((db, rows, NSA_HD), F32)],
        compiler_params=pltpu.CompilerParams(dimension_semantics=("arbitrary",),
                                             vmem_limit_bytes=VMEM_LIMIT_BYTES),
        name="win_attention_sample",
    )(qr, buf_k, buf_v, k_new, v_new)


def _kv_cols(p, i):
    c0 = CB_KV * LANE + i * NSA_KV
    return p[..., c0:c0 + NSA_KV]


def token_mix_prompt(x, lp):
    b, s, _ = x.shape
    pos = jnp.arange(s)
    x2 = x.reshape(b * s, D_MODEL)
    p2 = mixer_projection(x2, lp['mix_norm'], lp['w_in'])
    p = p2.reshape(b, s, P_COLS)
    o_a, rec_new = gdn_prompt(p, lp)
    assert s >= GDN_CONV - 1
    conv_new = p[:, s - (GDN_CONV - 1):, :GDN_CONV_DIM]
    heads = lambda a_: a_.reshape(b, -1, NSA_KV_HEADS, NSA_HD)
    qn, qr, ks_r, kw_r = nsa_prep(p, pos, lp)
    kcc, vcc, nc = compressed_kv_prompt(p, lp)
    o_c, sel = cmp_select_prompt(qn, kcc, vcc, nc)
    vs, vw = _kv_cols(p, 3), _kv_cols(p, 5)
    o_s = slc_attention_prompt(qr, sel, ks_r.astype(BF16), vs.astype(BF16))
    o_w = win_attention_prompt(qr, kw_r.astype(BF16), vw.astype(BF16))
    flat = lambda a_: a_.reshape(b * s, -1)
    y = merge_branches_fused(x2, p2, flat(o_a), flat(o_c), flat(o_s), flat(o_w), lp['w_o']).reshape(b, s, D_MODEL)
    wb = min(WINDOW, s)
    return y, (conv_new, rec_new, heads(_kv_cols(p, 0)), heads(_kv_cols(p, 1)), heads(ks_r), heads(vs),
               heads(kw_r[:, s - wb:]), heads(vw[:, s - wb:]))


def token_mix_sample(x, layer, conv_buf, rec, cache_k_cmp, cache_v_cmp, cache_k_slc, cache_v_slc,
                     win_k, win_v, page_table, lp):
    db, t, _ = x.shape
    assert t == 1 and t < CMP_STRIDE
    q_pos = PAST_LEN
    x2 = x.reshape(db, D_MODEL)
    p2 = mixer_projection(x2, lp['mix_norm'], lp['w_in'])
    p = p2.reshape(db, 1, P_COLS)
    o_a, conv_new, rec_new = gdn_sample(p, conv_buf, rec, lp)
    heads = lambda a_: a_.reshape(db, -1, NSA_KV_HEADS, NSA_HD)
    qn, qr, ks_r, kw_r = [a_.reshape(db, 1, -1) for a_ in
                          nsa_prep(p2.reshape(1, db, P_COLS), jnp.full((db,), q_pos), lp)]
    (o_c, imp), nc = cmp_attention_sample(qn, cache_k_cmp[layer], cache_v_cmp[layer], page_table, lp, q_pos)
    idx = select_blocks_sample(imp, nc, q_pos, PAST_LEN + t)
    vs, vw = _kv_cols(p, 3), _kv_cols(p, 5)
    o_s = slc_attention_sample(qr, idx, cache_k_slc[layer], cache_v_slc[layer], page_table, ks_r, vs)
    wb = win_k.shape[1]
    o_w, win_k_new, win_v_new = win_attention_sample(
        qr, win_k.reshape(db, wb * NSA_KV_HEADS, NSA_HD), win_v.reshape(db, wb * NSA_KV_HEADS, NSA_HD),
        kw_r, vw, q_pos)
    flat = lambda a_: a_.reshape(db, -1)
    y = merge_branches_fused(x2, p2, flat(o_a), flat(o_c), flat(o_s), flat(o_w), lp['w_o']).reshape(db, 1, D_MODEL)
    return y, (conv_new, rec_new, heads(_kv_cols(p, 0)), heads(_kv_cols(p, 1)), heads(ks_r), heads(vs),
               heads(win_k_new), heads(win_v_new))


def kernel(x_prompt, x_sample, state_gdn_conv, state_gdn_rec, cache_k_cmp, cache_v_cmp, cache_k_slc, cache_v_slc, cache_k_win, cache_v_win, page_table, ffn1_norm, ffn1_w_in, ffn1_w_out, mix_norm, w_in, gdn_conv_w, gdn_a_log, gdn_dt_bias, gdn_out_norm, nsa_q_norm, nsa_k_norm_cmp, nsa_k_norm_slc, nsa_k_norm_win, cmp_w1_k, cmp_w2_k, cmp_pe_k, cmp_w1_v, cmp_w2_v, cmp_pe_v, w_o, ffn2_norm, ffn2_w_in, ffn2_w_out):
    depth = w_in.shape[0]
    yp, ys = x_prompt, x_sample
    p_list, s_list = [], []
    for l in range(depth):
        lp = {'mix_norm': mix_norm[l], 'w_in': w_in[l], 'gdn_conv_w': gdn_conv_w[l],
              'gdn_a_log': gdn_a_log[l], 'gdn_dt_bias': gdn_dt_bias[l], 'gdn_out_norm': gdn_out_norm[l],
              'nsa_q_norm': nsa_q_norm[l], 'nsa_k_norm_cmp': nsa_k_norm_cmp[l],
              'nsa_k_norm_slc': nsa_k_norm_slc[l], 'nsa_k_norm_win': nsa_k_norm_win[l],
              'cmp_w1_k': cmp_w1_k[l], 'cmp_w2_k': cmp_w2_k[l], 'cmp_pe_k': cmp_pe_k[l],
              'cmp_w1_v': cmp_w1_v[l], 'cmp_w2_v': cmp_w2_v[l], 'cmp_pe_v': cmp_pe_v[l], 'w_o': w_o[l]}
        yp = swiglu_half(yp, ffn1_norm[l], ffn1_w_in[l], ffn1_w_out[l])
        yp, st_p = token_mix_prompt(yp, lp)
        yp = swiglu_half(yp, ffn2_norm[l], ffn2_w_in[l], ffn2_w_out[l])
        ys = swiglu_half(ys, ffn1_norm[l], ffn1_w_in[l], ffn1_w_out[l])
        ys, st_s = token_mix_sample(ys, l, state_gdn_conv[l], state_gdn_rec[l], cache_k_cmp, cache_v_cmp,
                                    cache_k_slc, cache_v_slc, cache_k_win[l], cache_v_win[l], page_table, lp)
        ys = swiglu_half(ys, ffn2_norm[l], ffn2_w_in[l], ffn2_w_out[l])
        p_list.append(st_p)
        s_list.append(st_s)
    (p_conv, p_rec, p_kc, p_vc, p_ks, p_vs, p_kw, p_vw) = [jnp.stack(a) for a in zip(*p_list)]
    (s_conv, s_rec, s_kc, s_vc, s_ks, s_vs, s_kw, s_vw) = [jnp.stack(a) for a in zip(*s_list)]
    return (yp, ys, p_conv, p_rec, p_kc, p_vc, p_ks, p_vs, p_kw, p_vw,
            s_conv, s_rec, s_kc, s_vc, s_ks, s_vs, s_kw, s_vw)
```
